```python
import jax
import jax.numpy as jnp
from jax import lax
import numpy as np

D_MODEL = 1024
BATCH = 2
SEQ = 8192
DEPTH = 2
DEC_BATCH = 2
DEC_SEQ = 16384
PAST_LEN = 128

BRANCH_W = 256
N_BRANCH = 4
EPS = 1e-6
NEG_INF = -1e30

HEAD_DIM = 64
ATTN_GROUPS = ((128, 1), (512, 4), (2048, 16))
N_GROUPS = 3
HEADS_PER_GROUP = 4
N_ATTN_HEADS = N_GROUPS * HEADS_PER_GROUP
ATTN_QKV = N_ATTN_HEADS * HEAD_DIM
HALF_KEYS = 64
Q_BLOCK = 128

POOL_WINDOWS = (2, 4, 8, 16)
POOL_GROUP = BRANCH_W // 4

CHUNK = 128
SG_GROUPS = 4
SG_GROUP_W = BRANCH_W // SG_GROUPS

RWKV_HEADS = 4
RWKV_N = BRANCH_W // RWKV_HEADS
DECAY_RANK = 64
AAA_RANK = 64
LAT_W = DECAY_RANK + AAA_RANK
N_DIRS = 2
GN_EPS = 64e-5

SEG_WIDTHS = (3 * ATTN_QKV, BRANCH_W, BRANCH_W, BRANCH_W, 2 * BRANCH_W, BRANCH_W, 3 * BRANCH_W, N_DIRS * LAT_W, BRANCH_W, N_BRANCH * D_MODEL)
PROJ_WIDTH = 3 * ATTN_QKV + 10 * BRANCH_W + N_DIRS * LAT_W + N_BRANCH * D_MODEL

kernel_name = "hybrid_bidir_encoder"


def rms_norm(x, g):
    xf = x.astype(jnp.float32)
    y = xf * lax.rsqrt(jnp.mean(xf * xf, axis=-1, keepdims=True) + EPS)
    return (y * g.astype(jnp.float32)).astype(x.dtype)


def dilated_attention(q, k, v, q_norm_g, k_norm_g):
    B, S, _ = q.shape

    def split_heads(t):
        return t.reshape(B, S, N_GROUPS, HEADS_PER_GROUP, HEAD_DIM).transpose(0, 2, 3, 1, 4)

    qh = split_heads(rms_norm(q.reshape(B, S, N_ATTN_HEADS, HEAD_DIM), q_norm_g)) * (HEAD_DIM ** -0.5)
    kh = split_heads(rms_norm(k.reshape(B, S, N_ATTN_HEADS, HEAD_DIM), k_norm_g))
    vh = split_heads(v)
    dil = jnp.array([d for _, d in ATTN_GROUPS], dtype=jnp.int32)
    offsets = dil[:, None] * jnp.arange(-HALF_KEYS, HALF_KEYS + 1, dtype=jnp.int32)[None, :]
    slopes = 2.0 ** (-8.0 * jnp.arange(1, N_ATTN_HEADS + 1, dtype=jnp.float32) / N_ATTN_HEADS)
    bias = -slopes.reshape(N_GROUPS, HEADS_PER_GROUP)[:, :, None] * jnp.abs(offsets).astype(jnp.float32)[:, None, :]
    gather = jax.vmap(lambda t, ii: t[:, :, ii], in_axes=(1, 0), out_axes=1)

    def attend_block(i):
        start = i * Q_BLOCK
        pos = start + jnp.arange(Q_BLOCK, dtype=jnp.int32)
        idx = pos[None, :, None] + offsets[:, None, :]
        valid = (idx >= 0) & (idx < S)
        idx = jnp.clip(idx, 0, S - 1)
        qb = lax.dynamic_slice_in_dim(qh, start, Q_BLOCK, axis=3)
        kb = gather(kh, idx)
        vb = gather(vh, idx)
        s = jnp.einsum('bghqd,bghqwd->bghqw', qb, kb).astype(jnp.float32) + bias[None, :, :, None, :]
        s = jnp.where(valid[None, :, None], s, NEG_INF)
        lse = jax.nn.logsumexp(s, axis=-1)
        p = jnp.exp(s - lse[..., None])
        o = jnp.einsum('bghqw,bghqwd->bghqd', p.astype(vb.dtype), vb)
        alpha = jax.nn.softmax(lse, axis=1)
        return jnp.einsum('bghq,bghqd->bhqd', alpha.astype(o.dtype), o)

    out = lax.map(attend_block, jnp.arange(S // Q_BLOCK, dtype=jnp.int32))
    return out.transpose(1, 0, 3, 2, 4).reshape(B, S, HEADS_PER_GROUP * HEAD_DIM)


def pool_mixer(u, pool_w, pool_scale):
    B, S, _ = u.shape
    uf = u.astype(jnp.float32)
    t = jnp.arange(S)
    diffs = []
    for gi, w in enumerate(POOL_WINDOWS):
        h = w // 2
        ug = uf[..., gi * POOL_GROUP:(gi + 1) * POOL_GROUP]
        padded = jnp.pad(ug, ((0, 0), (h, h), (0, 0)))
        cs = jnp.concatenate([jnp.zeros_like(padded[:, :1]), jnp.cumsum(padded, axis=1)], axis=1)
        win_sum = cs[:, 2 * h:2 * h + S] - cs[:, :S]
        count = (jnp.minimum(t + h, S) - jnp.maximum(t - h, 0)).astype(jnp.float32)
        diffs.append(win_sum / count[None, :, None] - ug)
    d = jnp.stack(diffs, axis=2)
    y = jnp.einsum('bsgc,gce->bsge', d, pool_w.astype(jnp.float32)).reshape(B, S, BRANCH_W)
    return (y * pool_scale.astype(jnp.float32)).astype(u.dtype)


def spatial_gating(uv, sg_norm_g, sg_w, sg_b):
    B, S, _ = uv.shape
    u, v = jnp.split(uv, 2, axis=-1)
    v = rms_norm(v, sg_norm_g).reshape(B, S // CHUNK, CHUNK, SG_GROUPS, SG_GROUP_W)
    sv = jnp.einsum('gts,bcsgd->bctgd', sg_w, v) + sg_b.T[None, None, :, :, None]
    return u * sv.reshape(B, S, BRANCH_W)


def token_shift(x, direction):
    if direction == 0:
        return jnp.pad(x, ((0, 0), (1, 0), (0, 0)))[:, :-1]
    return jnp.pad(x, ((0, 0), (0, 1), (0, 0)))[:, 1:]


def rwkv_step(state, inp):
    r, decay, k, v, kk, a = inp[0], inp[1], inp[2], inp[3], inp[4], inp[5]
    sa = jnp.einsum('...ij,...j->...i', state, -kk)
    state = state * decay[..., None, :] + sa[..., :, None] * (kk * a)[..., None, :] + v[..., :, None] * k[..., None, :]
    return state, jnp.einsum('...ij,...j->...i', state, r)


def rwkv7_bidirectional(rkv, lat, mu_rkv, mu_lat, w0, w_up, a0, a_up, k_k, k_a, r_k, ln_g, ln_b):
    B, S, _ = rkv.shape
    dtype = rkv.dtype
    rkv = rkv.astype(jnp.float32)
    lat = lat.astype(jnp.float32)

    def heads(t):
        return t.reshape(t.shape[:-1] + (RWKV_HEADS, RWKV_N))

    lat_dirs = jnp.split(lat, N_DIRS, axis=-1)
    seqs, bonus = [], []
    for d in range(N_DIRS):
        xr = rkv + (token_shift(rkv, d) - rkv) * mu_rkv[d]
        xl = lat_dirs[d] + (token_shift(lat_dirs[d], d) - lat_dirs[d]) * mu_lat[d]
        r, k, v = (heads(t) for t in jnp.split(xr, 3, axis=-1))
        lw, la = jnp.split(xl, [DECAY_RANK], axis=-1)
        w_log = -jax.nn.softplus(-(w0[d] + jnp.tanh(lw) @ w_up[d])) - 0.5
        decay = heads(jnp.exp(-jnp.exp(w_log)))
        a = heads(jax.nn.sigmoid(a0[d] + la @ a_up[d]))
        kk = k * heads(k_k[d])
        kk = kk * lax.rsqrt(jnp.sum(kk * kk, axis=-1, keepdims=True) + 1e-12)
        k = k * (1.0 + (a - 1.0) * heads(k_a[d]))
        bonus.append((jnp.sum(r * k * heads(r_k[d]), axis=-1, keepdims=True) * v).reshape(B, S, BRANCH_W))
        step_in = jnp.stack([r, decay, k, v, kk, a], axis=0)
        if d == 1:
            step_in = step_in[:, :, ::-1]
        seqs.append(step_in)
    xs = jnp.moveaxis(jnp.stack(seqs, axis=1), 3, 0)
    state0 = jnp.zeros((N_DIRS, B, RWKV_HEADS, RWKV_N, RWKV_N), jnp.float32)
    _, o = lax.scan(rwkv_step, state0, xs)
    o = jnp.moveaxis(o, 0, 2)
    outs = (o[0], o[1][:, ::-1])
    y = 0.0
    for d in range(N_DIRS):
        od = outs[d]
        mu = jnp.mean(od, axis=-1, keepdims=True)
        var = jnp.mean(jnp.square(od - mu), axis=-1, keepdims=True)
        on = ((od - mu) * lax.rsqrt(var + GN_EPS)).reshape(B, S, BRANCH_W) * ln_g + ln_b
        y = y + on + bonus[d]
    return y.astype(dtype)


def encoder_layer(x, norm_g, w_in, q_norm_g, k_norm_g, pool_w, pool_scale, sg_norm_g, sg_w, sg_b,
                  mu_rkv, mu_lat, w0, w_up, a0, a_up, k_k, k_a, r_k, ln_g, ln_b, w_branch, w_out):
    B, S, _ = x.shape
    h = rms_norm(x, norm_g)
    proj = jnp.einsum('bsd,de->bse', h, w_in)
    split_points = [int(p) for p in np.cumsum(SEG_WIDTHS)[:-1]]
    qkv, z_a, u_b, z_b, uv_c, z_c, rkv_d, lat_d, z_d, gate_logits = jnp.split(proj, split_points, axis=-1)
    q, k, v = jnp.split(qkv, 3, axis=-1)
    y_a = dilated_attention(q, k, v, q_norm_g, k_norm_g)
    y_b = pool_mixer(u_b, pool_w, pool_scale)
    y_c = spatial_gating(uv_c, sg_norm_g, sg_w, sg_b)
    y_d = rwkv7_bidirectional(rkv_d, lat_d, mu_rkv, mu_lat, w0, w_up, a0, a_up, k_k, k_a, r_k, ln_g, ln_b)
    ys = jnp.stack([(y * jax.nn.silu(z)).astype(x.dtype)
                    for y, z in ((y_a, z_a), (y_b, z_b), (y_c, z_c), (y_d, z_d))], axis=2)
    branch = jnp.einsum('bsnc,ncd->bsnd', ys, w_branch)
    gates = jax.nn.sigmoid(gate_logits.reshape(B, S, N_BRANCH, D_MODEL))
    merged = jnp.sum(gates * branch, axis=2)
    return x + jnp.einsum('bsd,de->bse', merged, w_out)


def encoder_trunk(x, params):
    for l in range(DEPTH):
        x = encoder_layer(x, *[p[l] for p in params])
    return x


def setup_inputs(seed: int = 0) -> dict:
    key = jax.random.key(seed)
    ks = jax.random.split(key, 24)
    L = DEPTH

    def nrm(k, shape, scale):
        return jax.random.normal(k, shape, jnp.float32) * scale

    return {
        "x_prompt": nrm(ks[0], (BATCH, SEQ, D_MODEL), 1.0),
        "x_sample": nrm(ks[1], (DEC_BATCH, DEC_SEQ, D_MODEL), 1.0),
        "norm_g": 1.0 + nrm(ks[2], (L, D_MODEL), 0.02),
        "w_in": nrm(ks[3], (L, D_MODEL, PROJ_WIDTH), D_MODEL ** -0.5),
        "q_norm_g": 1.0 + nrm(ks[4], (L, HEAD_DIM), 0.02),
        "k_norm_g": 1.0 + nrm(ks[5], (L, HEAD_DIM), 0.02),
        "pool_w": nrm(ks[6], (L, len(POOL_WINDOWS), POOL_GROUP, POOL_GROUP), POOL_GROUP ** -0.5),
        "pool_scale": 1.0 + nrm(ks[7], (L, BRANCH_W), 0.02),
        "sg_norm_g": 1.0 + nrm(ks[8], (L, BRANCH_W), 0.02),
        "sg_w": nrm(ks[9], (L, SG_GROUPS, CHUNK, CHUNK), CHUNK ** -0.5),
        "sg_b": 1.0 + nrm(ks[10], (L, SG_GROUPS, CHUNK), 0.02),
        "mu_rkv": jax.random.uniform(ks[11], (L, N_DIRS, 3 * BRANCH_W), jnp.float32),
        "mu_lat": jax.random.uniform(ks[12], (L, N_DIRS, LAT_W), jnp.float32),
        "w0": jax.random.uniform(ks[13], (L, N_DIRS, BRANCH_W), jnp.float32, minval=-6.0, maxval=1.0),
        "w_up": nrm(ks[14], (L, N_DIRS, DECAY_RANK, BRANCH_W), 0.1),
        "a0": nrm(ks[15], (L, N_DIRS, BRANCH_W), 0.1),
        "a_up": nrm(ks[16], (L, N_DIRS, AAA_RANK, BRANCH_W), 0.1),
        "k_k": 0.85 + nrm(ks[17], (L, N_DIRS, BRANCH_W), 0.02),
        "k_a": 1.0 + nrm(ks[18], (L, N_DIRS, BRANCH_W), 0.02),
        "r_k": nrm(ks[19], (L, N_DIRS, BRANCH_W), 0.1),
        "ln_g": 1.0 + nrm(ks[20], (L, BRANCH_W), 0.02),
        "ln_b": nrm(ks[21], (L, BRANCH_W), 0.02),
        "w_branch": nrm(ks[22], (L, N_BRANCH, BRANCH_W, D_MODEL), BRANCH_W ** -0.5),
        "w_out": nrm(ks[23], (L, D_MODEL, D_MODEL), D_MODEL ** -0.5),
    }


def reference(x_prompt, x_sample, norm_g, w_in, q_norm_g, k_norm_g, pool_w, pool_scale, sg_norm_g, sg_w, sg_b,
              mu_rkv, mu_lat, w0, w_up, a0, a_up, k_k, k_a, r_k, ln_g, ln_b, w_branch, w_out):
    params = (norm_g, w_in, q_norm_g, k_norm_g, pool_w, pool_scale, sg_norm_g, sg_w, sg_b,
              mu_rkv, mu_lat, w0, w_up, a0, a_up, k_k, k_a, r_k, ln_g, ln_b, w_branch, w_out)
    y_prompt = encoder_trunk(x_prompt, params)
    y_sample = encoder_trunk(x_sample, params)
    return (y_prompt, y_sample)
```

```python
import functools

import jax
import jax.numpy as jnp
import numpy as np
from jax import lax
from jax.experimental import pallas as pl
from jax.experimental.pallas import tpu as pltpu

F32 = jnp.float32
BF16 = jnp.bfloat16

D_MODEL = 1024
BRANCH_W = 256
N_BRANCH = 4
EPS = 1e-6
NEG_INF = -1e30

HEAD_DIM = 64
ATTN_DILATIONS = (1, 4, 16)
N_GROUPS = 3
HEADS_PER_GROUP = 4
N_ATTN_HEADS = N_GROUPS * HEADS_PER_GROUP
ATTN_QKV = N_ATTN_HEADS * HEAD_DIM
HALF_KEYS = 64
Q_SUB = 128
K_WIN = Q_SUB + 2 * HALF_KEYS

POOL_HALF = (1, 2, 4, 8)
POOL_HALO = 8
CHUNK = 128
SG_GROUPS = 4

RWKV_HEADS = 4
RWKV_N = 64
RWKV_CHUNK = 64
LAT_W = 128
GN_EPS = 64e-5
NEUMANN_DOUBLINGS = 5

QKV_W = 3 * ATTN_QKV
RKV_W = 3 * BRANCH_W
BC_W = 3 * BRANCH_W
Z_W = 4 * BRANCH_W
PROJ_SMALL_W = QKV_W + RKV_W + 2 * LAT_W + BC_W + Z_W

PROJ_ROWS = 256
MERGE_ROWS = 256
ATTN_ROWS = 512
RWKV_ROWS = 512
VMEM_LIMIT = 56 * 1024 * 1024


def _params(*sem):
    return pltpu.CompilerParams(dimension_semantics=sem, vmem_limit_bytes=VMEM_LIMIT)


def _dot(a, b):
    return jnp.dot(a, b, preferred_element_type=F32)


def _split_bf16(x):
    hi = x.astype(BF16)
    lo = (x - hi.astype(F32)).astype(BF16)
    return hi, lo


def _dot_split_lhs(x, m_bf16):
    hi, lo = _split_bf16(x)
    return _dot(hi, m_bf16) + _dot(lo, m_bf16)


def _dot_split_both(a, b):
    a_hi, a_lo = _split_bf16(a)
    b_hi, b_lo = _split_bf16(b)
    return _dot(a_hi, b_hi) + _dot(a_lo, b_hi) + _dot(a_hi, b_lo)


def _lane_head(rows):
    return lax.broadcasted_iota(jnp.int32, (rows, BRANCH_W), 1) // HEAD_DIM


def _proj_kernel(x_ref, g_ref, w_ref, qg_ref, kg_ref, bd_ref, qkv_ref, rkv_ref, lat_ref, bc_ref, z_ref):
    x = x_ref[...]
    ms = jnp.mean(x * x, axis=-1, keepdims=True)
    h = (x * lax.rsqrt(ms + EPS) * g_ref[...]).astype(BF16)
    bd = bd_ref[...]
    for c in range(2 * N_GROUPS):
        lo = c * BRANCH_W
        y = _dot(h, w_ref[:, lo:lo + BRANCH_W])
        msq = _dot_split_lhs(y * y, bd)
        if c < N_GROUPS:
            yn = y * lax.rsqrt(msq + EPS) * qg_ref[...] * (HEAD_DIM ** -0.5)
        else:
            yn = y * lax.rsqrt(msq + EPS) * kg_ref[...]
        qkv_ref[:, lo:lo + BRANCH_W] = yn.astype(BF16)
    qkv_ref[:, 2 * ATTN_QKV:QKV_W] = _dot(h, w_ref[:, 2 * ATTN_QKV:QKV_W]).astype(BF16)
    o = QKV_W
    rkv_ref[...] = _dot(h, w_ref[:, o:o + RKV_W])
    o += RKV_W
    lat_ref[...] = _dot(h, w_ref[:, o:o + 2 * LAT_W])
    o += 2 * LAT_W
    bc_ref[...] = _dot(h, w_ref[:, o:o + BC_W])
    o += BC_W
    z_ref[...] = _dot(h, w_ref[:, o:o + Z_W])


def _proj(x2d, lp):
    rows = x2d.shape[0]
    tm = PROJ_ROWS
    const = lambda i: (0, 0)
    row = lambda i: (i, 0)
    widths = (QKV_W, RKV_W, 2 * LAT_W, BC_W, Z_W)
    dtypes = (BF16, F32, F32, F32, F32)
    return pl.pallas_call(
        _proj_kernel,
        grid=(rows // tm,),
        in_specs=[
            pl.BlockSpec((tm, D_MODEL), row),
            pl.BlockSpec((1, D_MODEL), const),
            pl.BlockSpec((D_MODEL, PROJ_SMALL_W), const),
            pl.BlockSpec((1, BRANCH_W), const),
            pl.BlockSpec((1, BRANCH_W), const),
            pl.BlockSpec((BRANCH_W, BRANCH_W), const),
        ],
        out_specs=[pl.BlockSpec((tm, w), row) for w in widths],
        out_shape=[jax.ShapeDtypeStruct((rows, w), dt) for w, dt in zip(widths, dtypes)],
        compiler_params=_params("parallel"),
        name="proj",
    )(x2d, lp["norm_g"], lp["w_small"], lp["q_gain"], lp["k_gain"], lp["bd_mean"])


def _attn_kernel(q_ref, kp_ref, k_ref, kn_ref, vp_ref, v_ref, vn_ref, bias_ref, o_ref, lse_ref,
                 kbuf, vbuf, *, tq, seq_m):
    i = pl.program_id(2)
    kbuf[0:HALF_KEYS, :] = kp_ref[0]
    kbuf[HALF_KEYS:HALF_KEYS + tq, :] = k_ref[0]
    kbuf[HALF_KEYS + tq:, :] = kn_ref[0]
    vbuf[0:HALF_KEYS, :] = vp_ref[0]
    vbuf[HALF_KEYS:HALF_KEYS + tq, :] = v_ref[0]
    vbuf[HALF_KEYS + tq:, :] = vn_ref[0]
    lane_h = _lane_head(Q_SUB)
    col = lax.broadcasted_iota(jnp.int32, (1, K_WIN), 1)
    bias = bias_ref[...]
    for sb in range(tq // Q_SUB):
        r0 = sb * Q_SUB
        qb = q_ref[0, r0:r0 + Q_SUB, :]
        qs = jnp.concatenate(
            [jnp.where(lane_h == h, qb, jnp.zeros_like(qb)) for h in range(HEADS_PER_GROUP)], axis=0)
        s = lax.dot_general(qs, kbuf[r0:r0 + K_WIN, :], (((1,), (1,)), ((), ())),
                            preferred_element_type=F32)
        kpos = i * tq + (r0 - HALF_KEYS) + col
        s = s + bias + jnp.where((kpos >= 0) & (kpos < seq_m), 0.0, NEG_INF)
        m = jnp.max(s, axis=-1, keepdims=True)
        p = jnp.exp(s - m)
        l = jnp.sum(p, axis=-1, keepdims=True)
        pv = _dot(p.astype(BF16), vbuf[r0:r0 + K_WIN, :])
        on = pv / l
        ls = m + jnp.log(l)
        o = jnp.zeros((Q_SUB, BRANCH_W), F32)
        lo = jnp.zeros((Q_SUB, BRANCH_W), F32)
        for h in range(HEADS_PER_GROUP):
            sel = lane_h == h
            o = jnp.where(sel, on[h * Q_SUB:(h + 1) * Q_SUB, :], o)
            lo = jnp.where(sel, ls[h * Q_SUB:(h + 1) * Q_SUB, :], lo)
        o_ref[0, r0:r0 + Q_SUB, :] = o
        lse_ref[0, r0:r0 + Q_SUB, :] = lo


def _attn_bias(group):
    dil = ATTN_DILATIONS[group]
    i = np.arange(Q_SUB)[:, None]
    j = np.arange(K_WIN)[None, :]
    off = np.abs(j - HALF_KEYS - i).astype(np.float32)
    slopes = 2.0 ** (-8.0 * np.arange(1, N_ATTN_HEADS + 1, dtype=np.float32) / N_ATTN_HEADS)
    blocks = []
    for h in range(HEADS_PER_GROUP):
        slope = slopes[group * HEADS_PER_GROUP + h]
        blocks.append(np.where(off <= HALF_KEYS, -slope * (off * dil), NEG_INF))
    return jnp.asarray(np.concatenate(blocks, axis=0), F32)


def _attn_group(qkv, group, batch, seq):
    dil = ATTN_DILATIONS[group]
    seq_m = seq // dil
    tq = min(ATTN_ROWS, seq_m)
    nq = seq_m // tq
    hb = tq // HALF_KEYS
    last_halo = seq_m // HALF_KEYS - 1
    per_pos = QKV_W // BRANCH_W
    view = qkv.reshape(batch, seq_m, dil * QKV_W)

    def col(which):
        return lambda b, r, i: r * per_pos + which * N_GROUPS + group

    def main(which):
        c = col(which)
        return pl.BlockSpec((1, tq, BRANCH_W), lambda b, r, i: (b, i, c(b, r, i)))

    def prev(which):
        c = col(which)
        return pl.BlockSpec((1, HALF_KEYS, BRANCH_W),
                            lambda b, r, i: (b, jnp.maximum(i * hb - 1, 0), c(b, r, i)))

    def nxt(which):
        c = col(which)
        return pl.BlockSpec((1, HALF_KEYS, BRANCH_W),
                            lambda b, r, i: (b, jnp.minimum((i + 1) * hb, last_halo), c(b, r, i)))

    out_spec = pl.BlockSpec((1, tq, BRANCH_W), lambda b, r, i: (b, i, r))
    out_shape = jax.ShapeDtypeStruct((batch, seq_m, dil * BRANCH_W), F32)
    o, lse = pl.pallas_call(
        functools.partial(_attn_kernel, tq=tq, seq_m=seq_m),
        grid=(batch, dil, nq),
        in_specs=[main(0), prev(1), main(1), nxt(1), prev(2), main(2), nxt(2),
                  pl.BlockSpec((HEADS_PER_GROUP * Q_SUB, K_WIN), lambda b, r, i: (0, 0))],
        out_specs=[out_spec, out_spec],
        out_shape=[out_shape, out_shape],
        scratch_shapes=[pltpu.VMEM((tq + 2 * HALF_KEYS, BRANCH_W), BF16),
                        pltpu.VMEM((tq + 2 * HALF_KEYS, BRANCH_W), BF16)],
        compiler_params=_params("parallel", "parallel", "parallel"),
        name=f"attn_g{group}",
    )(view, view, view, view, view, view, view, _attn_bias(group))
    return o.reshape(batch, seq, BRANCH_W), lse.reshape(batch, seq, BRANCH_W)


def _softplus(y):
    return jnp.maximum(y, 0.0) + jnp.log1p(jnp.exp(-jnp.abs(y)))


def _stack_heads(x, lane_h):
    xb = x.astype(BF16)
    return jnp.concatenate(
        [jnp.where(lane_h == h, xb, jnp.zeros_like(xb)) for h in range(RWKV_HEADS)], axis=0)


def _rwkv_kernel(rkv_ref, rkvh_ref, lat_ref, lath_ref, mur_ref, mul_ref, w0_ref, wup_ref, a0_ref, aup_ref,
                 kk_ref, ka_ref, rk_ref, lng_ref, lnb_ref, bdm_ref, bd1_ref, out_ref,
                 st_ref, r_s, k_s, v_s, p_s, b_s, lw_s, o_s, *, backward, tc):
    i = pl.program_id(1)

    @pl.when(i == 0)
    def _():
        st_ref[...] = jnp.zeros_like(st_ref)

    def shifted(x, halo):
        row = lax.broadcasted_iota(jnp.int32, x.shape, 0)
        if backward:
            edge = jnp.where(i == 0, 0.0, halo[0:1, :])
            return jnp.where(row == tc - 1, edge, pltpu.roll(x, tc - 1, 0))
        edge = jnp.where(i == 0, 0.0, halo[7:8, :])
        return jnp.where(row == 0, edge, pltpu.roll(x, 1, 0))

    x = rkv_ref[0]
    xr = x + (shifted(x, rkvh_ref[0]) - x) * mur_ref[0]
    lt = lat_ref[0]
    xl = lt + (shifted(lt, lath_ref[0]) - lt) * mul_ref[0]
    r = xr[:, 0:BRANCH_W]
    k = xr[:, BRANCH_W:2 * BRANCH_W]
    v = xr[:, 2 * BRANCH_W:3 * BRANCH_W]
    w_log = -_softplus(-(w0_ref[0] + _dot_split_both(jnp.tanh(xl), wup_ref[0]))) - 0.5
    a = jax.nn.sigmoid(a0_ref[0] + _dot_split_both(xl, aup_ref[0]))
    bd1 = bd1_ref[...]
    kk = k * kk_ref[0]
    kk = kk * lax.rsqrt(_dot_split_lhs(kk * kk, bd1) + 1e-12)
    k2 = k * (1.0 + (a - 1.0) * ka_ref[0])
    bonus = _dot_split_lhs(r * k2 * rk_ref[0], bd1) * v
    r_s[...] = r
    k_s[...] = k2
    v_s[...] = v
    p_s[...] = -kk
    b_s[...] = kk * a
    lw_s[...] = -jnp.exp(w_log)

    c = RWKV_CHUNK
    n_rows = RWKV_HEADS * c
    lane_h = _lane_head(c)
    t_i = lax.broadcasted_iota(jnp.int32, (c, c), 0)
    s_i = lax.broadcasted_iota(jnp.int32, (c, c), 1)
    tri = jnp.where((s_i >= t_i) if backward else (s_i <= t_i), 1.0, 0.0).astype(BF16)
    row = lax.broadcasted_iota(jnp.int32, (n_rows, n_rows), 0)
    colm = lax.broadcasted_iota(jnp.int32, (n_rows, n_rows), 1)
    same = (row // c) == (colm // c)
    tau = row % c
    sig = colm % c
    if backward:
        strict = same & (sig > tau)
        incl = same & (sig >= tau)
    else:
        strict = same & (sig < tau)
        incl = same & (sig <= tau)
    eye = row == colm
    last = 0 if backward else c - 1

    def nt(a_, b_):
        return lax.dot_general(a_, b_, (((1,), (1,)), ((), ())), preferred_element_type=F32)

    def tn(a_, b_):
        return lax.dot_general(a_, b_, (((0,), (0,)), ((), ())), preferred_element_type=F32)

    def chunk(j, carry):
        cj = (tc // c - 1 - j) if backward else j
        rows = pl.ds(pl.multiple_of(cj * c, c), c)
        lw = lw_s[rows, :]
        lw_hi = lw.astype(BF16)
        lw_r = lw - lw_hi.astype(F32)
        lw_mid = lw_r.astype(BF16)
        lw_lo = (lw_r - lw_mid.astype(F32)).astype(BF16)
        cumi = _dot(tri, lw_hi) + _dot(tri, lw_mid) + _dot(tri, lw_lo)
        cumx = cumi - lw
        tot = cumi[last:last + 1, :]
        e_m = jnp.exp(-cumi)
        e_t = jnp.exp(tot - cumi)
        b = b_s[rows, :]
        k2c = k_s[rows, :]
        pm = _stack_heads(p_s[rows, :] * jnp.exp(cumx), lane_h)
        rm = _stack_heads(r_s[rows, :] * jnp.exp(cumi), lane_h)
        bm = _stack_heads(b * e_m, lane_h)
        km = _stack_heads(k2c * e_m, lane_h)
        btm = _stack_heads(b * e_t, lane_h)
        ktm = _stack_heads(k2c * e_t, lane_h)
        vm = _stack_heads(v_s[rows, :], lane_h)

        a_pb = jnp.where(strict, nt(pm, bm), 0.0)
        a_pk = jnp.where(strict, nt(pm, km), 0.0).astype(BF16)
        a_rb = jnp.where(incl, nt(rm, bm), 0.0).astype(BF16)
        a_rk = jnp.where(incl, nt(rm, km), 0.0).astype(BF16)

        tinv = jnp.where(eye, 1.0, a_pb)
        apow = a_pb.astype(BF16)
        for _ in range(NEUMANN_DOUBLINGS):
            a2 = _dot(apow, apow)
            apow = a2.astype(BF16)
            tinv = tinv + _dot(tinv.astype(BF16), apow)
        tb = tinv.astype(BF16)

        ah = _dot(tb, pm).astype(BF16)
        uh = _dot(tb, _dot(a_pk, vm).astype(BF16)).astype(BF16)
        rh = (rm.astype(F32) + _dot(a_rb, ah)).astype(BF16)
        oh = _dot(a_rk, vm) + _dot(a_rb, uh)
        mt = jnp.where(eye, jnp.exp(tot), 0.0) + tn(btm, ah)
        gt = tn(btm, uh) + tn(ktm, vm)

        st = st_ref[...]
        stb = st.astype(BF16)
        om = _dot(rh, stb) + oh
        st_ref[...] = _dot(mt.astype(BF16), stb) + gt
        o = om[0:c, :]
        for h in range(1, RWKV_HEADS):
            o = o + om[h * c:(h + 1) * c, :]
        o_s[rows, :] = o
        return carry

    lax.fori_loop(0, tc // c, chunk, 0)

    o = o_s[...]
    bdm = bdm_ref[...]
    mu = _dot_split_lhs(o, bdm)
    xc = o - mu
    var = _dot_split_lhs(xc * xc, bdm)
    out_ref[0] = xc * lax.rsqrt(var + GN_EPS) * lng_ref[...] + lnb_ref[...] + bonus


def _rwkv_dir(rkv, lat, lp, direction, batch, seq):
    backward = direction == 1
    tc = min(RWKV_ROWS, seq)
    nt = seq // tc
    hb = tc // 8
    last_halo = seq // 8 - 1
    tile = (lambda i: nt - 1 - i) if backward else (lambda i: i)
    if backward:
        halo = lambda i: jnp.minimum((tile(i) + 1) * hb, last_halo)
    else:
        halo = lambda i: jnp.maximum(tile(i) * hb - 1, 0)
    d = direction
    vec = lambda w: pl.BlockSpec((1, 1, w), lambda b, i: (d, 0, 0))
    const2 = lambda shape: pl.BlockSpec(shape, lambda b, i: (0, 0))
    scr = pltpu.VMEM((tc, BRANCH_W), F32)
    return pl.pallas_call(
        functools.partial(_rwkv_kernel, backward=backward, tc=tc),
        grid=(batch, nt),
        in_specs=[
            pl.BlockSpec((1, tc, RKV_W), lambda b, i: (b, tile(i), 0)),
            pl.BlockSpec((1, 8, RKV_W), lambda b, i: (b, halo(i), 0)),
            pl.BlockSpec((1, tc, LAT_W), lambda b, i: (b, tile(i), d)),
            pl.BlockSpec((1, 8, LAT_W), lambda b, i: (b, halo(i), d)),
            vec(RKV_W), vec(LAT_W), vec(BRANCH_W),
            pl.BlockSpec((1, LAT_W, BRANCH_W), lambda b, i: (d, 0, 0)),
            vec(BRANCH_W),
            pl.BlockSpec((1, LAT_W, BRANCH_W), lambda b, i: (d, 0, 0)),
            vec(BRANCH_W), vec(BRANCH_W), vec(BRANCH_W),
            const2((1, BRANCH_W)), const2((1, BRANCH_W)),
            const2((BRANCH_W, BRANCH_W)), const2((BRANCH_W, BRANCH_W)),
        ],
        out_specs=pl.BlockSpec((1, tc, BRANCH_W), lambda b, i: (b, tile(i), 0)),
        out_shape=jax.ShapeDtypeStruct((batch, seq, BRANCH_W), F32),
        scratch_shapes=[pltpu.VMEM((RWKV_HEADS * RWKV_CHUNK, BRANCH_W), F32)] + [scr] * 7,
        compiler_params=_params("parallel", "arbitrary"),
        name=f"rwkv_d{direction}",
    )(rkv, rkv, lat, lat, lp["mu_rkv"], lp["mu_lat"], lp["w0"], lp["w_up_pad"], lp["a0"], lp["a_up_pad"],
      lp["k_k"], lp["k_a"], lp["r_k"], lp["ln_g"], lp["ln_b"], lp["bd_mean"], lp["bd_ones"])


def _silu(z):
    return z * jax.nn.sigmoid(z)


def _merge_kernel(x_ref, g_ref, o0_ref, l0_ref, o1_ref, l1_ref, o2_ref, l2_ref, z_ref,
                  ubp_ref, bc_ref, ubn_ref, yd0_ref, yd1_ref,
                  poolw_ref, pools_ref, sgg_ref, sgw_ref, sgb_ref, wg_ref, wb_ref, wo_ref,
                  out_ref, *, tm, seq):
    i = pl.program_id(1)
    n_tiles = seq // tm
    x = x_ref[0]
    ms = jnp.mean(x * x, axis=-1, keepdims=True)
    h = (x * lax.rsqrt(ms + EPS) * g_ref[...]).astype(BF16)
    z = z_ref[0]
    lane_g = _lane_head(tm)

    l0, l1, l2 = l0_ref[0], l1_ref[0], l2_ref[0]
    mx = jnp.maximum(jnp.maximum(l0, l1), l2)
    e0, e1, e2 = jnp.exp(l0 - mx), jnp.exp(l1 - mx), jnp.exp(l2 - mx)
    y_a = (e0 * o0_ref[0] + e1 * o1_ref[0] + e2 * o2_ref[0]) / (e0 + e1 + e2)

    bc = bc_ref[0]
    u = bc[:, 0:BRANCH_W]
    prev = jnp.where(i == 0, 0.0, ubp_ref[0][:, 0:BRANCH_W])
    nxt = jnp.where(i == n_tiles - 1, 0.0, ubn_ref[0][:, 0:BRANCH_W])
    ext = jnp.concatenate([prev, u, nxt], axis=0)
    n_ext = tm + 2 * POOL_HALO
    f1 = ext + pltpu.roll(ext, 1, 0)
    f2 = pltpu.roll(f1, 1, 0) + pltpu.roll(f1, n_ext - 1, 0)
    f4 = pltpu.roll(f2, 2, 0) + pltpu.roll(f2, n_ext - 2, 0)
    f8 = pltpu.roll(f4, 4, 0) + pltpu.roll(f4, n_ext - 4, 0)
    lane_e = _lane_head(n_ext)
    win = jnp.where(lane_e == 0, f1, jnp.where(lane_e == 1, f2, jnp.where(lane_e == 2, f4, f8)))
    win = win[POOL_HALO:POOL_HALO + tm, :]
    pos = i * tm + lax.broadcasted_iota(jnp.int32, (tm, BRANCH_W), 0)
    half = jnp.where(lane_g == 0, POOL_HALF[0],
                     jnp.where(lane_g == 1, POOL_HALF[1], jnp.where(lane_g == 2, POOL_HALF[2], POOL_HALF[3])))
    cnt = (jnp.minimum(pos + half, seq) - jnp.maximum(pos - half, 0)).astype(F32)
    y_b = _dot((win / cnt - u).astype(BF16), poolw_ref[...]) * pools_ref[...]

    u_c = bc[:, BRANCH_W:2 * BRANCH_W]
    v_c = bc[:, 2 * BRANCH_W:3 * BRANCH_W]
    msv = jnp.mean(v_c * v_c, axis=-1, keepdims=True)
    v_n = (v_c * lax.rsqrt(msv + EPS) * sgg_ref[...]).astype(BF16)
    lane_c = _lane_head(CHUNK)
    sv_chunks = []
    for ch in range(tm // CHUNK):
        vch = v_n[ch * CHUNK:(ch + 1) * CHUNK, :]
        sv = jnp.zeros((CHUNK, BRANCH_W), F32)
        for g in range(SG_GROUPS):
            sv = jnp.where(lane_c == g, _dot(sgw_ref[g], vch), sv)
        sv_chunks.append(sv + sgb_ref[...])
    y_c = u_c * jnp.concatenate(sv_chunks, axis=0)

    y_d = yd0_ref[0] + yd1_ref[0]

    merged = jnp.zeros((tm, D_MODEL), F32)
    for n, y in enumerate((y_a, y_b, y_c, y_d)):
        ys = (y * _silu(z[:, n * BRANCH_W:(n + 1) * BRANCH_W])).astype(BF16)
        gate = jax.nn.sigmoid(_dot(h, wg_ref[:, n * D_MODEL:(n + 1) * D_MODEL]))
        merged = merged + gate * _dot(ys, wb_ref[n])
    out_ref[0] = x + _dot(merged.astype(BF16), wo_ref[...])


def _merge(x, attn, z, bc, yd0, yd1, lp, batch, seq):
    tm = MERGE_ROWS
    hb = tm // POOL_HALO
    last_halo = seq // POOL_HALO - 1
    row = lambda w: pl.BlockSpec((1, tm, w), lambda b, i: (b, i, 0))
    const2 = lambda shape: pl.BlockSpec(shape, lambda b, i: (0, 0))
    const3 = lambda shape: pl.BlockSpec(shape, lambda b, i: (0, 0, 0))
    in_specs = [row(D_MODEL), const2((1, D_MODEL))]
    in_specs += [row(BRANCH_W)] * 6
    in_specs += [
        row(Z_W),
        pl.BlockSpec((1, POOL_HALO, BRANCH_W), lambda b, i: (b, jnp.maximum(i * hb - 1, 0), 0)),
        row(BC_W),
        pl.BlockSpec((1, POOL_HALO, BRANCH_W), lambda b, i: (b, jnp.minimum((i + 1) * hb, last_halo), 0)),
        row(BRANCH_W), row(BRANCH_W),
        const2((BRANCH_W, BRANCH_W)), const2((1, BRANCH_W)), const2((1, BRANCH_W)),
        const3((SG_GROUPS, CHUNK, CHUNK)), const2((CHUNK, BRANCH_W)),
        const2((D_MODEL, N_BRANCH * D_MODEL)), const3((N_BRANCH, BRANCH_W, D_MODEL)),
        const2((D_MODEL, D_MODEL)),
    ]
    flat_attn = [a for pair in attn for a in pair]
    return pl.pallas_call(
        functools.partial(_merge_kernel, tm=tm, seq=seq),
        grid=(batch, seq // tm),
        in_specs=in_specs,
        out_specs=row(D_MODEL),
        out_shape=jax.ShapeDtypeStruct((batch, seq, D_MODEL), F32),
        compiler_params=_params("parallel", "parallel"),
        name="merge",
    )(x, lp["norm_g"], *flat_attn, z, bc, bc, bc, yd0, yd1,
      lp["pool_w_bd"], lp["pool_scale"], lp["sg_norm_g"], lp["sg_w"], lp["sg_bias"],
      lp["w_gate"], lp["w_branch"], lp["w_out"])


def _block_diag(blocks):
    n = blocks.shape[0]
    w = blocks.shape[1]
    out = jnp.zeros((n * w, n * w), blocks.dtype)
    for g in range(n):
        out = out.at[g * w:(g + 1) * w, g * w:(g + 1) * w].set(blocks[g])
    return out


def _layer_params(l, norm_g, w_in, q_norm_g, k_norm_g, pool_w, pool_scale, sg_norm_g, sg_w, sg_b,
                  mu_rkv, mu_lat, w0, w_up, a0, a_up, k_k, k_a, r_k, ln_g, ln_b, w_branch, w_out):
    w = w_in[l]
    seg = lambda lo, width: w[:, lo:lo + width]
    w_small = jnp.concatenate([
        seg(0, QKV_W), seg(3840, RKV_W), seg(4608, 2 * LAT_W), seg(2560, BRANCH_W), seg(3072, 2 * BRANCH_W),
        seg(2304, BRANCH_W), seg(2816, BRANCH_W), seg(3584, BRANCH_W), seg(4864, BRANCH_W)], axis=1)
    ones_blocks = jnp.ones((RWKV_HEADS, HEAD_DIM, HEAD_DIM), F32)
    zeros_lat = jnp.zeros((2, LAT_W // 2, BRANCH_W), F32)
    vec3 = lambda p: p[l][:, None, :]
    return {
        "norm_g": norm_g[l][None, :],
        "w_small": w_small.astype(BF16),
        "w_gate": w[:, PROJ_SMALL_W:].astype(BF16),
        "q_gain": jnp.tile(q_norm_g[l], HEADS_PER_GROUP)[None, :],
        "k_gain": jnp.tile(k_norm_g[l], HEADS_PER_GROUP)[None, :],
        "bd_mean": (_block_diag(ones_blocks) / HEAD_DIM).astype(BF16),
        "bd_ones": _block_diag(ones_blocks).astype(BF16),
        "pool_w_bd": _block_diag(pool_w[l]).astype(BF16),
        "pool_scale": pool_scale[l][None, :],
        "sg_norm_g": sg_norm_g[l][None, :],
        "sg_w": sg_w[l].astype(BF16),
        "sg_bias": jnp.repeat(sg_b[l].T, HEAD_DIM, axis=1),
        "mu_rkv": vec3(mu_rkv), "mu_lat": vec3(mu_lat), "w0": vec3(w0), "a0": vec3(a0),
        "k_k": vec3(k_k), "k_a": vec3(k_a), "r_k": vec3(r_k),
        "w_up_pad": jnp.concatenate([w_up[l], zeros_lat], axis=1),
        "a_up_pad": jnp.concatenate([zeros_lat, a_up[l]], axis=1),
        "ln_g": ln_g[l][None, :], "ln_b": ln_b[l][None, :],
        "w_branch": w_branch[l].astype(BF16),
        "w_out": w_out[l].astype(BF16),
    }


def _layer(x, lp):
    batch, seq, _ = x.shape
    qkv, rkv, lat, bc, z = _proj(x.reshape(batch * seq, D_MODEL), lp)
    attn = [_attn_group(qkv, g, batch, seq) for g in range(N_GROUPS)]
    rkv = rkv.reshape(batch, seq, RKV_W)
    lat = lat.reshape(batch, seq, 2 * LAT_W)
    yd0 = _rwkv_dir(rkv, lat, lp, 0, batch, seq)
    yd1 = _rwkv_dir(rkv, lat, lp, 1, batch, seq)
    return _merge(x, attn, z.reshape(batch, seq, Z_W), bc.reshape(batch, seq, BC_W), yd0, yd1, lp, batch, seq)


def kernel(x_prompt, x_sample, norm_g, w_in, q_norm_g, k_norm_g, pool_w, pool_scale, sg_norm_g, sg_w, sg_b,
           mu_rkv, mu_lat, w0, w_up, a0, a_up, k_k, k_a, r_k, ln_g, ln_b, w_branch, w_out):
    weights = (norm_g, w_in, q_norm_g, k_norm_g, pool_w, pool_scale, sg_norm_g, sg_w, sg_b,
               mu_rkv, mu_lat, w0, w_up, a0, a_up, k_k, k_a, r_k, ln_g, ln_b, w_branch, w_out)
    layers = [_layer_params(l, *weights) for l in range(norm_g.shape[0])]
    outs = []
    for x in (x_prompt, x_sample):
        for lp in layers:
            x = _layer(x, lp)
        outs.append(x)
    return tuple(outs)
```

```python
import functools

import jax
import jax.numpy as jnp
import numpy as np
from jax import lax
from jax.experimental import pallas as pl
from jax.experimental.pallas import tpu as pltpu

F32 = jnp.float32
BF16 = jnp.bfloat16

D_MODEL = 1024
BRANCH_W = 256
N_BRANCH = 4
EPS = 1e-6
NEG_INF = -1e30

HEAD_DIM = 64
ATTN_DILATIONS = (1, 4, 16)
N_GROUPS = 3
HEADS_PER_GROUP = 4
N_ATTN_HEADS = N_GROUPS * HEADS_PER_GROUP
ATTN_QKV = N_ATTN_HEADS * HEAD_DIM
HALF_KEYS = 64
Q_SUB = 128
K_WIN = Q_SUB + 2 * HALF_KEYS

POOL_HALF = (1, 2, 4, 8)
POOL_HALO = 8
CHUNK = 128
SG_GROUPS = 4

RWKV_HEADS = 4
RWKV_N = 64
RWKV_CHUNK = 64
LAT_W = 128
GN_EPS = 64e-5
NEUMANN_DOUBLINGS = 5

QKV_W = 3 * ATTN_QKV
RKV_W = 3 * BRANCH_W
BC_W = 3 * BRANCH_W
Z_W = 4 * BRANCH_W
PROJ_SMALL_W = QKV_W + RKV_W + 2 * LAT_W + BC_W + Z_W

PROJ_ROWS = 256
MERGE_ROWS = 256
ATTN_ROWS = 512
RWKV_ROWS = 512
VMEM_LIMIT = 56 * 1024 * 1024


def _params(*sem):
    return pltpu.CompilerParams(dimension_semantics=sem, vmem_limit_bytes=VMEM_LIMIT)


def _dot(a, b):
    return jnp.dot(a, b, preferred_element_type=F32)


def _split_bf16(x):
    hi = x.astype(BF16)
    lo = (x - hi.astype(F32)).astype(BF16)
    return hi, lo


def _dot_split_lhs(x, m_bf16):
    hi, lo = _split_bf16(x)
    return _dot(hi, m_bf16) + _dot(lo, m_bf16)


def _dot_split_both(a, b):
    a_hi, a_lo = _split_bf16(a)
    b_hi, b_lo = _split_bf16(b)
    return _dot(a_hi, b_hi) + _dot(a_lo, b_hi) + _dot(a_hi, b_lo)


def _lane_head(rows):
    return lax.broadcasted_iota(jnp.int32, (rows, BRANCH_W), 1) // HEAD_DIM


def _proj_kernel(x_ref, g_ref, w_ref, qg_ref, kg_ref, bd_ref, qkv_ref, rkv_ref, lat_ref, bc_ref, z_ref):
    x = x_ref[...]
    ms = jnp.mean(x * x, axis=-1, keepdims=True)
    h = (x * lax.rsqrt(ms + EPS) * g_ref[...]).astype(BF16)
    bd = bd_ref[...]
    for c in range(2 * N_GROUPS):
        lo = c * BRANCH_W
        y = _dot(h, w_ref[:, lo:lo + BRANCH_W])
        msq = _dot_split_lhs(y * y, bd)
        if c < N_GROUPS:
            yn = y * lax.rsqrt(msq + EPS) * qg_ref[...] * (HEAD_DIM ** -0.5)
        else:
            yn = y * lax.rsqrt(msq + EPS) * kg_ref[...]
        qkv_ref[:, lo:lo + BRANCH_W] = yn.astype(BF16)
    qkv_ref[:, 2 * ATTN_QKV:QKV_W] = _dot(h, w_ref[:, 2 * ATTN_QKV:QKV_W]).astype(BF16)
    o = QKV_W
    rkv_ref[...] = _dot(h, w_ref[:, o:o + RKV_W])
    o += RKV_W
    lat_ref[...] = _dot(h, w_ref[:, o:o + 2 * LAT_W])
    o += 2 * LAT_W
    bc_ref[...] = _dot(h, w_ref[:, o:o + BC_W])
    o += BC_W
    z_ref[...] = _dot(h, w_ref[:, o:o + Z_W])


def _proj(x2d, lp):
    rows = x2d.shape[0]
    tm = PROJ_ROWS
    const = lambda i: (0, 0)
    row = lambda i: (i, 0)
    widths = (QKV_W, RKV_W, 2 * LAT_W, BC_W, Z_W)
    dtypes = (BF16, F32, F32, F32, F32)
    return pl.pallas_call(
        _proj_kernel,
        grid=(rows // tm,),
        in_specs=[
            pl.BlockSpec((tm, D_MODEL), row),
            pl.BlockSpec((1, D_MODEL), const),
            pl.BlockSpec((D_MODEL, PROJ_SMALL_W), const),
            pl.BlockSpec((1, BRANCH_W), const),
            pl.BlockSpec((1, BRANCH_W), const),
            pl.BlockSpec((BRANCH_W, BRANCH_W), const),
        ],
        out_specs=[pl.BlockSpec((tm, w), row) for w in widths],
        out_shape=[jax.ShapeDtypeStruct((rows, w), dt) for w, dt in zip(widths, dtypes)],
        compiler_params=_params("parallel"),
        name="proj",
    )(x2d, lp["norm_g"], lp["w_small"], lp["q_gain"], lp["k_gain"], lp["bd_mean"])


def _attn_kernel(q_ref, kp_ref, k_ref, kn_ref, vp_ref, v_ref, vn_ref, bias_ref, o_ref, lse_ref,
                 kbuf, vbuf, *, tq, seq_m):
    i = pl.program_id(2)
    kbuf[0:HALF_KEYS, :] = kp_ref[0]
    kbuf[HALF_KEYS:HALF_KEYS + tq, :] = k_ref[0]
    kbuf[HALF_KEYS + tq:, :] = kn_ref[0]
    vbuf[0:HALF_KEYS, :] = vp_ref[0]
    vbuf[HALF_KEYS:HALF_KEYS + tq, :] = v_ref[0]
    vbuf[HALF_KEYS + tq:, :] = vn_ref[0]
    lane_h = _lane_head(Q_SUB)
    col = lax.broadcasted_iota(jnp.int32, (1, K_WIN), 1)
    bias = bias_ref[...]
    for sb in range(tq // Q_SUB):
        r0 = sb * Q_SUB
        qb = q_ref[0, r0:r0 + Q_SUB, :]
        qs = jnp.concatenate(
            [jnp.where(lane_h == h, qb, jnp.zeros_like(qb)) for h in range(HEADS_PER_GROUP)], axis=0)
        s = lax.dot_general(qs, kbuf[r0:r0 + K_WIN, :], (((1,), (1,)), ((), ())),
                            preferred_element_type=F32)
        kpos = i * tq + (r0 - HALF_KEYS) + col
        s = s + bias + jnp.where((kpos >= 0) & (kpos < seq_m), 0.0, NEG_INF)
        m = jnp.max(s, axis=-1, keepdims=True)
        p = jnp.exp(s - m)
        l = jnp.sum(p, axis=-1, keepdims=True)
        pv = _dot(p.astype(BF16), vbuf[r0:r0 + K_WIN, :])
        on = pv / l
        ls = m + jnp.log(l)
        o = jnp.zeros((Q_SUB, BRANCH_W), F32)
        lo = jnp.zeros((Q_SUB, BRANCH_W), F32)
        for h in range(HEADS_PER_GROUP):
            sel = lane_h == h
            o = jnp.where(sel, on[h * Q_SUB:(h + 1) * Q_SUB, :], o)
            lo = jnp.where(sel, ls[h * Q_SUB:(h + 1) * Q_SUB, :], lo)
        o_ref[0, r0:r0 + Q_SUB, :] = o
        lse_ref[0, r0:r0 + Q_SUB, :] = lo


def _attn_bias(group):
    dil = ATTN_DILATIONS[group]
    i = np.arange(Q_SUB)[:, None]
    j = np.arange(K_WIN)[None, :]
    off = np.abs(j - HALF_KEYS - i).astype(np.float32)
    slopes = 2.0 ** (-8.0 * np.arange(1, N_ATTN_HEADS + 1, dtype=np.float32) / N_ATTN_HEADS)
    blocks = []
    for h in range(HEADS_PER_GROUP):
        slope = slopes[group * HEADS_PER_GROUP + h]
        blocks.append(np.where(off <= HALF_KEYS, -slope * (off * dil), NEG_INF))
    return jnp.asarray(np.concatenate(blocks, axis=0), F32)


def _attn_group(qkv, group, batch, seq):
    dil = ATTN_DILATIONS[group]
    seq_m = seq // dil
    tq = min(ATTN_ROWS, seq_m)
    nq = seq_m // tq
    hb = tq // HALF_KEYS
    last_halo = seq_m // HALF_KEYS - 1
    per_pos = QKV_W // BRANCH_W
    view = qkv.reshape(batch, seq_m, dil * QKV_W)

    def col(which):
        return lambda b, r, i: r * per_pos + which * N_GROUPS + group

    def main(which):
        c = col(which)
        return pl.BlockSpec((1, tq, BRANCH_W), lambda b, r, i: (b, i, c(b, r, i)))

    def prev(which):
        c = col(which)
        return pl.BlockSpec((1, HALF_KEYS, BRANCH_W),
                            lambda b, r, i: (b, jnp.maximum(i * hb - 1, 0), c(b, r, i)))

    def nxt(which):
        c = col(which)
        return pl.BlockSpec((1, HALF_KEYS, BRANCH_W),
                            lambda b, r, i: (b, jnp.minimum((i + 1) * hb, last_halo), c(b, r, i)))

    out_spec = pl.BlockSpec((1, tq, BRANCH_W), lambda b, r, i: (b, i, r))
    out_shape = jax.ShapeDtypeStruct((batch, seq_m, dil * BRANCH_W), F32)
    o, lse = pl.pallas_call(
        functools.partial(_attn_kernel, tq=tq, seq_m=seq_m),
        grid=(batch, dil, nq),
        in_specs=[main(0), prev(1), main(1), nxt(1), prev(2), main(2), nxt(2),
                  pl.BlockSpec((HEADS_PER_GROUP * Q_SUB, K_WIN), lambda b, r, i: (0, 0))],
        out_specs=[out_spec, out_spec],
        out_shape=[out_shape, out_shape],
        scratch_shapes=[pltpu.VMEM((tq + 2 * HALF_KEYS, BRANCH_W), BF16),
                        pltpu.VMEM((tq + 2 * HALF_KEYS, BRANCH_W), BF16)],
        compiler_params=_params("parallel", "parallel", "parallel"),
        name=f"attn_g{group}",
    )(view, view, view, view, view, view, view, _attn_bias(group))
    return o.reshape(batch, seq, BRANCH_W), lse.reshape(batch, seq, BRANCH_W)


def _softplus(y):
    return jnp.maximum(y, 0.0) + jnp.log(1.0 + jnp.exp(-jnp.abs(y)))


def _stack_heads(xb, lane_h):
    return jnp.concatenate(
        [jnp.where(lane_h == h, xb, jnp.zeros_like(xb)) for h in range(RWKV_HEADS)], axis=0)


def _nt(a, b):
    return lax.dot_general(a, b, (((1,), (1,)), ((), ())), preferred_element_type=F32)


def _rwkv_kernel(rkvf_ref, rkvfh_ref, rkvb_ref, rkvbh_ref, latf_ref, latfh_ref, latb_ref, latbh_ref,
                 mur_ref, mul_ref, w0_ref, wup_ref, a0_ref, aup_ref, kk_ref, ka_ref, rk_ref,
                 lng_ref, lnb_ref, bdm_ref, bd1_ref, outf_ref, outb_ref,
                 st_ref, r_s, k_s, v_s, p_s, b_s, lw_s, o_s, *, batch, tc):
    i = pl.program_id(0)
    c = RWKV_CHUNK
    n_chunks = tc // c
    streams = [(d, b) for d in range(2) for b in range(batch)]

    @pl.when(i == 0)
    def _():
        st_ref[...] = jnp.zeros_like(st_ref)

    def shifted(x, halo, backward):
        row = lax.broadcasted_iota(jnp.int32, x.shape, 0)
        if backward:
            edge = jnp.where(i == 0, 0.0, halo[0:1, :])
            return jnp.where(row == tc - 1, edge, pltpu.roll(x, tc - 1, 0))
        edge = jnp.where(i == 0, 0.0, halo[7:8, :])
        return jnp.where(row == 0, edge, pltpu.roll(x, 1, 0))

    bd1 = bd1_ref[...]
    for s, (d, b) in enumerate(streams):
        backward = d == 1
        x_ref, xh_ref, l_ref, lh_ref = ((rkvb_ref, rkvbh_ref, latb_ref, latbh_ref) if backward
                                        else (rkvf_ref, rkvfh_ref, latf_ref, latfh_ref))
        x = x_ref[b]
        xr = x + (shifted(x, xh_ref[b], backward) - x) * mur_ref[d]
        lt = l_ref[b]
        xl = lt + (shifted(lt, lh_ref[b], backward) - lt) * mul_ref[d]
        r = xr[:, 0:BRANCH_W]
        k = xr[:, BRANCH_W:2 * BRANCH_W]
        w_log = -_softplus(-(w0_ref[d] + _dot_split_both(jnp.tanh(xl), wup_ref[d]))) - 0.5
        a = jax.nn.sigmoid(a0_ref[d] + _dot_split_both(xl, aup_ref[d]))
        kk = k * kk_ref[d]
        kk = kk * lax.rsqrt(_dot_split_lhs(kk * kk, bd1) + 1e-12)
        r_s[s] = r
        k_s[s] = k * (1.0 + (a - 1.0) * ka_ref[d])
        v_s[s] = xr[:, 2 * BRANCH_W:3 * BRANCH_W]
        p_s[s] = -kk
        b_s[s] = kk * a
        lw_s[s] = -jnp.exp(w_log)

    lane_h = _lane_head(c)
    t_i = lax.broadcasted_iota(jnp.int32, (c, c), 0)
    s_i = lax.broadcasted_iota(jnp.int32, (c, c), 1)
    tau = lax.broadcasted_iota(jnp.int32, (c, BRANCH_W), 0)
    sig = lax.broadcasted_iota(jnp.int32, (c, BRANCH_W), 1) % c
    eye = sig == tau
    eye_b = jnp.where(eye, 1.0, 0.0).astype(BF16)

    tri_f = jnp.where(s_i <= t_i, 1.0, 0.0).astype(BF16)
    tri_b = jnp.where(s_i >= t_i, 1.0, 0.0).astype(BF16)
    sm = lambda xb: _stack_heads(xb, lane_h)
    bf = lambda x: x.astype(BF16)
    top = lambda x: x[0:c]
    mid = lambda x: x[c:2 * c]

    def chunks(j, carry):
        ids = range(len(streams))
        back = [d == 1 for d, _ in streams]
        rows = [pl.ds(pl.multiple_of(((n_chunks - 1 - j) if bk else j) * c, c), c) for bk in back]
        strict = [(sig > tau) if bk else (sig < tau) for bk in back]
        incl = [(sig >= tau) if bk else (sig <= tau) for bk in back]

        def cumsum(s):
            lw = lw_s[s, rows[s], :]
            lw_hi = bf(lw)
            lw_r = lw - lw_hi.astype(F32)
            lw_mid = bf(lw_r)
            lw_lo = bf(lw_r - lw_mid.astype(F32))
            tri = tri_b if back[s] else tri_f
            return _dot(tri, lw_hi) + _dot(tri, lw_mid) + _dot(tri, lw_lo)

        cumi = [cumsum(s) for s in ids]
        tot = [cumi[s][0:1, :] if back[s] else cumi[s][c - 1:c, :] for s in ids]
        e_m = [jnp.exp(-cumi[s]) for s in ids]
        e_t = [jnp.exp(tot[s] - cumi[s]) for s in ids]
        bq = [b_s[s, rows[s], :] for s in ids]
        kq = [k_s[s, rows[s], :] for s in ids]
        pt = [bf(p_s[s, rows[s], :] * jnp.exp(cumi[s] - lw_s[s, rows[s], :])) for s in ids]
        rt = [r_s[s, rows[s], :] * jnp.exp(cumi[s]) for s in ids]
        pr = [jnp.concatenate([pt[s], bf(rt[s])], axis=0) for s in ids]
        ab = [_nt(pr[s], sm(bf(bq[s] * e_m[s]))) for s in ids]
        ak = [_nt(pr[s], sm(bf(kq[s] * e_m[s]))) for s in ids]
        a_pb = [jnp.where(strict[s], top(ab[s]), 0.0) for s in ids]
        a_rb = [bf(jnp.where(incl[s], mid(ab[s]), 0.0)) for s in ids]
        a_pk = [bf(jnp.where(strict[s], top(ak[s]), 0.0)) for s in ids]
        a_rk = [bf(jnp.where(incl[s], mid(ak[s]), 0.0)) for s in ids]
        l_b = [bf(_nt(eye_b, sm(bf(bq[s] * e_t[s])))) for s in ids]
        l_k = [bf(_nt(eye_b, sm(bf(kq[s] * e_t[s])))) for s in ids]

        tinv = [jnp.where(eye, 1.0, a_pb[s]) for s in ids]
        apow = [bf(a_pb[s]) for s in ids]
        a2 = [_dot(apow[s], sm(apow[s])) for s in ids]
        for _ in range(NEUMANN_DOUBLINGS - 1):
            apow = [bf(a2[s]) for s in ids]
            both = [_dot(jnp.concatenate([bf(tinv[s]), apow[s]], axis=0), sm(apow[s])) for s in ids]
            tinv = [tinv[s] + top(both[s]) for s in ids]
            a2 = [mid(both[s]) for s in ids]
        tinv = [tinv[s] + _dot(bf(tinv[s]), sm(bf(a2[s]))) for s in ids]
        tb = [bf(t) for t in tinv]

        av = [_dot(jnp.concatenate([a_pk[s], a_rk[s], l_k[s]], axis=0), sm(bf(v_s[s, rows[s], :])))
              for s in ids]
        ah = [_dot(tb[s], sm(pt[s])) for s in ids]
        uh = [_dot(tb[s], sm(bf(top(av[s])))) for s in ids]
        lhs = [jnp.concatenate([a_rb[s], l_b[s]], axis=0) for s in ids]
        o_a = [_dot(lhs[s], sm(bf(ah[s]))) for s in ids]
        o_u = [_dot(lhs[s], sm(bf(uh[s]))) for s in ids]
        rh = [rt[s] + top(o_a[s]) for s in ids]
        mt = [jnp.where(eye, jnp.exp(tot[s]), 0.0) + mid(o_a[s]) for s in ids]
        oh = [mid(av[s]) + top(o_u[s]) for s in ids]
        gt = [mid(o_u[s]) + av[s][2 * c:3 * c] for s in ids]

        out = [_dot(jnp.concatenate([bf(rh[s]), bf(mt[s])], axis=0), sm(bf(st_ref[s]))) for s in ids]
        for s in ids:
            o_s[s, rows[s], :] = top(out[s]) + oh[s]
            st_ref[s] = mid(out[s]) + gt[s]
        return carry

    lax.fori_loop(0, n_chunks, chunks, 0)

    bdm = bdm_ref[...]
    for s, (d, b) in enumerate(streams):
        o = o_s[s]
        xc = o - _dot_split_lhs(o, bdm)
        var = _dot((xc * xc).astype(BF16), bdm)
        bonus = _dot((r_s[s] * k_s[s] * rk_ref[d]).astype(BF16), bd1) * v_s[s]
        y = xc * lax.rsqrt(var + GN_EPS) * lng_ref[...] + lnb_ref[...] + bonus
        if d == 1:
            outb_ref[b] = y
        else:
            outf_ref[b] = y


def _rwkv(rkv, lat, lp, batch, seq):
    tc = min(RWKV_ROWS, seq)
    nt = seq // tc
    hb = tc // 8
    last_halo = seq // 8 - 1
    fwd = lambda i: i
    bwd = lambda i: nt - 1 - i
    fwd_halo = lambda i: jnp.maximum(i * hb - 1, 0)
    bwd_halo = lambda i: jnp.minimum((bwd(i) + 1) * hb, last_halo)
    full3 = lambda shape: pl.BlockSpec(shape, lambda i: (0, 0, 0))
    full2 = lambda shape: pl.BlockSpec(shape, lambda i: (0, 0))
    n_streams = 2 * batch
    scr = pltpu.VMEM((n_streams, tc, BRANCH_W), F32)
    out_shape = jax.ShapeDtypeStruct((batch, seq, BRANCH_W), F32)
    return pl.pallas_call(
        functools.partial(_rwkv_kernel, batch=batch, tc=tc),
        grid=(nt,),
        in_specs=[
            pl.BlockSpec((batch, tc, RKV_W), lambda i: (0, fwd(i), 0)),
            pl.BlockSpec((batch, 8, RKV_W), lambda i: (0, fwd_halo(i), 0)),
            pl.BlockSpec((batch, tc, RKV_W), lambda i: (0, bwd(i), 0)),
            pl.BlockSpec((batch, 8, RKV_W), lambda i: (0, bwd_halo(i), 0)),
            pl.BlockSpec((batch, tc, LAT_W), lambda i: (0, fwd(i), 0)),
            pl.BlockSpec((batch, 8, LAT_W), lambda i: (0, fwd_halo(i), 0)),
            pl.BlockSpec((batch, tc, LAT_W), lambda i: (0, bwd(i), 1)),
            pl.BlockSpec((batch, 8, LAT_W), lambda i: (0, bwd_halo(i), 1)),
            full3((2, 1, RKV_W)), full3((2, 1, LAT_W)), full3((2, 1, BRANCH_W)),
            full3((2, LAT_W, BRANCH_W)), full3((2, 1, BRANCH_W)), full3((2, LAT_W, BRANCH_W)),
            full3((2, 1, BRANCH_W)), full3((2, 1, BRANCH_W)), full3((2, 1, BRANCH_W)),
            full2((1, BRANCH_W)), full2((1, BRANCH_W)),
            full2((BRANCH_W, BRANCH_W)), full2((BRANCH_W, BRANCH_W)),
        ],
        out_specs=[pl.BlockSpec((batch, tc, BRANCH_W), lambda i: (0, fwd(i), 0)),
                   pl.BlockSpec((batch, tc, BRANCH_W), lambda i: (0, bwd(i), 0))],
        out_shape=[out_shape, out_shape],
        scratch_shapes=[pltpu.VMEM((n_streams, RWKV_CHUNK, BRANCH_W), F32)] + [scr] * 7,
        compiler_params=_params("arbitrary"),
        name="rwkv",
    )(rkv, rkv, rkv, rkv, lat, lat, lat, lat, lp["mu_rkv"], lp["mu_lat"], lp["w0"], lp["w_up_pad"],
      lp["a0"], lp["a_up_pad"], lp["k_k"], lp["k_a"], lp["r_k"], lp["ln_g"], lp["ln_b"],
      lp["bd_mean"], lp["bd_ones"])


def _silu(z):
    return z * jax.nn.sigmoid(z)


def _merge_kernel(x_ref, g_ref, o0_ref, l0_ref, o1_ref, l1_ref, o2_ref, l2_ref, z_ref,
                  ubp_ref, bc_ref, ubn_ref, yd0_ref, yd1_ref,
                  poolw_ref, pools_ref, sgg_ref, sgw_ref, sgb_ref, wg_ref, wb_ref, wo_ref,
                  out_ref, *, tm, seq):
    i = pl.program_id(1)
    n_tiles = seq // tm
    x = x_ref[0]
    ms = jnp.mean(x * x, axis=-1, keepdims=True)
    h = (x * lax.rsqrt(ms + EPS) * g_ref[...]).astype(BF16)
    z = z_ref[0]
    lane_g = _lane_head(tm)

    l0, l1, l2 = l0_ref[0], l1_ref[0], l2_ref[0]
    mx = jnp.maximum(jnp.maximum(l0, l1), l2)
    e0, e1, e2 = jnp.exp(l0 - mx), jnp.exp(l1 - mx), jnp.exp(l2 - mx)
    y_a = (e0 * o0_ref[0] + e1 * o1_ref[0] + e2 * o2_ref[0]) / (e0 + e1 + e2)

    bc = bc_ref[0]
    u = bc[:, 0:BRANCH_W]
    prev = jnp.where(i == 0, 0.0, ubp_ref[0][:, 0:BRANCH_W])
    nxt = jnp.where(i == n_tiles - 1, 0.0, ubn_ref[0][:, 0:BRANCH_W])
    ext = jnp.concatenate([prev, u, nxt], axis=0)
    n_ext = tm + 2 * POOL_HALO
    f1 = ext + pltpu.roll(ext, 1, 0)
    f2 = pltpu.roll(f1, 1, 0) + pltpu.roll(f1, n_ext - 1, 0)
    f4 = pltpu.roll(f2, 2, 0) + pltpu.roll(f2, n_ext - 2, 0)
    f8 = pltpu.roll(f4, 4, 0) + pltpu.roll(f4, n_ext - 4, 0)
    lane_e = _lane_head(n_ext)
    win = jnp.where(lane_e == 0, f1, jnp.where(lane_e == 1, f2, jnp.where(lane_e == 2, f4, f8)))
    win = win[POOL_HALO:POOL_HALO + tm, :]
    pos = i * tm + lax.broadcasted_iota(jnp.int32, (tm, BRANCH_W), 0)
    half = jnp.where(lane_g == 0, POOL_HALF[0],
                     jnp.where(lane_g == 1, POOL_HALF[1], jnp.where(lane_g == 2, POOL_HALF[2], POOL_HALF[3])))
    cnt = (jnp.minimum(pos + half, seq) - jnp.maximum(pos - half, 0)).astype(F32)
    y_b = _dot((win / cnt - u).astype(BF16), poolw_ref[...]) * pools_ref[...]

    u_c = bc[:, BRANCH_W:2 * BRANCH_W]
    v_c = bc[:, 2 * BRANCH_W:3 * BRANCH_W]
    msv = jnp.mean(v_c * v_c, axis=-1, keepdims=True)
    v_n = (v_c * lax.rsqrt(msv + EPS) * sgg_ref[...]).astype(BF16)
    lane_c = _lane_head(CHUNK)
    sv_chunks = []
    for ch in range(tm // CHUNK):
        vch = v_n[ch * CHUNK:(ch + 1) * CHUNK, :]
        sv = jnp.zeros((CHUNK, BRANCH_W), F32)
        for g in range(SG_GROUPS):
            sv = jnp.where(lane_c == g, _dot(sgw_ref[g], vch), sv)
        sv_chunks.append(sv + sgb_ref[...])
    y_c = u_c * jnp.concatenate(sv_chunks, axis=0)

    y_d = yd0_ref[0] + yd1_ref[0]

    merged = jnp.zeros((tm, D_MODEL), F32)
    for n, y in enumerate((y_a, y_b, y_c, y_d)):
        ys = (y * _silu(z[:, n * BRANCH_W:(n + 1) * BRANCH_W])).astype(BF16)
        gate = jax.nn.sigmoid(_dot(h, wg_ref[:, n * D_MODEL:(n + 1) * D_MODEL]))
        merged = merged + gate * _dot(ys, wb_ref[n])
    out_ref[0] = x + _dot(merged.astype(BF16), wo_ref[...])


def _merge(x, attn, z, bc, yd0, yd1, lp, batch, seq):
    tm = MERGE_ROWS
    hb = tm // POOL_HALO
    last_halo = seq // POOL_HALO - 1
    row = lambda w: pl.BlockSpec((1, tm, w), lambda b, i: (b, i, 0))
    const2 = lambda shape: pl.BlockSpec(shape, lambda b, i: (0, 0))
    const3 = lambda shape: pl.BlockSpec(shape, lambda b, i: (0, 0, 0))
    in_specs = [row(D_MODEL), const2((1, D_MODEL))]
    in_specs += [row(BRANCH_W)] * 6
    in_specs += [
        row(Z_W),
        pl.BlockSpec((1, POOL_HALO, BRANCH_W), lambda b, i: (b, jnp.maximum(i * hb - 1, 0), 0)),
        row(BC_W),
        pl.BlockSpec((1, POOL_HALO, BRANCH_W), lambda b, i: (b, jnp.minimum((i + 1) * hb, last_halo), 0)),
        row(BRANCH_W), row(BRANCH_W),
        const2((BRANCH_W, BRANCH_W)), const2((1, BRANCH_W)), const2((1, BRANCH_W)),
        const3((SG_GROUPS, CHUNK, CHUNK)), const2((CHUNK, BRANCH_W)),
        const2((D_MODEL, N_BRANCH * D_MODEL)), const3((N_BRANCH, BRANCH_W, D_MODEL)),
        const2((D_MODEL, D_MODEL)),
    ]
    flat_attn = [a for pair in attn for a in pair]
    return pl.pallas_call(
        functools.partial(_merge_kernel, tm=tm, seq=seq),
        grid=(batch, seq // tm),
        in_specs=in_specs,
        out_specs=row(D_MODEL),
        out_shape=jax.ShapeDtypeStruct((batch, seq, D_MODEL), F32),
        compiler_params=_params("parallel", "parallel"),
        name="merge",
    )(x, lp["norm_g"], *flat_attn, z, bc, bc, bc, yd0, yd1,
      lp["pool_w_bd"], lp["pool_scale"], lp["sg_norm_g"], lp["sg_w"], lp["sg_bias"],
      lp["w_gate"], lp["w_branch"], lp["w_out"])


def _block_diag(blocks):
    n = blocks.shape[0]
    w = blocks.shape[1]
    out = jnp.zeros((n * w, n * w), blocks.dtype)
    for g in range(n):
        out = out.at[g * w:(g + 1) * w, g * w:(g + 1) * w].set(blocks[g])
    return out


def _layer_params(l, norm_g, w_in, q_norm_g, k_norm_g, pool_w, pool_scale, sg_norm_g, sg_w, sg_b,
                  mu_rkv, mu_lat, w0, w_up, a0, a_up, k_k, k_a, r_k, ln_g, ln_b, w_branch, w_out):
    w = w_in[l]
    seg = lambda lo, width: w[:, lo:lo + width]
    w_small = jnp.concatenate([
        seg(0, QKV_W), seg(3840, RKV_W), seg(4608, 2 * LAT_W), seg(2560, BRANCH_W), seg(3072, 2 * BRANCH_W),
        seg(2304, BRANCH_W), seg(2816, BRANCH_W), seg(3584, BRANCH_W), seg(4864, BRANCH_W)], axis=1)
    ones_blocks = jnp.ones((RWKV_HEADS, HEAD_DIM, HEAD_DIM), F32)
    zeros_lat = jnp.zeros((2, LAT_W // 2, BRANCH_W), F32)
    vec3 = lambda p: p[l][:, None, :]
    return {
        "norm_g": norm_g[l][None, :],
        "w_small": w_small.astype(BF16),
        "w_gate": w[:, PROJ_SMALL_W:].astype(BF16),
        "q_gain": jnp.tile(q_norm_g[l], HEADS_PER_GROUP)[None, :],
        "k_gain": jnp.tile(k_norm_g[l], HEADS_PER_GROUP)[None, :],
        "bd_mean": (_block_diag(ones_blocks) / HEAD_DIM).astype(BF16),
        "bd_ones": _block_diag(ones_blocks).astype(BF16),
        "pool_w_bd": _block_diag(pool_w[l]).astype(BF16),
        "pool_scale": pool_scale[l][None, :],
        "sg_norm_g": sg_norm_g[l][None, :],
        "sg_w": sg_w[l].astype(BF16),
        "sg_bias": jnp.repeat(sg_b[l].T, HEAD_DIM, axis=1),
        "mu_rkv": vec3(mu_rkv), "mu_lat": vec3(mu_lat), "w0": vec3(w0), "a0": vec3(a0),
        "k_k": vec3(k_k), "k_a": vec3(k_a), "r_k": vec3(r_k),
        "w_up_pad": jnp.concatenate([w_up[l], zeros_lat], axis=1),
        "a_up_pad": jnp.concatenate([zeros_lat, a_up[l]], axis=1),
        "ln_g": ln_g[l][None, :], "ln_b": ln_b[l][None, :],
        "w_branch": w_branch[l].astype(BF16),
        "w_out": w_out[l].astype(BF16),
    }


def _layer(x, lp):
    batch, seq, _ = x.shape
    qkv, rkv, lat, bc, z = _proj(x.reshape(batch * seq, D_MODEL), lp)
    attn = [_attn_group(qkv, g, batch, seq) for g in range(N_GROUPS)]
    rkv = rkv.reshape(batch, seq, RKV_W)
    lat = lat.reshape(batch, seq, 2 * LAT_W)
    yd0, yd1 = _rwkv(rkv, lat, lp, batch, seq)
    return _merge(x, attn, z.reshape(batch, seq, Z_W), bc.reshape(batch, seq, BC_W), yd0, yd1, lp, batch, seq)


def kernel(x_prompt, x_sample, norm_g, w_in, q_norm_g, k_norm_g, pool_w, pool_scale, sg_norm_g, sg_w, sg_b,
           mu_rkv, mu_lat, w0, w_up, a0, a_up, k_k, k_a, r_k, ln_g, ln_b, w_branch, w_out):
    weights = (norm_g, w_in, q_norm_g, k_norm_g, pool_w, pool_scale, sg_norm_g, sg_w, sg_b,
               mu_rkv, mu_lat, w0, w_up, a0, a_up, k_k, k_a, r_k, ln_g, ln_b, w_branch, w_out)
    layers = [_layer_params(l, *weights) for l in range(norm_g.shape[0])]
    outs = []
    for x in (x_prompt, x_sample):
        for lp in layers:
            x = _layer(x, lp)
        outs.append(x)
    return tuple(outs)
```

```python
import functools

import jax
import jax.numpy as jnp
import numpy as np
from jax import lax
from jax.experimental import pallas as pl
from jax.experimental.pallas import tpu as pltpu

F32 = jnp.float32
BF16 = jnp.bfloat16

D_MODEL = 1024
BRANCH_W = 256
N_BRANCH = 4
EPS = 1e-6
NEG_INF = -1e30

HEAD_DIM = 64
ATTN_DILATIONS = (1, 4, 16)
N_GROUPS = 3
HEADS_PER_GROUP = 4
N_ATTN_HEADS = N_GROUPS * HEADS_PER_GROUP
ATTN_QKV = N_ATTN_HEADS * HEAD_DIM
HALF_KEYS = 64
Q_SUB = 128
K_WIN = Q_SUB + 2 * HALF_KEYS

POOL_HALF = (1, 2, 4, 8)
POOL_HALO = 8
CHUNK = 128
SG_GROUPS = 4

RWKV_HEADS = 4
RWKV_N = 64
RWKV_CHUNK = 64
LAT_W = 128
GN_EPS = 64e-5
NEUMANN_DOUBLINGS = 5

QKV_W = 3 * ATTN_QKV
RKV_W = 3 * BRANCH_W
BC_W = 3 * BRANCH_W
Z_W = 4 * BRANCH_W
PROJ_SMALL_W = QKV_W + RKV_W + 2 * LAT_W + BC_W + Z_W

PROJ_ROWS = 256
MERGE_ROWS = 256
PERM_ROWS = 256
ATTN_ROWS = 512
RWKV_ROWS = 512
VMEM_LIMIT = 56 * 1024 * 1024


def _params(*sem):
    return pltpu.CompilerParams(dimension_semantics=sem, vmem_limit_bytes=VMEM_LIMIT)


def _dot(a, b):
    return jnp.dot(a, b, preferred_element_type=F32)


def _split_bf16(x):
    hi = x.astype(BF16)
    lo = (x - hi.astype(F32)).astype(BF16)
    return hi, lo


def _dot_split_lhs(x, m_bf16):
    hi, lo = _split_bf16(x)
    return _dot(hi, m_bf16) + _dot(lo, m_bf16)


def _dot_split_lhs_rhs(m_bf16, x):
    hi, lo = _split_bf16(x)
    return _dot(m_bf16, hi) + _dot(m_bf16, lo)


def _dot_split_both(a, b):
    a_hi, a_lo = _split_bf16(a)
    b_hi, b_lo = _split_bf16(b)
    return _dot(a_hi, b_hi) + _dot(a_lo, b_hi) + _dot(a_hi, b_lo)


def _lane_head(rows):
    return lax.broadcasted_iota(jnp.int32, (rows, BRANCH_W), 1) // HEAD_DIM


def _proj_kernel(x_ref, g_ref, w_ref, qg_ref, kg_ref, bd_ref, perm_ref, qkv_ref, rkv_ref, lat_ref, bc_ref,
                 z_ref):
    x = x_ref[...]
    ms = jnp.mean(x * x, axis=-1, keepdims=True)
    h = (x * lax.rsqrt(ms + EPS) * g_ref[...]).astype(BF16)
    bd = bd_ref[...]
    n_chunks = 3 * N_GROUPS

    def project(c):
        return _dot(h, w_ref[:, c * BRANCH_W:(c + 1) * BRANCH_W])

    def normalize(c, y):
        if c < 2 * N_GROUPS:
            msq = _dot_split_lhs(y * y, bd)
            if c < N_GROUPS:
                y = y * lax.rsqrt(msq + EPS) * qg_ref[...] * (HEAD_DIM ** -0.5)
            else:
                y = y * lax.rsqrt(msq + EPS) * kg_ref[...]
        return y.astype(BF16)

    def store(c, yb):
        group = c % N_GROUPS
        if group > 0:
            yb = _dot(perm_ref[group - 1], yb).astype(BF16)
        qkv_ref[:, c * BRANCH_W:(c + 1) * BRANCH_W] = yb

    others = []
    o = QKV_W
    for ref, width in ((rkv_ref, RKV_W), (lat_ref, 2 * LAT_W), (bc_ref, BC_W), (z_ref, Z_W)):
        others.append((ref, o, width))
        o += width

    projected, normalized = {}, {}
    for step in range(n_chunks + 2):
        if step < n_chunks:
            projected[step] = project(step)
        elif others:
            ref, lo, width = others.pop(0)
            ref[...] = _dot(h, w_ref[:, lo:lo + width])
        if 1 <= step <= n_chunks:
            normalized[step - 1] = normalize(step - 1, projected.pop(step - 1))
        if step >= 2:
            store(step - 2, normalized.pop(step - 2))
    for ref, lo, width in others:
        ref[...] = _dot(h, w_ref[:, lo:lo + width])


def _residue_perms(transpose):
    mats = []
    for dil in ATTN_DILATIONS[1:]:
        per = PERM_ROWS // dil
        p = np.zeros((PERM_ROWS, PERM_ROWS), np.float32)
        old = np.arange(PERM_ROWS)
        p[(old % dil) * per + old // dil, old] = 1.0
        mats.append(p.T if transpose else p)
    return jnp.asarray(np.stack(mats), BF16)


def _proj(x2d, lp):
    rows = x2d.shape[0]
    tm = PROJ_ROWS
    assert tm == PERM_ROWS
    const = lambda i: (0, 0)
    row = lambda i: (i, 0)
    widths = (QKV_W, RKV_W, 2 * LAT_W, BC_W, Z_W)
    dtypes = (BF16, F32, F32, F32, F32)
    return pl.pallas_call(
        _proj_kernel,
        grid=(rows // tm,),
        in_specs=[
            pl.BlockSpec((tm, D_MODEL), row),
            pl.BlockSpec((1, D_MODEL), const),
            pl.BlockSpec((D_MODEL, PROJ_SMALL_W), const),
            pl.BlockSpec((1, BRANCH_W), const),
            pl.BlockSpec((1, BRANCH_W), const),
            pl.BlockSpec((BRANCH_W, BRANCH_W), const),
            pl.BlockSpec((N_GROUPS - 1, tm, tm), lambda i: (0, 0, 0)),
        ],
        out_specs=[pl.BlockSpec((tm, w), row) for w in widths],
        out_shape=[jax.ShapeDtypeStruct((rows, w), dt) for w, dt in zip(widths, dtypes)],
        compiler_params=_params("parallel"),
        name="proj",
    )(x2d, lp["norm_g"], lp["w_small"], lp["q_gain"], lp["k_gain"], lp["bd_mean"], _residue_perms(False))


def _attn_kernel(q_ref, kp_ref, k_ref, kn_ref, vp_ref, v_ref, vn_ref, bias_ref, o_ref, lse_ref,
                 qbuf, kbuf, vbuf, *, tq, seq_m):
    i = pl.program_id(2)

    def gather(dst, at, src):
        per = src.shape[2]
        for j in range(src.shape[1]):
            dst[at + j * per:at + (j + 1) * per, :] = src[0, j]

    gather(qbuf, 0, q_ref)
    gather(kbuf, 0, kp_ref)
    gather(kbuf, HALF_KEYS, k_ref)
    gather(kbuf, HALF_KEYS + tq, kn_ref)
    gather(vbuf, 0, vp_ref)
    gather(vbuf, HALF_KEYS, v_ref)
    gather(vbuf, HALF_KEYS + tq, vn_ref)
    per_out = o_ref.shape[2]
    lane_h = _lane_head(Q_SUB)
    col = lax.broadcasted_iota(jnp.int32, (1, K_WIN), 1)
    bias = bias_ref[...]
    for sb in range(tq // Q_SUB):
        r0 = sb * Q_SUB
        qb = qbuf[r0:r0 + Q_SUB, :]
        qs = jnp.concatenate(
            [jnp.where(lane_h == h, qb, jnp.zeros_like(qb)) for h in range(HEADS_PER_GROUP)], axis=0)
        s = lax.dot_general(qs, kbuf[r0:r0 + K_WIN, :], (((1,), (1,)), ((), ())),
                            preferred_element_type=F32)
        kpos = i * tq + (r0 - HALF_KEYS) + col
        s = s + bias + jnp.where((kpos >= 0) & (kpos < seq_m), 0.0, NEG_INF)
        m = jnp.max(s, axis=-1, keepdims=True)
        p = jnp.exp(s - m)
        l = jnp.sum(p, axis=-1, keepdims=True)
        pv = _dot(p.astype(BF16), vbuf[r0:r0 + K_WIN, :])
        on = pv / l
        ls = m + jnp.log(l)
        o = jnp.zeros((Q_SUB, BRANCH_W), F32)
        lo = jnp.zeros((Q_SUB, BRANCH_W), F32)
        for h in range(HEADS_PER_GROUP):
            sel = lane_h == h
            o = jnp.where(sel, on[h * Q_SUB:(h + 1) * Q_SUB, :], o)
            lo = jnp.where(sel, ls[h * Q_SUB:(h + 1) * Q_SUB, :], lo)
        if per_out >= Q_SUB:
            o_ref[0, r0 // per_out, r0 % per_out:r0 % per_out + Q_SUB, :] = o.astype(o_ref.dtype)
            lse_ref[0, r0 // per_out, r0 % per_out:r0 % per_out + Q_SUB, :] = lo
        for j in range(Q_SUB // per_out):
            o_ref[0, r0 // per_out + j] = o[j * per_out:(j + 1) * per_out, :].astype(o_ref.dtype)
            lse_ref[0, r0 // per_out + j] = lo[j * per_out:(j + 1) * per_out, :]


def _attn_bias(group):
    dil = ATTN_DILATIONS[group]
    i = np.arange(Q_SUB)[:, None]
    j = np.arange(K_WIN)[None, :]
    off = np.abs(j - HALF_KEYS - i).astype(np.float32)
    slopes = 2.0 ** (-8.0 * np.arange(1, N_ATTN_HEADS + 1, dtype=np.float32) / N_ATTN_HEADS)
    blocks = []
    for h in range(HEADS_PER_GROUP):
        slope = slopes[group * HEADS_PER_GROUP + h]
        blocks.append(np.where(off <= HALF_KEYS, -slope * (off * dil), NEG_INF))
    return jnp.asarray(np.concatenate(blocks, axis=0), F32)


def _attn_group(qkv, group, batch, seq):
    dil = ATTN_DILATIONS[group]
    seq_m = seq // dil
    tq = min(ATTN_ROWS, seq_m)
    nq = seq_m // tq
    per = PERM_ROWS // dil
    halo_per = min(per, HALF_KEYS)
    halo_split = per // halo_per
    last_halo = seq_m // HALF_KEYS - 1
    view = qkv.reshape(batch, seq // PERM_ROWS, dil, per, QKV_W)

    def col(which):
        return which * N_GROUPS + group

    def main(which):
        return pl.BlockSpec((1, tq // per, None, per, BRANCH_W),
                            lambda b, r, i: (b, i, r, 0, col(which)))

    def halo(which, index):
        if halo_split == 1:
            return pl.BlockSpec((1, HALF_KEYS // per, None, per, BRANCH_W),
                                lambda b, r, i: (b, index(i), r, 0, col(which)))
        return pl.BlockSpec((1, 1, None, HALF_KEYS, BRANCH_W),
                            lambda b, r, i: (b, index(i) // halo_split, r, index(i) % halo_split, col(which)))

    hb = tq // HALF_KEYS
    prev = lambda i: jnp.maximum(i * hb - 1, 0)
    nxt = lambda i: jnp.minimum((i + 1) * hb, last_halo)
    out_spec = pl.BlockSpec((1, tq // per, None, per, BRANCH_W), lambda b, r, i: (b, i, r, 0, 0))
    out_dims = (batch, seq // PERM_ROWS, dil, per, BRANCH_W)
    buf = lambda rows: pltpu.VMEM((rows, BRANCH_W), BF16)
    return pl.pallas_call(
        functools.partial(_attn_kernel, tq=tq, seq_m=seq_m),
        grid=(batch, dil, nq),
        in_specs=[main(0), halo(1, prev), main(1), halo(1, nxt), halo(2, prev), main(2), halo(2, nxt),
                  pl.BlockSpec((HEADS_PER_GROUP * Q_SUB, K_WIN), lambda b, r, i: (0, 0))],
        out_specs=[out_spec, out_spec],
        out_shape=[jax.ShapeDtypeStruct(out_dims, BF16), jax.ShapeDtypeStruct(out_dims, F32)],
        scratch_shapes=[buf(tq), buf(tq + 2 * HALF_KEYS), buf(tq + 2 * HALF_KEYS)],
        compiler_params=_params("parallel", "parallel", "parallel"),
        name=f"attn_g{group}",
    )(view, view, view, view, view, view, view, _attn_bias(group))


def _softplus(y):
    return jnp.maximum(y, 0.0) + jnp.log(1.0 + jnp.exp(-jnp.abs(y)))


def _stack_heads(xb, lane_h):
    return jnp.concatenate(
        [jnp.where(lane_h == h, xb, jnp.zeros_like(xb)) for h in range(RWKV_HEADS)], axis=0)


def _nt(a, b):
    return lax.dot_general(a, b, (((1,), (1,)), ((), ())), preferred_element_type=F32)


def _rwkv_kernel(rkvf_ref, rkvfh_ref, rkvb_ref, rkvbh_ref, latf_ref, latfh_ref, latb_ref, latbh_ref,
                 mur_ref, mul_ref, w0_ref, wup_ref, a0_ref, aup_ref, kk_ref, ka_ref, rk_ref,
                 lng_ref, lnb_ref, bdm_ref, bd1_ref, outf_ref, outb_ref,
                 st_ref, r_s, k_s, v_s, p_s, b_s, lw_s, o_s, *, batch, tc):
    i = pl.program_id(0)
    c = RWKV_CHUNK
    n_chunks = tc // c
    streams = [(d, b) for d in range(2) for b in range(batch)]

    @pl.when(i == 0)
    def _():
        st_ref[...] = jnp.zeros_like(st_ref)

    def shifted(x, halo, backward):
        row = lax.broadcasted_iota(jnp.int32, x.shape, 0)
        if backward:
            edge = jnp.where(i == 0, 0.0, halo[0:1, :])
            return jnp.where(row == tc - 1, edge, pltpu.roll(x, tc - 1, 0))
        edge = jnp.where(i == 0, 0.0, halo[7:8, :])
        return jnp.where(row == 0, edge, pltpu.roll(x, 1, 0))

    bd1 = bd1_ref[...]
    for s, (d, b) in enumerate(streams):
        backward = d == 1
        x_ref, xh_ref, l_ref, lh_ref = ((rkvb_ref, rkvbh_ref, latb_ref, latbh_ref) if backward
                                        else (rkvf_ref, rkvfh_ref, latf_ref, latfh_ref))
        x = x_ref[b]
        xr = x + (shifted(x, xh_ref[b], backward) - x) * mur_ref[d]
        lt = l_ref[b]
        xl = lt + (shifted(lt, lh_ref[b], backward) - lt) * mul_ref[d]
        r = xr[:, 0:BRANCH_W]
        k = xr[:, BRANCH_W:2 * BRANCH_W]
        w_log = -_softplus(-(w0_ref[d] + _dot_split_both(jnp.tanh(xl), wup_ref[d]))) - 0.5
        a = jax.nn.sigmoid(a0_ref[d] + _dot_split_both(xl, aup_ref[d]))
        kk = k * kk_ref[d]
        kk = kk * lax.rsqrt(_dot_split_lhs(kk * kk, bd1) + 1e-12)
        r_s[s] = r
        k_s[s] = k * (1.0 + (a - 1.0) * ka_ref[d])
        v_s[s] = xr[:, 2 * BRANCH_W:3 * BRANCH_W]
        p_s[s] = -kk
        b_s[s] = kk * a
        lw_s[s] = -jnp.exp(w_log)

    lane_h = _lane_head(c)
    t_i = lax.broadcasted_iota(jnp.int32, (c, c), 0)
    s_i = lax.broadcasted_iota(jnp.int32, (c, c), 1)
    tau = lax.broadcasted_iota(jnp.int32, (c, BRANCH_W), 0)
    sig = lax.broadcasted_iota(jnp.int32, (c, BRANCH_W), 1) % c
    eye = sig == tau
    eye_b = jnp.where(eye, 1.0, 0.0).astype(BF16)

    tri_f = jnp.where(s_i <= t_i, 1.0, 0.0).astype(BF16)
    tri_b = jnp.where(s_i >= t_i, 1.0, 0.0).astype(BF16)
    sm = lambda xb: _stack_heads(xb, lane_h)
    bf = lambda x: x.astype(BF16)
    top = lambda x: x[0:c]
    mid = lambda x: x[c:2 * c]

    def chunks(j, carry):
        ids = range(len(streams))
        back = [d == 1 for d, _ in streams]
        rows = [pl.ds(pl.multiple_of(((n_chunks - 1 - j) if bk else j) * c, c), c) for bk in back]
        strict = [(sig > tau) if bk else (sig < tau) for bk in back]
        incl = [(sig >= tau) if bk else (sig <= tau) for bk in back]

        def cumsum(s):
            lw = lw_s[s, rows[s], :]
            lw_hi = bf(lw)
            lw_r = lw - lw_hi.astype(F32)
            lw_mid = bf(lw_r)
            lw_lo = bf(lw_r - lw_mid.astype(F32))
            tri = tri_b if back[s] else tri_f
            return _dot(tri, lw_hi) + _dot(tri, lw_mid) + _dot(tri, lw_lo)

        cumi = [cumsum(s) for s in ids]
        tot = [cumi[s][0:1, :] if back[s] else cumi[s][c - 1:c, :] for s in ids]
        e_m = [jnp.exp(-cumi[s]) for s in ids]
        e_t = [jnp.exp(tot[s] - cumi[s]) for s in ids]
        bq = [b_s[s, rows[s], :] for s in ids]
        kq = [k_s[s, rows[s], :] for s in ids]
        pt = [bf(p_s[s, rows[s], :] * jnp.exp(cumi[s] - lw_s[s, rows[s], :])) for s in ids]
        rt = [r_s[s, rows[s], :] * jnp.exp(cumi[s]) for s in ids]
        pr = [jnp.concatenate([pt[s], bf(rt[s])], axis=0) for s in ids]
        ab = [_nt(pr[s], sm(bf(bq[s] * e_m[s]))) for s in ids]
        ak = [_nt(pr[s], sm(bf(kq[s] * e_m[s]))) for s in ids]
        a_pb = [jnp.where(strict[s], top(ab[s]), 0.0) for s in ids]
        a_rb = [bf(jnp.where(incl[s], mid(ab[s]), 0.0)) for s in ids]
        a_pk = [bf(jnp.where(strict[s], top(ak[s]), 0.0)) for s in ids]
        a_rk = [bf(jnp.where(incl[s], mid(ak[s]), 0.0)) for s in ids]
        l_b = [bf(_nt(eye_b, sm(bf(bq[s] * e_t[s])))) for s in ids]
        l_k = [bf(_nt(eye_b, sm(bf(kq[s] * e_t[s])))) for s in ids]

        tinv = [jnp.where(eye, 1.0, a_pb[s]) for s in ids]
        apow = [bf(a_pb[s]) for s in ids]
        a2 = [_dot(apow[s], sm(apow[s])) for s in ids]
        for _ in range(NEUMANN_DOUBLINGS - 1):
            apow = [bf(a2[s]) for s in ids]
            both = [_dot(jnp.concatenate([bf(tinv[s]), apow[s]], axis=0), sm(apow[s])) for s in ids]
            tinv = [tinv[s] + top(both[s]) for s in ids]
            a2 = [mid(both[s]) for s in ids]
        tinv = [tinv[s] + _dot(bf(tinv[s]), sm(bf(a2[s]))) for s in ids]
        tb = [bf(t) for t in tinv]

        av = [_dot(jnp.concatenate([a_pk[s], a_rk[s], l_k[s]], axis=0), sm(bf(v_s[s, rows[s], :])))
              for s in ids]
        ah = [_dot(tb[s], sm(pt[s])) for s in ids]
        uh = [_dot(tb[s], sm(bf(top(av[s])))) for s in ids]
        lhs = [jnp.concatenate([a_rb[s], l_b[s]], axis=0) for s in ids]
        o_a = [_dot(lhs[s], sm(bf(ah[s]))) for s in ids]
        o_u = [_dot(lhs[s], sm(bf(uh[s]))) for s in ids]
        rh = [rt[s] + top(o_a[s]) for s in ids]
        mt = [jnp.where(eye, jnp.exp(tot[s]), 0.0) + mid(o_a[s]) for s in ids]
        oh = [mid(av[s]) + top(o_u[s]) for s in ids]
        gt = [mid(o_u[s]) + av[s][2 * c:3 * c] for s in ids]

        out = [_dot(jnp.concatenate([bf(rh[s]), bf(mt[s])], axis=0), sm(bf(st_ref[s]))) for s in ids]
        for s in ids:
            o_s[s, rows[s], :] = top(out[s]) + oh[s]
            st_ref[s] = mid(out[s]) + gt[s]
        return carry

    lax.fori_loop(0, n_chunks, chunks, 0)

    bdm = bdm_ref[...]
    for s, (d, b) in enumerate(streams):
        o = o_s[s]
        xc = o - _dot_split_lhs(o, bdm)
        var = _dot((xc * xc).astype(BF16), bdm)
        bonus = _dot((r_s[s] * k_s[s] * rk_ref[d]).astype(BF16), bd1) * v_s[s]
        y = xc * lax.rsqrt(var + GN_EPS) * lng_ref[...] + lnb_ref[...] + bonus
        if d == 1:
            outb_ref[b] = y
        else:
            outf_ref[b] = y


def _rwkv(rkv, lat, lp, batch, seq):
    tc = min(RWKV_ROWS, seq)
    nt = seq // tc
    hb = tc // 8
    last_halo = seq // 8 - 1
    fwd = lambda i: i
    bwd = lambda i: nt - 1 - i
    fwd_halo = lambda i: jnp.maximum(i * hb - 1, 0)
    bwd_halo = lambda i: jnp.minimum((bwd(i) + 1) * hb, last_halo)
    full3 = lambda shape: pl.BlockSpec(shape, lambda i: (0, 0, 0))
    full2 = lambda shape: pl.BlockSpec(shape, lambda i: (0, 0))
    n_streams = 2 * batch
    scr = pltpu.VMEM((n_streams, tc, BRANCH_W), F32)
    out_shape = jax.ShapeDtypeStruct((batch, seq, BRANCH_W), F32)
    return pl.pallas_call(
        functools.partial(_rwkv_kernel, batch=batch, tc=tc),
        grid=(nt,),
        in_specs=[
            pl.BlockSpec((batch, tc, RKV_W), lambda i: (0, fwd(i), 0)),
            pl.BlockSpec((batch, 8, RKV_W), lambda i: (0, fwd_halo(i), 0)),
            pl.BlockSpec((batch, tc, RKV_W), lambda i: (0, bwd(i), 0)),
            pl.BlockSpec((batch, 8, RKV_W), lambda i: (0, bwd_halo(i), 0)),
            pl.BlockSpec((batch, tc, LAT_W), lambda i: (0, fwd(i), 0)),
            pl.BlockSpec((batch, 8, LAT_W), lambda i: (0, fwd_halo(i), 0)),
            pl.BlockSpec((batch, tc, LAT_W), lambda i: (0, bwd(i), 1)),
            pl.BlockSpec((batch, 8, LAT_W), lambda i: (0, bwd_halo(i), 1)),
            full3((2, 1, RKV_W)), full3((2, 1, LAT_W)), full3((2, 1, BRANCH_W)),
            full3((2, LAT_W, BRANCH_W)), full3((2, 1, BRANCH_W)), full3((2, LAT_W, BRANCH_W)),
            full3((2, 1, BRANCH_W)), full3((2, 1, BRANCH_W)), full3((2, 1, BRANCH_W)),
            full2((1, BRANCH_W)), full2((1, BRANCH_W)),
            full2((BRANCH_W, BRANCH_W)), full2((BRANCH_W, BRANCH_W)),
        ],
        out_specs=[pl.BlockSpec((batch, tc, BRANCH_W), lambda i: (0, fwd(i), 0)),
                   pl.BlockSpec((batch, tc, BRANCH_W), lambda i: (0, bwd(i), 0))],
        out_shape=[out_shape, out_shape],
        scratch_shapes=[pltpu.VMEM((n_streams, RWKV_CHUNK, BRANCH_W), F32)] + [scr] * 7,
        compiler_params=_params("arbitrary"),
        name="rwkv",
    )(rkv, rkv, rkv, rkv, lat, lat, lat, lat, lp["mu_rkv"], lp["mu_lat"], lp["w0"], lp["w_up_pad"],
      lp["a0"], lp["a_up_pad"], lp["k_k"], lp["k_a"], lp["r_k"], lp["ln_g"], lp["ln_b"],
      lp["bd_mean"], lp["bd_ones"])


def _silu(z):
    return z * jax.nn.sigmoid(z)


def _merge_kernel(x_ref, g_ref, o0_ref, l0_ref, o1_ref, l1_ref, o2_ref, l2_ref, z_ref,
                  ubp_ref, bc_ref, ubn_ref, yd0_ref, yd1_ref,
                  unperm_ref, poolw_ref, pools_ref, sgg_ref, sgw_ref, sgb_ref, wg_ref, wb_ref, wo_ref,
                  out_ref, *, tm, seq):
    i = pl.program_id(1)
    n_tiles = seq // tm
    x = x_ref[0]
    ms = jnp.mean(x * x, axis=-1, keepdims=True)
    h = (x * lax.rsqrt(ms + EPS) * g_ref[...]).astype(BF16)
    z = z_ref[0]
    lane_g = _lane_head(tm)

    def natural(ref, group, exact):
        if group == 0:
            return ref[0, 0, 0].astype(F32)
        v = jnp.concatenate([ref[0, 0, r] for r in range(ATTN_DILATIONS[group])], axis=0)
        unperm = unperm_ref[group - 1]
        return _dot_split_lhs_rhs(unperm, v) if exact else _dot(unperm, v)

    l0, l1, l2 = natural(l0_ref, 0, True), natural(l1_ref, 1, True), natural(l2_ref, 2, True)
    mx = jnp.maximum(jnp.maximum(l0, l1), l2)
    e0, e1, e2 = jnp.exp(l0 - mx), jnp.exp(l1 - mx), jnp.exp(l2 - mx)
    y_a = (e0 * natural(o0_ref, 0, False) + e1 * natural(o1_ref, 1, False)
           + e2 * natural(o2_ref, 2, False)) / (e0 + e1 + e2)

    bc = bc_ref[0]
    u = bc[:, 0:BRANCH_W]
    prev = jnp.where(i == 0, 0.0, ubp_ref[0][:, 0:BRANCH_W])
    nxt = jnp.where(i == n_tiles - 1, 0.0, ubn_ref[0][:, 0:BRANCH_W])
    ext = jnp.concatenate([prev, u, nxt], axis=0)
    n_ext = tm + 2 * POOL_HALO
    f1 = ext + pltpu.roll(ext, 1, 0)
    f2 = pltpu.roll(f1, 1, 0) + pltpu.roll(f1, n_ext - 1, 0)
    f4 = pltpu.roll(f2, 2, 0) + pltpu.roll(f2, n_ext - 2, 0)
    f8 = pltpu.roll(f4, 4, 0) + pltpu.roll(f4, n_ext - 4, 0)
    lane_e = _lane_head(n_ext)
    win = jnp.where(lane_e == 0, f1, jnp.where(lane_e == 1, f2, jnp.where(lane_e == 2, f4, f8)))
    win = win[POOL_HALO:POOL_HALO + tm, :]
    pos = i * tm + lax.broadcasted_iota(jnp.int32, (tm, BRANCH_W), 0)
    half = jnp.where(lane_g == 0, POOL_HALF[0],
                     jnp.where(lane_g == 1, POOL_HALF[1], jnp.where(lane_g == 2, POOL_HALF[2], POOL_HALF[3])))
    cnt = (jnp.minimum(pos + half, seq) - jnp.maximum(pos - half, 0)).astype(F32)
    y_b = _dot((win / cnt - u).astype(BF16), poolw_ref[...]) * pools_ref[...]

    u_c = bc[:, BRANCH_W:2 * BRANCH_W]
    v_c = bc[:, 2 * BRANCH_W:3 * BRANCH_W]
    msv = jnp.mean(v_c * v_c, axis=-1, keepdims=True)
    v_n = (v_c * lax.rsqrt(msv + EPS) * sgg_ref[...]).astype(BF16)
    lane_c = _lane_head(CHUNK)
    sv_chunks = []
    for ch in range(tm // CHUNK):
        vch = v_n[ch * CHUNK:(ch + 1) * CHUNK, :]
        sv = jnp.zeros((CHUNK, BRANCH_W), F32)
        for g in range(SG_GROUPS):
            sv = jnp.where(lane_c == g, _dot(sgw_ref[g], vch), sv)
        sv_chunks.append(sv + sgb_ref[...])
    y_c = u_c * jnp.concatenate(sv_chunks, axis=0)

    y_d = yd0_ref[0] + yd1_ref[0]

    merged = jnp.zeros((tm, D_MODEL), F32)
    for n, y in enumerate((y_a, y_b, y_c, y_d)):
        ys = (y * _silu(z[:, n * BRANCH_W:(n + 1) * BRANCH_W])).astype(BF16)
        gate = jax.nn.sigmoid(_dot(h, wg_ref[:, n * D_MODEL:(n + 1) * D_MODEL]))
        merged = merged + gate * _dot(ys, wb_ref[n])
    out_ref[0] = x + _dot(merged.astype(BF16), wo_ref[...])


def _merge(x, attn, z, bc, yd0, yd1, lp, batch, seq):
    tm = MERGE_ROWS
    hb = tm // POOL_HALO
    last_halo = seq // POOL_HALO - 1
    row = lambda w: pl.BlockSpec((1, tm, w), lambda b, i: (b, i, 0))
    const2 = lambda shape: pl.BlockSpec(shape, lambda b, i: (0, 0))
    const3 = lambda shape: pl.BlockSpec(shape, lambda b, i: (0, 0, 0))
    assert tm == PERM_ROWS
    in_specs = [row(D_MODEL), const2((1, D_MODEL))]
    for dil in ATTN_DILATIONS:
        slab = pl.BlockSpec((1, 1, dil, tm // dil, BRANCH_W), lambda b, i: (b, i, 0, 0, 0))
        in_specs += [slab, slab]
    in_specs += [
        row(Z_W),
        pl.BlockSpec((1, POOL_HALO, BRANCH_W), lambda b, i: (b, jnp.maximum(i * hb - 1, 0), 0)),
        row(BC_W),
        pl.BlockSpec((1, POOL_HALO, BRANCH_W), lambda b, i: (b, jnp.minimum((i + 1) * hb, last_halo), 0)),
        row(BRANCH_W), row(BRANCH_W),
        const3((N_GROUPS - 1, tm, tm)),
        const2((BRANCH_W, BRANCH_W)), const2((1, BRANCH_W)), const2((1, BRANCH_W)),
        const3((SG_GROUPS, CHUNK, CHUNK)), const2((CHUNK, BRANCH_W)),
        const2((D_MODEL, N_BRANCH * D_MODEL)), const3((N_BRANCH, BRANCH_W, D_MODEL)),
        const2((D_MODEL, D_MODEL)),
    ]
    flat_attn = [a for pair in attn for a in pair]
    return pl.pallas_call(
        functools.partial(_merge_kernel, tm=tm, seq=seq),
        grid=(batch, seq // tm),
        in_specs=in_specs,
        out_specs=row(D_MODEL),
        out_shape=jax.ShapeDtypeStruct((batch, seq, D_MODEL), F32),
        compiler_params=_params("parallel", "parallel"),
        name="merge",
    )(x, lp["norm_g"], *flat_attn, z, bc, bc, bc, yd0, yd1, _residue_perms(True),
      lp["pool_w_bd"], lp["pool_scale"], lp["sg_norm_g"], lp["sg_w"], lp["sg_bias"],
      lp["w_gate"], lp["w_branch"], lp["w_out"])


def _block_diag(blocks):
    n = blocks.shape[0]
    w = blocks.shape[1]
    out = jnp.zeros((n * w, n * w), blocks.dtype)
    for g in range(n):
        out = out.at[g * w:(g + 1) * w, g * w:(g + 1) * w].set(blocks[g])
    return out


def _layer_params(l, norm_g, w_in, q_norm_g, k_norm_g, pool_w, pool_scale, sg_norm_g, sg_w, sg_b,
                  mu_rkv, mu_lat, w0, w_up, a0, a_up, k_k, k_a, r_k, ln_g, ln_b, w_branch, w_out):
    w = w_in[l]
    seg = lambda lo, width: w[:, lo:lo + width]
    w_small = jnp.concatenate([
        seg(0, QKV_W), seg(3840, RKV_W), seg(4608, 2 * LAT_W), seg(2560, BRANCH_W), seg(3072, 2 * BRANCH_W),
        seg(2304, BRANCH_W), seg(2816, BRANCH_W), seg(3584, BRANCH_W), seg(4864, BRANCH_W)], axis=1)
    ones_blocks = jnp.ones((RWKV_HEADS, HEAD_DIM, HEAD_DIM), F32)
    zeros_lat = jnp.zeros((2, LAT_W // 2, BRANCH_W), F32)
    vec3 = lambda p: p[l][:, None, :]
    return {
        "norm_g": norm_g[l][None, :],
        "w_small": w_small.astype(BF16),
        "w_gate": w[:, PROJ_SMALL_W:].astype(BF16),
        "q_gain": jnp.tile(q_norm_g[l], HEADS_PER_GROUP)[None, :],
        "k_gain": jnp.tile(k_norm_g[l], HEADS_PER_GROUP)[None, :],
        "bd_mean": (_block_diag(ones_blocks) / HEAD_DIM).astype(BF16),
        "bd_ones": _block_diag(ones_blocks).astype(BF16),
        "pool_w_bd": _block_diag(pool_w[l]).astype(BF16),
        "pool_scale": pool_scale[l][None, :],
        "sg_norm_g": sg_norm_g[l][None, :],
        "sg_w": sg_w[l].astype(BF16),
        "sg_bias": jnp.repeat(sg_b[l].T, HEAD_DIM, axis=1),
        "mu_rkv": vec3(mu_rkv), "mu_lat": vec3(mu_lat), "w0": vec3(w0), "a0": vec3(a0),
        "k_k": vec3(k_k), "k_a": vec3(k_a), "r_k": vec3(r_k),
        "w_up_pad": jnp.concatenate([w_up[l], zeros_lat], axis=1),
        "a_up_pad": jnp.concatenate([zeros_lat, a_up[l]], axis=1),
        "ln_g": ln_g[l][None, :], "ln_b": ln_b[l][None, :],
        "w_branch": w_branch[l].astype(BF16),
        "w_out": w_out[l].astype(BF16),
    }


def _layer(x, lp):
    batch, seq, _ = x.shape
    qkv, rkv, lat, bc, z = _proj(x.reshape(batch * seq, D_MODEL), lp)
    attn = [_attn_group(qkv, g, batch, seq) for g in range(N_GROUPS)]
    rkv = rkv.reshape(batch, seq, RKV_W)
    lat = lat.reshape(batch, seq, 2 * LAT_W)
    yd0, yd1 = _rwkv(rkv, lat, lp, batch, seq)
    return _merge(x, attn, z.reshape(batch, seq, Z_W), bc.reshape(batch, seq, BC_W), yd0, yd1, lp, batch, seq)


def kernel(x_prompt, x_sample, norm_g, w_in, q_norm_g, k_norm_g, pool_w, pool_scale, sg_norm_g, sg_w, sg_b,
           mu_rkv, mu_lat, w0, w_up, a0, a_up, k_k, k_a, r_k, ln_g, ln_b, w_branch, w_out):
    weights = (norm_g, w_in, q_norm_g, k_norm_g, pool_w, pool_scale, sg_norm_g, sg_w, sg_b,
               mu_rkv, mu_lat, w0, w_up, a0, a_up, k_k, k_a, r_k, ln_g, ln_b, w_branch, w_out)
    layers = [_layer_params(l, *weights) for l in range(norm_g.shape[0])]
    outs = []
    for x in (x_prompt, x_sample):
        for lp in layers:
            x = _layer(x, lp)
        outs.append(x)
    return tuple(outs)
```

```python
import functools

import jax
import jax.numpy as jnp
import numpy as np
from jax import lax
from jax.experimental import pallas as pl
from jax.experimental.pallas import tpu as pltpu

F32 = jnp.float32
BF16 = jnp.bfloat16

D_MODEL = 1024
BRANCH_W = 256
N_BRANCH = 4
EPS = 1e-6
NEG_INF = -1e30

HEAD_DIM = 64
ATTN_DILATIONS = (1, 4, 16)
N_GROUPS = 3
HEADS_PER_GROUP = 4
N_ATTN_HEADS = N_GROUPS * HEADS_PER_GROUP
ATTN_QKV = N_ATTN_HEADS * HEAD_DIM
HALF_KEYS = 64
Q_SUB = 128
K_WIN = Q_SUB + 2 * HALF_KEYS

POOL_HALF = (1, 2, 4, 8)
POOL_HALO = 8
CHUNK = 128
SG_GROUPS = 4

RWKV_HEADS = 4
RWKV_N = 64
RWKV_CHUNK = 64
RWKV_UNIT = 128
LAT_W = 128
GN_EPS = 64e-5
NEUMANN_DOUBLINGS = 5

QKV_W = 3 * ATTN_QKV
RKV_W = 3 * BRANCH_W
BC_W = 3 * BRANCH_W
Z_W = 4 * BRANCH_W
PROJ_SMALL_W = QKV_W + RKV_W + 2 * LAT_W + BC_W + Z_W

PROJ_ROWS = 256
MERGE_ROWS = 256
PERM_ROWS = 256
ATTN_ROWS = 512
RWKV_ROWS = 512
VMEM_LIMIT = 56 * 1024 * 1024


def _params(*sem):
    return pltpu.CompilerParams(dimension_semantics=sem, vmem_limit_bytes=VMEM_LIMIT)


def _dot(a, b):
    return jnp.dot(a, b, preferred_element_type=F32)


def _split_bf16(x):
    hi = x.astype(BF16)
    lo = (x - hi.astype(F32)).astype(BF16)
    return hi, lo


def _dot_split_lhs(x, m_bf16):
    hi, lo = _split_bf16(x)
    return _dot(hi, m_bf16) + _dot(lo, m_bf16)


def _dot_split_lhs_rhs(m_bf16, x):
    hi, lo = _split_bf16(x)
    return _dot(m_bf16, hi) + _dot(m_bf16, lo)


def _dot_split_both(a, b):
    a_hi, a_lo = _split_bf16(a)
    b_hi, b_lo = _split_bf16(b)
    return _dot(a_hi, b_hi) + _dot(a_lo, b_hi) + _dot(a_hi, b_lo)


def _lane_head(rows):
    return lax.broadcasted_iota(jnp.int32, (rows, BRANCH_W), 1) // HEAD_DIM


def _proj_kernel(x_ref, g_ref, w_ref, qg_ref, kg_ref, bd_ref, perm_ref, qkv_ref, rkv_ref, lat_ref, bc_ref,
                 z_ref):
    x = x_ref[...]
    ms = jnp.mean(x * x, axis=-1, keepdims=True)
    h = (x * lax.rsqrt(ms + EPS) * g_ref[...]).astype(BF16)
    bd = bd_ref[...]
    n_chunks = 3 * N_GROUPS

    def project(c):
        return _dot(h, w_ref[:, c * BRANCH_W:(c + 1) * BRANCH_W])

    def normalize(c, y):
        if c < 2 * N_GROUPS:
            msq = _dot_split_lhs(y * y, bd)
            if c < N_GROUPS:
                y = y * lax.rsqrt(msq + EPS) * qg_ref[...] * (HEAD_DIM ** -0.5)
            else:
                y = y * lax.rsqrt(msq + EPS) * kg_ref[...]
        return y.astype(BF16)

    def store(c, yb):
        group = c % N_GROUPS
        if group > 0:
            yb = _dot(perm_ref[group - 1], yb).astype(BF16)
        qkv_ref[:, c * BRANCH_W:(c + 1) * BRANCH_W] = yb

    others = []
    o = QKV_W
    for ref, width in ((rkv_ref, RKV_W), (lat_ref, 2 * LAT_W), (bc_ref, BC_W), (z_ref, Z_W)):
        others.append((ref, o, width))
        o += width

    projected, normalized = {}, {}
    for step in range(n_chunks + 2):
        if step < n_chunks:
            projected[step] = project(step)
        elif others:
            ref, lo, width = others.pop(0)
            ref[...] = _dot(h, w_ref[:, lo:lo + width])
        if 1 <= step <= n_chunks:
            normalized[step - 1] = normalize(step - 1, projected.pop(step - 1))
        if step >= 2:
            store(step - 2, normalized.pop(step - 2))
    for ref, lo, width in others:
        ref[...] = _dot(h, w_ref[:, lo:lo + width])


def _residue_perms(transpose):
    mats = []
    for dil in ATTN_DILATIONS[1:]:
        per = PERM_ROWS // dil
        p = np.zeros((PERM_ROWS, PERM_ROWS), np.float32)
        old = np.arange(PERM_ROWS)
        p[(old % dil) * per + old // dil, old] = 1.0
        mats.append(p.T if transpose else p)
    return jnp.asarray(np.stack(mats), BF16)


def _proj(x2d, lp):
    rows = x2d.shape[0]
    tm = PROJ_ROWS
    assert tm == PERM_ROWS
    const = lambda i: (0, 0)
    row = lambda i: (i, 0)
    widths = (QKV_W, RKV_W, 2 * LAT_W, BC_W, Z_W)
    dtypes = (BF16, F32, F32, F32, F32)
    return pl.pallas_call(
        _proj_kernel,
        grid=(rows // tm,),
        in_specs=[
            pl.BlockSpec((tm, D_MODEL), row),
            pl.BlockSpec((1, D_MODEL), const),
            pl.BlockSpec((D_MODEL, PROJ_SMALL_W), const),
            pl.BlockSpec((1, BRANCH_W), const),
            pl.BlockSpec((1, BRANCH_W), const),
            pl.BlockSpec((BRANCH_W, BRANCH_W), const),
            pl.BlockSpec((N_GROUPS - 1, tm, tm), lambda i: (0, 0, 0)),
        ],
        out_specs=[pl.BlockSpec((tm, w), row) for w in widths],
        out_shape=[jax.ShapeDtypeStruct((rows, w), dt) for w, dt in zip(widths, dtypes)],
        compiler_params=_params("parallel"),
        name="proj",
    )(x2d, lp["norm_g"], lp["w_small"], lp["q_gain"], lp["k_gain"], lp["bd_mean"], _residue_perms(False))


def _attn_kernel(q_ref, kp_ref, k_ref, kn_ref, vp_ref, v_ref, vn_ref, bias_ref, o_ref, lse_ref,
                 qbuf, kbuf, vbuf, *, tq, seq_m):
    i = pl.program_id(2)

    def gather(dst, at, src):
        per = src.shape[2]
        for j in range(src.shape[1]):
            dst[at + j * per:at + (j + 1) * per, :] = src[0, j]

    gather(qbuf, 0, q_ref)
    gather(kbuf, 0, kp_ref)
    gather(kbuf, HALF_KEYS, k_ref)
    gather(kbuf, HALF_KEYS + tq, kn_ref)
    gather(vbuf, 0, vp_ref)
    gather(vbuf, HALF_KEYS, v_ref)
    gather(vbuf, HALF_KEYS + tq, vn_ref)
    per_out = o_ref.shape[2]
    lane_h = _lane_head(Q_SUB)
    col = lax.broadcasted_iota(jnp.int32, (1, K_WIN), 1)
    bias = bias_ref[...]
    for sb in range(tq // Q_SUB):
        r0 = sb * Q_SUB
        qb = qbuf[r0:r0 + Q_SUB, :]
        qs = jnp.concatenate(
            [jnp.where(lane_h == h, qb, jnp.zeros_like(qb)) for h in range(HEADS_PER_GROUP)], axis=0)
        s = lax.dot_general(qs, kbuf[r0:r0 + K_WIN, :], (((1,), (1,)), ((), ())),
                            preferred_element_type=F32)
        kpos = i * tq + (r0 - HALF_KEYS) + col
        s = s + bias + jnp.where((kpos >= 0) & (kpos < seq_m), 0.0, NEG_INF)
        m = jnp.max(s, axis=-1, keepdims=True)
        p = jnp.exp(s - m)
        l = jnp.sum(p, axis=-1, keepdims=True)
        pv = _dot(p.astype(BF16), vbuf[r0:r0 + K_WIN, :])
        on = pv / l
        ls = m + jnp.log(l)
        o = jnp.zeros((Q_SUB, BRANCH_W), F32)
        lo = jnp.zeros((Q_SUB, BRANCH_W), F32)
        for h in range(HEADS_PER_GROUP):
            sel = lane_h == h
            o = jnp.where(sel, on[h * Q_SUB:(h + 1) * Q_SUB, :], o)
            lo = jnp.where(sel, ls[h * Q_SUB:(h + 1) * Q_SUB, :], lo)
        if per_out >= Q_SUB:
            o_ref[0, r0 // per_out, r0 % per_out:r0 % per_out + Q_SUB, :] = o.astype(o_ref.dtype)
            lse_ref[0, r0 // per_out, r0 % per_out:r0 % per_out + Q_SUB, :] = lo
        for j in range(Q_SUB // per_out):
            o_ref[0, r0 // per_out + j] = o[j * per_out:(j + 1) * per_out, :].astype(o_ref.dtype)
            lse_ref[0, r0 // per_out + j] = lo[j * per_out:(j + 1) * per_out, :]


def _attn_bias(group):
    dil = ATTN_DILATIONS[group]
    i = np.arange(Q_SUB)[:, None]
    j = np.arange(K_WIN)[None, :]
    off = np.abs(j - HALF_KEYS - i).astype(np.float32)
    slopes = 2.0 ** (-8.0 * np.arange(1, N_ATTN_HEADS + 1, dtype=np.float32) / N_ATTN_HEADS)
    blocks = []
    for h in range(HEADS_PER_GROUP):
        slope = slopes[group * HEADS_PER_GROUP + h]
        blocks.append(np.where(off <= HALF_KEYS, -slope * (off * dil), NEG_INF))
    return jnp.asarray(np.concatenate(blocks, axis=0), F32)


def _attn_group(qkv, group, batch, seq):
    dil = ATTN_DILATIONS[group]
    seq_m = seq // dil
    tq = min(ATTN_ROWS, seq_m)
    nq = seq_m // tq
    per = PERM_ROWS // dil
    halo_per = min(per, HALF_KEYS)
    halo_split = per // halo_per
    last_halo = seq_m // HALF_KEYS - 1
    view = qkv.reshape(batch, seq // PERM_ROWS, dil, per, QKV_W)

    def col(which):
        return which * N_GROUPS + group

    def main(which):
        return pl.BlockSpec((1, tq // per, None, per, BRANCH_W),
                            lambda b, r, i: (b, i, r, 0, col(which)))

    def halo(which, index):
        if halo_split == 1:
            return pl.BlockSpec((1, HALF_KEYS // per, None, per, BRANCH_W),
                                lambda b, r, i: (b, index(i), r, 0, col(which)))
        return pl.BlockSpec((1, 1, None, HALF_KEYS, BRANCH_W),
                            lambda b, r, i: (b, index(i) // halo_split, r, index(i) % halo_split, col(which)))

    hb = tq // HALF_KEYS
    prev = lambda i: jnp.maximum(i * hb - 1, 0)
    nxt = lambda i: jnp.minimum((i + 1) * hb, last_halo)
    out_spec = pl.BlockSpec((1, tq // per, None, per, BRANCH_W), lambda b, r, i: (b, i, r, 0, 0))
    out_dims = (batch, seq // PERM_ROWS, dil, per, BRANCH_W)
    buf = lambda rows: pltpu.VMEM((rows, BRANCH_W), BF16)
    return pl.pallas_call(
        functools.partial(_attn_kernel, tq=tq, seq_m=seq_m),
        grid=(batch, dil, nq),
        in_specs=[main(0), halo(1, prev), main(1), halo(1, nxt), halo(2, prev), main(2), halo(2, nxt),
                  pl.BlockSpec((HEADS_PER_GROUP * Q_SUB, K_WIN), lambda b, r, i: (0, 0))],
        out_specs=[out_spec, out_spec],
        out_shape=[jax.ShapeDtypeStruct(out_dims, BF16), jax.ShapeDtypeStruct(out_dims, F32)],
        scratch_shapes=[buf(tq), buf(tq + 2 * HALF_KEYS), buf(tq + 2 * HALF_KEYS)],
        compiler_params=_params("parallel", "parallel", "parallel"),
        name=f"attn_g{group}",
    )(view, view, view, view, view, view, view, _attn_bias(group))


def _softplus(y):
    return jnp.maximum(y, 0.0) + jnp.log(1.0 + jnp.exp(-jnp.abs(y)))


def _stack_heads(xb, lane_h):
    return jnp.concatenate(
        [jnp.where(lane_h == h, xb, jnp.zeros_like(xb)) for h in range(RWKV_HEADS)], axis=0)


def _nt(a, b):
    return lax.dot_general(a, b, (((1,), (1,)), ((), ())), preferred_element_type=F32)


def _rwkv_kernel(rkvf_ref, rkvfh_ref, rkvb_ref, rkvbh_ref, latf_ref, latfh_ref, latb_ref, latbh_ref,
                 mur_ref, mul_ref, w0_ref, wup_ref, a0_ref, aup_ref, kk_ref, ka_ref, rk_ref,
                 lng_ref, lnb_ref, bdm_ref, bd1_ref, outf_ref, outb_ref,
                 st_ref, r_s, k_s, v_s, kk_s, b_s, lw_s, o_s, y_s, *, batch, tc):
    i = pl.program_id(0)
    c = RWKV_CHUNK
    unit = RWKV_UNIT
    n_units = tc // unit
    streams = [(d, b) for d in range(2) for b in range(batch)]
    ids = range(len(streams))
    back = [d == 1 for d, _ in streams]
    items = [(s, h) for s in ids for h in (reversed(range(unit // c)) if back[s] else range(unit // c))]
    wid = range(len(items))
    iback = [back[s] for s, _ in items]

    @pl.when(i == 0)
    def _():
        st_ref[...] = jnp.zeros_like(st_ref)

    lane_h = _lane_head(c)
    t_i = lax.broadcasted_iota(jnp.int32, (c, c), 0)
    s_i = lax.broadcasted_iota(jnp.int32, (c, c), 1)
    tau = lax.broadcasted_iota(jnp.int32, (c, BRANCH_W), 0)
    sig = lax.broadcasted_iota(jnp.int32, (c, BRANCH_W), 1) % c
    eye = sig == tau
    eye_b = jnp.where(eye, 1.0, 0.0).astype(BF16)
    tri_f = jnp.where(s_i <= t_i, 1.0, 0.0).astype(BF16)
    tri_b = jnp.where(s_i >= t_i, 1.0, 0.0).astype(BF16)
    strict = [(sig > tau) if bk else (sig < tau) for bk in iback]
    incl = [(sig >= tau) if bk else (sig <= tau) for bk in iback]
    bd1 = bd1_ref[...]
    bdm = bdm_ref[...]
    sm = lambda xb: _stack_heads(xb, lane_h)
    bf = lambda x: x.astype(BF16)
    top = lambda x: x[0:c]
    mid = lambda x: x[c:2 * c]
    rows_of = lambda x, k: x[k * unit:(k + 1) * unit]
    stack = lambda xs: jnp.concatenate(xs, axis=0)

    x_refs = [(rkvb_ref, rkvbh_ref, latb_ref, latbh_ref) if bk else (rkvf_ref, rkvfh_ref, latf_ref, latfh_ref)
              for bk in back]

    def unit_rows(j):
        start = [pl.multiple_of(((n_units - 1 - j) if bk else j) * unit, unit) for bk in back]
        return start, [pl.ds(st, unit) for st in start]

    def prepare(j, slot):
        start, rows = unit_rows(j)

        def mixed(s, x_ref, halo_ref, mu):
            d, b = streams[s]
            x = x_ref[b, rows[s], :]
            row = lax.broadcasted_iota(jnp.int32, x.shape, 0)
            if back[s]:
                near = x_ref[b, pl.ds(pl.multiple_of(jnp.minimum(start[s] + unit, tc - 8), 8), 8), :][0:1, :]
                edge = jnp.where(j == 0, jnp.where(i == 0, 0.0, halo_ref[b][0:1, :]), near)
                sh = jnp.where(row == unit - 1, edge, pltpu.roll(x, unit - 1, 0))
            else:
                near = x_ref[b, pl.ds(pl.multiple_of(jnp.maximum(start[s] - 8, 0), 8), 8), :][7:8, :]
                edge = jnp.where(j == 0, jnp.where(i == 0, 0.0, halo_ref[b][7:8, :]), near)
                sh = jnp.where(row == 0, edge, pltpu.roll(x, 1, 0))
            return x + (sh - x) * mu[d]

        xr = [mixed(s, x_refs[s][0], x_refs[s][1], mur_ref) for s in ids]
        xl = [mixed(s, x_refs[s][2], x_refs[s][3], mul_ref) for s in ids]
        th = [jnp.tanh(stack([xl[s] for s in ids if streams[s][0] == d])) for d in range(2)]
        yield
        w_log, a = [], []
        for d in range(2):
            xl_d = stack([xl[s] for s in ids if streams[s][0] == d])
            wl = -_softplus(-(w0_ref[d] + _dot_split_both(th[d], wup_ref[d]))) - 0.5
            al = jax.nn.sigmoid(a0_ref[d] + _dot_split_both(xl_d, aup_ref[d]))
            w_log += [rows_of(wl, k) for k in range(batch)]
            a += [rows_of(al, k) for k in range(batch)]
        k = [xr[s][:, BRANCH_W:2 * BRANCH_W] for s in ids]
        kk = [k[s] * kk_ref[streams[s][0]] for s in ids]
        yield
        kk_sq = _dot_split_lhs(stack([kk[s] * kk[s] for s in ids]), bd1)
        yield
        for s in ids:
            kks = kk[s] * lax.rsqrt(rows_of(kk_sq, s) + 1e-12)
            r_s[slot, s] = xr[s][:, 0:BRANCH_W]
            k_s[slot, s] = k[s] * (1.0 + (a[s] - 1.0) * ka_ref[streams[s][0]])
            v_s[slot, s] = xr[s][:, 2 * BRANCH_W:3 * BRANCH_W]
            kk_s[slot, s] = kks
            b_s[slot, s] = kks * a[s]
            lw_s[slot, s] = -jnp.exp(w_log[s])

    def advance(slot):
        ids = wid
        back = iback
        part = lambda ref: [ref[slot, s, h * c:(h + 1) * c, :] for s, h in items]
        lw, r, k2, v, kk, bq = part(lw_s), part(r_s), part(k_s), part(v_s), part(kk_s), part(b_s)

        def cumsum(s):
            lw_hi = bf(lw[s])
            lw_r = lw[s] - lw_hi.astype(F32)
            lw_mid = bf(lw_r)
            lw_lo = bf(lw_r - lw_mid.astype(F32))
            tri = tri_b if back[s] else tri_f
            return _dot(tri, lw_hi) + _dot(tri, lw_mid) + _dot(tri, lw_lo)

        cumi = [cumsum(s) for s in ids]
        yield
        tot = [cumi[s][0:1, :] if back[s] else cumi[s][c - 1:c, :] for s in ids]
        e_m = [jnp.exp(-cumi[s]) for s in ids]
        e_t = [jnp.exp(tot[s] - cumi[s]) for s in ids]
        pt = [bf(-kk[s] * jnp.exp(cumi[s] - lw[s])) for s in ids]
        rt = [r[s] * jnp.exp(cumi[s]) for s in ids]
        pr = [stack([pt[s], bf(rt[s])]) for s in ids]
        ab = [_nt(pr[s], sm(bf(bq[s] * e_m[s]))) for s in ids]
        ak = [_nt(pr[s], sm(bf(k2[s] * e_m[s]))) for s in ids]
        yield
        a_pb = [jnp.where(strict[s], top(ab[s]), 0.0) for s in ids]
        a_rb = [bf(jnp.where(incl[s], mid(ab[s]), 0.0)) for s in ids]
        a_pk = [bf(jnp.where(strict[s], top(ak[s]), 0.0)) for s in ids]
        a_rk = [bf(jnp.where(incl[s], mid(ak[s]), 0.0)) for s in ids]
        l_b = [bf(_nt(eye_b, sm(bf(bq[s] * e_t[s])))) for s in ids]
        l_k = [bf(_nt(eye_b, sm(bf(k2[s] * e_t[s])))) for s in ids]
        yield

        tinv = [jnp.where(eye, 1.0, a_pb[s]) for s in ids]
        apow = [bf(a_pb[s]) for s in ids]
        a2 = [_dot(apow[s], sm(apow[s])) for s in ids]
        av = [_dot(stack([a_pk[s], a_rk[s], l_k[s]]), sm(bf(v[s]))) for s in ids]
        yield
        for _ in range(NEUMANN_DOUBLINGS - 1):
            apow = [bf(a2[s]) for s in ids]
            both = [_dot(stack([bf(tinv[s]), apow[s]]), sm(apow[s])) for s in ids]
            yield
            tinv = [tinv[s] + top(both[s]) for s in ids]
            a2 = [mid(both[s]) for s in ids]
        tinv = [tinv[s] + _dot(bf(tinv[s]), sm(bf(a2[s]))) for s in ids]
        yield
        tb = [bf(t) for t in tinv]

        ah = [_dot(tb[s], sm(pt[s])) for s in ids]
        uh = [_dot(tb[s], sm(bf(top(av[s])))) for s in ids]
        yield
        lhs = [stack([a_rb[s], l_b[s]]) for s in ids]
        o_a = [_dot(lhs[s], sm(bf(ah[s]))) for s in ids]
        o_u = [_dot(lhs[s], sm(bf(uh[s]))) for s in ids]
        yield
        rh = [rt[s] + top(o_a[s]) for s in ids]
        mt = [jnp.where(eye, jnp.exp(tot[s]), 0.0) + mid(o_a[s]) for s in ids]
        oh = [mid(av[s]) + top(o_u[s]) for s in ids]
        gt = [mid(o_u[s]) + av[s][2 * c:3 * c] for s in ids]

        bonus = _dot(bf(stack([r[w] * k2[w] * rk_ref[streams[items[w][0]][0]] for w in ids])), bd1) * stack(v)
        for w, (s, h) in enumerate(items):
            y_s[slot, s, h * c:(h + 1) * c, :] = bonus[w * c:(w + 1) * c]
        state = [st_ref[s] for s in range(len(streams))]
        for step in range(unit // c):
            now = [w for w in ids if w % (unit // c) == step]
            out = {w: _dot(stack([bf(rh[w]), bf(mt[w])]), sm(bf(state[items[w][0]]))) for w in now}
            yield
            for w in now:
                s, h = items[w]
                state[s] = mid(out[w]) + gt[w]
                o_s[slot, s, h * c:(h + 1) * c, :] = top(out[w]) + oh[w]
        for s in range(len(streams)):
            st_ref[s] = state[s]

    def finish(j, slot):
        _, rows = unit_rows(j)
        o = stack([o_s[slot, s] for s in ids])
        mean = _dot_split_lhs(o, bdm)
        yield
        xc = o - mean
        var = _dot(bf(xc * xc), bdm)
        yield
        y = xc * lax.rsqrt(var + GN_EPS) * lng_ref[...] + lnb_ref[...] + stack([y_s[slot, s] for s in ids])
        for s, (d, b) in enumerate(streams):
            out_ref = outb_ref if d == 1 else outf_ref
            out_ref[b, rows[s], :] = rows_of(y, s)

    def interleave(*phases):
        live = list(phases)
        while live:
            for phase in list(live):
                if next(phase, "done") == "done":
                    live.remove(phase)

    o_s[1] = jnp.zeros(o_s.shape[1:], F32)
    y_s[1] = jnp.zeros(y_s.shape[1:], F32)
    interleave(prepare(0, 0))

    def pipeline(j, carry):
        interleave(finish(jnp.maximum(j - 1, 0), (j + 1) % 2),
                   advance(j % 2),
                   prepare(jnp.minimum(j + 1, n_units - 1), (j + 1) % 2))
        return carry

    lax.fori_loop(0, n_units, pipeline, 0)
    interleave(finish(n_units - 1, (n_units - 1) % 2))


def _rwkv(rkv, lat, lp, batch, seq):
    tc = min(RWKV_ROWS, seq)
    nt = seq // tc
    hb = tc // 8
    last_halo = seq // 8 - 1
    fwd = lambda i: i
    bwd = lambda i: nt - 1 - i
    fwd_halo = lambda i: jnp.maximum(i * hb - 1, 0)
    bwd_halo = lambda i: jnp.minimum((bwd(i) + 1) * hb, last_halo)
    full3 = lambda shape: pl.BlockSpec(shape, lambda i: (0, 0, 0))
    full2 = lambda shape: pl.BlockSpec(shape, lambda i: (0, 0))
    n_streams = 2 * batch
    out_shape = jax.ShapeDtypeStruct((batch, seq, BRANCH_W), F32)
    return pl.pallas_call(
        functools.partial(_rwkv_kernel, batch=batch, tc=tc),
        grid=(nt,),
        in_specs=[
            pl.BlockSpec((batch, tc, RKV_W), lambda i: (0, fwd(i), 0)),
            pl.BlockSpec((batch, 8, RKV_W), lambda i: (0, fwd_halo(i), 0)),
            pl.BlockSpec((batch, tc, RKV_W), lambda i: (0, bwd(i), 0)),
            pl.BlockSpec((batch, 8, RKV_W), lambda i: (0, bwd_halo(i), 0)),
            pl.BlockSpec((batch, tc, LAT_W), lambda i: (0, fwd(i), 0)),
            pl.BlockSpec((batch, 8, LAT_W), lambda i: (0, fwd_halo(i), 0)),
            pl.BlockSpec((batch, tc, LAT_W), lambda i: (0, bwd(i), 1)),
            pl.BlockSpec((batch, 8, LAT_W), lambda i: (0, bwd_halo(i), 1)),
            full3((2, 1, RKV_W)), full3((2, 1, LAT_W)), full3((2, 1, BRANCH_W)),
            full3((2, LAT_W, BRANCH_W)), full3((2, 1, BRANCH_W)), full3((2, LAT_W, BRANCH_W)),
            full3((2, 1, BRANCH_W)), full3((2, 1, BRANCH_W)), full3((2, 1, BRANCH_W)),
            full2((1, BRANCH_W)), full2((1, BRANCH_W)),
            full2((BRANCH_W, BRANCH_W)), full2((BRANCH_W, BRANCH_W)),
        ],
        out_specs=[pl.BlockSpec((batch, tc, BRANCH_W), lambda i: (0, fwd(i), 0)),
                   pl.BlockSpec((batch, tc, BRANCH_W), lambda i: (0, bwd(i), 0))],
        out_shape=[out_shape, out_shape],
        scratch_shapes=[pltpu.VMEM((n_streams, RWKV_CHUNK, BRANCH_W), F32)]
        + [pltpu.VMEM((2, n_streams, RWKV_UNIT, BRANCH_W), F32)] * 8,
        compiler_params=_params("arbitrary"),
        name="rwkv",
    )(rkv, rkv, rkv, rkv, lat, lat, lat, lat, lp["mu_rkv"], lp["mu_lat"], lp["w0"], lp["w_up_pad"],
      lp["a0"], lp["a_up_pad"], lp["k_k"], lp["k_a"], lp["r_k"], lp["ln_g"], lp["ln_b"],
      lp["bd_mean"], lp["bd_ones"])


def _silu(z):
    return z * jax.nn.sigmoid(z)


def _merge_kernel(x_ref, g_ref, o0_ref, l0_ref, o1_ref, l1_ref, o2_ref, l2_ref, z_ref,
                  ubp_ref, bc_ref, ubn_ref, yd0_ref, yd1_ref,
                  unperm_ref, poolw_ref, pools_ref, sgg_ref, sgw_ref, sgb_ref, wg_ref, wb_ref, wo_ref,
                  out_ref, *, tm, seq):
    i = pl.program_id(1)
    n_tiles = seq // tm
    x = x_ref[0]
    ms = jnp.mean(x * x, axis=-1, keepdims=True)
    h = (x * lax.rsqrt(ms + EPS) * g_ref[...]).astype(BF16)
    z = z_ref[0]
    lane_g = _lane_head(tm)

    def natural(ref, group, exact):
        if group == 0:
            return ref[0, 0, 0].astype(F32)
        v = jnp.concatenate([ref[0, 0, r] for r in range(ATTN_DILATIONS[group])], axis=0)
        unperm = unperm_ref[group - 1]
        return _dot_split_lhs_rhs(unperm, v) if exact else _dot(unperm, v)

    l0, l1, l2 = natural(l0_ref, 0, True), natural(l1_ref, 1, True), natural(l2_ref, 2, True)
    mx = jnp.maximum(jnp.maximum(l0, l1), l2)
    e0, e1, e2 = jnp.exp(l0 - mx), jnp.exp(l1 - mx), jnp.exp(l2 - mx)
    y_a = (e0 * natural(o0_ref, 0, False) + e1 * natural(o1_ref, 1, False)
           + e2 * natural(o2_ref, 2, False)) / (e0 + e1 + e2)

    bc = bc_ref[0]
    u = bc[:, 0:BRANCH_W]
    prev = jnp.where(i == 0, 0.0, ubp_ref[0][:, 0:BRANCH_W])
    nxt = jnp.where(i == n_tiles - 1, 0.0, ubn_ref[0][:, 0:BRANCH_W])
    ext = jnp.concatenate([prev, u, nxt], axis=0)
    n_ext = tm + 2 * POOL_HALO
    f1 = ext + pltpu.roll(ext, 1, 0)
    f2 = pltpu.roll(f1, 1, 0) + pltpu.roll(f1, n_ext - 1, 0)
    f4 = pltpu.roll(f2, 2, 0) + pltpu.roll(f2, n_ext - 2, 0)
    f8 = pltpu.roll(f4, 4, 0) + pltpu.roll(f4, n_ext - 4, 0)
    lane_e = _lane_head(n_ext)
    win = jnp.where(lane_e == 0, f1, jnp.where(lane_e == 1, f2, jnp.where(lane_e == 2, f4, f8)))
    win = win[POOL_HALO:POOL_HALO + tm, :]
    pos = i * tm + lax.broadcasted_iota(jnp.int32, (tm, BRANCH_W), 0)
    half = jnp.where(lane_g == 0, POOL_HALF[0],
                     jnp.where(lane_g == 1, POOL_HALF[1], jnp.where(lane_g == 2, POOL_HALF[2], POOL_HALF[3])))
    cnt = (jnp.minimum(pos + half, seq) - jnp.maximum(pos - half, 0)).astype(F32)
    y_b = _dot((win / cnt - u).astype(BF16), poolw_ref[...]) * pools_ref[...]

    u_c = bc[:, BRANCH_W:2 * BRANCH_W]
    v_c = bc[:, 2 * BRANCH_W:3 * BRANCH_W]
    msv = jnp.mean(v_c * v_c, axis=-1, keepdims=True)
    v_n = (v_c * lax.rsqrt(msv + EPS) * sgg_ref[...]).astype(BF16)
    lane_c = _lane_head(CHUNK)
    sv_chunks = []
    for ch in range(tm // CHUNK):
        vch = v_n[ch * CHUNK:(ch + 1) * CHUNK, :]
        sv = jnp.zeros((CHUNK, BRANCH_W), F32)
        for g in range(SG_GROUPS):
            sv = jnp.where(lane_c == g, _dot(sgw_ref[g], vch), sv)
        sv_chunks.append(sv + sgb_ref[...])
    y_c = u_c * jnp.concatenate(sv_chunks, axis=0)

    y_d = yd0_ref[0] + yd1_ref[0]

    merged = jnp.zeros((tm, D_MODEL), F32)
    for n, y in enumerate((y_a, y_b, y_c, y_d)):
        ys = (y * _silu(z[:, n * BRANCH_W:(n + 1) * BRANCH_W])).astype(BF16)
        gate = jax.nn.sigmoid(_dot(h, wg_ref[:, n * D_MODEL:(n + 1) * D_MODEL]))
        merged = merged + gate * _dot(ys, wb_ref[n])
    out_ref[0] = x + _dot(merged.astype(BF16), wo_ref[...])


def _merge(x, attn, z, bc, yd0, yd1, lp, batch, seq):
    tm = MERGE_ROWS
    hb = tm // POOL_HALO
    last_halo = seq // POOL_HALO - 1
    row = lambda w: pl.BlockSpec((1, tm, w), lambda b, i: (b, i, 0))
    const2 = lambda shape: pl.BlockSpec(shape, lambda b, i: (0, 0))
    const3 = lambda shape: pl.BlockSpec(shape, lambda b, i: (0, 0, 0))
    assert tm == PERM_ROWS
    in_specs = [row(D_MODEL), const2((1, D_MODEL))]
    for dil in ATTN_DILATIONS:
        slab = pl.BlockSpec((1, 1, dil, tm // dil, BRANCH_W), lambda b, i: (b, i, 0, 0, 0))
        in_specs += [slab, slab]
    in_specs += [
        row(Z_W),
        pl.BlockSpec((1, POOL_HALO, BRANCH_W), lambda b, i: (b, jnp.maximum(i * hb - 1, 0), 0)),
        row(BC_W),
        pl.BlockSpec((1, POOL_HALO, BRANCH_W), lambda b, i: (b, jnp.minimum((i + 1) * hb, last_halo), 0)),
        row(BRANCH_W), row(BRANCH_W),
        const3((N_GROUPS - 1, tm, tm)),
        const2((BRANCH_W, BRANCH_W)), const2((1, BRANCH_W)), const2((1, BRANCH_W)),
        const3((SG_GROUPS, CHUNK, CHUNK)), const2((CHUNK, BRANCH_W)),
        const2((D_MODEL, N_BRANCH * D_MODEL)), const3((N_BRANCH, BRANCH_W, D_MODEL)),
        const2((D_MODEL, D_MODEL)),
    ]
    flat_attn = [a for pair in attn for a in pair]
    return pl.pallas_call(
        functools.partial(_merge_kernel, tm=tm, seq=seq),
        grid=(batch, seq // tm),
        in_specs=in_specs,
        out_specs=row(D_MODEL),
        out_shape=jax.ShapeDtypeStruct((batch, seq, D_MODEL), F32),
        compiler_params=_params("parallel", "parallel"),
        name="merge",
    )(x, lp["norm_g"], *flat_attn, z, bc, bc, bc, yd0, yd1, _residue_perms(True),
      lp["pool_w_bd"], lp["pool_scale"], lp["sg_norm_g"], lp["sg_w"], lp["sg_bias"],
      lp["w_gate"], lp["w_branch"], lp["w_out"])


def _block_diag(blocks):
    n = blocks.shape[0]
    w = blocks.shape[1]
    out = jnp.zeros((n * w, n * w), blocks.dtype)
    for g in range(n):
        out = out.at[g * w:(g + 1) * w, g * w:(g + 1) * w].set(blocks[g])
    return out


def _layer_params(l, norm_g, w_in, q_norm_g, k_norm_g, pool_w, pool_scale, sg_norm_g, sg_w, sg_b,
                  mu_rkv, mu_lat, w0, w_up, a0, a_up, k_k, k_a, r_k, ln_g, ln_b, w_branch, w_out):
    w = w_in[l]
    seg = lambda lo, width: w[:, lo:lo + width]
    w_small = jnp.concatenate([
        seg(0, QKV_W), seg(3840, RKV_W), seg(4608, 2 * LAT_W), seg(2560, BRANCH_W), seg(3072, 2 * BRANCH_W),
        seg(2304, BRANCH_W), seg(2816, BRANCH_W), seg(3584, BRANCH_W), seg(4864, BRANCH_W)], axis=1)
    ones_blocks = jnp.ones((RWKV_HEADS, HEAD_DIM, HEAD_DIM), F32)
    zeros_lat = jnp.zeros((2, LAT_W // 2, BRANCH_W), F32)
    vec3 = lambda p: p[l][:, None, :]
    return {
        "norm_g": norm_g[l][None, :],
        "w_small": w_small.astype(BF16),
        "w_gate": w[:, PROJ_SMALL_W:].astype(BF16),
        "q_gain": jnp.tile(q_norm_g[l], HEADS_PER_GROUP)[None, :],
        "k_gain": jnp.tile(k_norm_g[l], HEADS_PER_GROUP)[None, :],
        "bd_mean": (_block_diag(ones_blocks) / HEAD_DIM).astype(BF16),
        "bd_ones": _block_diag(ones_blocks).astype(BF16),
        "pool_w_bd": _block_diag(pool_w[l]).astype(BF16),
        "pool_scale": pool_scale[l][None, :],
        "sg_norm_g": sg_norm_g[l][None, :],
        "sg_w": sg_w[l].astype(BF16),
        "sg_bias": jnp.repeat(sg_b[l].T, HEAD_DIM, axis=1),
        "mu_rkv": vec3(mu_rkv), "mu_lat": vec3(mu_lat), "w0": vec3(w0), "a0": vec3(a0),
        "k_k": vec3(k_k), "k_a": vec3(k_a), "r_k": vec3(r_k),
        "w_up_pad": jnp.concatenate([w_up[l], zeros_lat], axis=1),
        "a_up_pad": jnp.concatenate([zeros_lat, a_up[l]], axis=1),
        "ln_g": ln_g[l][None, :], "ln_b": ln_b[l][None, :],
        "w_branch": w_branch[l].astype(BF16),
        "w_out": w_out[l].astype(BF16),
    }


def _layer(x, lp):
    batch, seq, _ = x.shape
    qkv, rkv, lat, bc, z = _proj(x.reshape(batch * seq, D_MODEL), lp)
    attn = [_attn_group(qkv, g, batch, seq) for g in range(N_GROUPS)]
    rkv = rkv.reshape(batch, seq, RKV_W)
    lat = lat.reshape(batch, seq, 2 * LAT_W)
    yd0, yd1 = _rwkv(rkv, lat, lp, batch, seq)
    return _merge(x, attn, z.reshape(batch, seq, Z_W), bc.reshape(batch, seq, BC_W), yd0, yd1, lp, batch, seq)


def kernel(x_prompt, x_sample, norm_g, w_in, q_norm_g, k_norm_g, pool_w, pool_scale, sg_norm_g, sg_w, sg_b,
           mu_rkv, mu_lat, w0, w_up, a0, a_up, k_k, k_a, r_k, ln_g, ln_b, w_branch, w_out):
    weights = (norm_g, w_in, q_norm_g, k_norm_g, pool_w, pool_scale, sg_norm_g, sg_w, sg_b,
               mu_rkv, mu_lat, w0, w_up, a0, a_up, k_k, k_a, r_k, ln_g, ln_b, w_branch, w_out)
    layers = [_layer_params(l, *weights) for l in range(norm_g.shape[0])]
    outs = []
    for x in (x_prompt, x_sample):
        for lp in layers:
            x = _layer(x, lp)
        outs.append(x)
    return tuple(outs)
```

```python
import functools

import jax
import jax.numpy as jnp
import numpy as np
from jax import lax
from jax.experimental import pallas as pl
from jax.experimental.pallas import tpu as pltpu

F32 = jnp.float32
BF16 = jnp.bfloat16

D_MODEL = 1024
BRANCH_W = 256
N_BRANCH = 4
EPS = 1e-6
NEG_INF = -1e30

HEAD_DIM = 64
ATTN_DILATIONS = (1, 4, 16)
N_GROUPS = 3
HEADS_PER_GROUP = 4
N_ATTN_HEADS = N_GROUPS * HEADS_PER_GROUP
ATTN_QKV = N_ATTN_HEADS * HEAD_DIM
HALF_KEYS = 64
Q_SUB = 128
K_WIN = Q_SUB + 2 * HALF_KEYS

POOL_HALF = (1, 2, 4, 8)
POOL_HALO = 8
CHUNK = 128
SG_GROUPS = 4

RWKV_HEADS = 4
RWKV_N = 64
RWKV_CHUNK = 64
RWKV_UNIT = 128
LAT_W = 128
GN_EPS = 64e-5
NEUMANN_DOUBLINGS = 5

QKV_W = 3 * ATTN_QKV
RKV_W = 3 * BRANCH_W
BC_W = 3 * BRANCH_W
Z_W = 4 * BRANCH_W
PROJ_SMALL_W = QKV_W + RKV_W + 2 * LAT_W + BC_W + Z_W

PROJ_ROWS = 512
MERGE_ROWS = 512
PERM_ROWS = 256
ATTN_ROWS = 1024
RWKV_ROWS = 512
VMEM_LIMIT = 56 * 1024 * 1024


def _params(*sem):
    return pltpu.CompilerParams(dimension_semantics=sem, vmem_limit_bytes=VMEM_LIMIT)


def _dot(a, b):
    return jnp.dot(a, b, preferred_element_type=F32)


def _split_bf16(x):
    hi = x.astype(BF16)
    lo = (x - hi.astype(F32)).astype(BF16)
    return hi, lo


def _dot_split_lhs(x, m_bf16):
    hi, lo = _split_bf16(x)
    return _dot(hi, m_bf16) + _dot(lo, m_bf16)


def _dot_split_lhs_rhs(m_bf16, x):
    hi, lo = _split_bf16(x)
    return _dot(m_bf16, hi) + _dot(m_bf16, lo)


def _dot_split_both(a, b):
    a_hi, a_lo = _split_bf16(a)
    b_hi, b_lo = _split_bf16(b)
    return _dot(a_hi, b_hi) + _dot(a_lo, b_hi) + _dot(a_hi, b_lo)


def _lane_head(rows):
    return lax.broadcasted_iota(jnp.int32, (rows, BRANCH_W), 1) // HEAD_DIM


def _proj_kernel(x_ref, g_ref, w_ref, qg_ref, kg_ref, bd_ref, perm_ref, qkv_ref, rkv_ref, lat_ref, bc_ref,
                 z_ref):
    x = x_ref[...]
    ms = jnp.mean(x * x, axis=-1, keepdims=True)
    h = (x * lax.rsqrt(ms + EPS) * g_ref[...]).astype(BF16)
    bd = bd_ref[...]
    n_chunks = 3 * N_GROUPS

    def project(c):
        return _dot(h, w_ref[:, c * BRANCH_W:(c + 1) * BRANCH_W])

    def normalize(c, y):
        if c < 2 * N_GROUPS:
            msq = _dot_split_lhs(y * y, bd)
            if c < N_GROUPS:
                y = y * lax.rsqrt(msq + EPS) * qg_ref[...] * (HEAD_DIM ** -0.5)
            else:
                y = y * lax.rsqrt(msq + EPS) * kg_ref[...]
        return y.astype(BF16)

    def store(c, yb):
        group = c % N_GROUPS
        if group > 0:
            perm = perm_ref[group - 1]
            yb = jnp.concatenate([_dot(perm, yb[t * PERM_ROWS:(t + 1) * PERM_ROWS]).astype(BF16)
                                  for t in range(yb.shape[0] // PERM_ROWS)], axis=0)
        qkv_ref[:, c * BRANCH_W:(c + 1) * BRANCH_W] = yb

    others = []
    o = QKV_W
    for ref, width in ((rkv_ref, RKV_W), (lat_ref, 2 * LAT_W), (bc_ref, BC_W), (z_ref, Z_W)):
        others.append((ref, o, width))
        o += width

    projected, normalized = {}, {}
    for step in range(n_chunks + 2):
        if step < n_chunks:
            projected[step] = project(step)
        elif others:
            ref, lo, width = others.pop(0)
            ref[...] = _dot(h, w_ref[:, lo:lo + width])
        if 1 <= step <= n_chunks:
            normalized[step - 1] = normalize(step - 1, projected.pop(step - 1))
        if step >= 2:
            store(step - 2, normalized.pop(step - 2))
    for ref, lo, width in others:
        ref[...] = _dot(h, w_ref[:, lo:lo + width])


def _residue_perms(transpose):
    mats = []
    for dil in ATTN_DILATIONS[1:]:
        per = PERM_ROWS // dil
        p = np.zeros((PERM_ROWS, PERM_ROWS), np.float32)
        old = np.arange(PERM_ROWS)
        p[(old % dil) * per + old // dil, old] = 1.0
        mats.append(p.T if transpose else p)
    return jnp.asarray(np.stack(mats), BF16)


def _proj(x2d, lp):
    rows = x2d.shape[0]
    tm = PROJ_ROWS
    assert tm % PERM_ROWS == 0
    const = lambda i: (0, 0)
    row = lambda i: (i, 0)
    widths = (QKV_W, RKV_W, 2 * LAT_W, BC_W, Z_W)
    dtypes = (BF16, F32, F32, F32, F32)
    return pl.pallas_call(
        _proj_kernel,
        grid=(rows // tm,),
        in_specs=[
            pl.BlockSpec((tm, D_MODEL), row),
            pl.BlockSpec((1, D_MODEL), const),
            pl.BlockSpec((D_MODEL, PROJ_SMALL_W), const),
            pl.BlockSpec((1, BRANCH_W), const),
            pl.BlockSpec((1, BRANCH_W), const),
            pl.BlockSpec((BRANCH_W, BRANCH_W), const),
            pl.BlockSpec((N_GROUPS - 1, PERM_ROWS, PERM_ROWS), lambda i: (0, 0, 0)),
        ],
        out_specs=[pl.BlockSpec((tm, w), row) for w in widths],
        out_shape=[jax.ShapeDtypeStruct((rows, w), dt) for w, dt in zip(widths, dtypes)],
        compiler_params=_params("parallel"),
        name="proj",
    )(x2d, lp["norm_g"], lp["w_small"], lp["q_gain"], lp["k_gain"], lp["bd_mean"], _residue_perms(False))


def _attn_kernel(q_ref, kp_ref, k_ref, kn_ref, vp_ref, v_ref, vn_ref, bias_ref, o_ref, lse_ref,
                 qbuf, kbuf, vbuf, *, tq, seq_m):
    i = pl.program_id(2)

    def gather(dst, at, src):
        per = src.shape[2]
        for j in range(src.shape[1]):
            dst[at + j * per:at + (j + 1) * per, :] = src[0, j]

    gather(qbuf, 0, q_ref)
    gather(kbuf, 0, kp_ref)
    gather(kbuf, HALF_KEYS, k_ref)
    gather(kbuf, HALF_KEYS + tq, kn_ref)
    gather(vbuf, 0, vp_ref)
    gather(vbuf, HALF_KEYS, v_ref)
    gather(vbuf, HALF_KEYS + tq, vn_ref)
    per_out = o_ref.shape[2]
    lane_h = _lane_head(Q_SUB)
    col = lax.broadcasted_iota(jnp.int32, (1, K_WIN), 1)
    bias = bias_ref[...]
    for sb in range(tq // Q_SUB):
        r0 = sb * Q_SUB
        qb = qbuf[r0:r0 + Q_SUB, :]
        qs = jnp.concatenate(
            [jnp.where(lane_h == h, qb, jnp.zeros_like(qb)) for h in range(HEADS_PER_GROUP)], axis=0)
        s = lax.dot_general(qs, kbuf[r0:r0 + K_WIN, :], (((1,), (1,)), ((), ())),
                            preferred_element_type=F32)
        kpos = i * tq + (r0 - HALF_KEYS) + col
        s = s + bias + jnp.where((kpos >= 0) & (kpos < seq_m), 0.0, NEG_INF)
        m = jnp.max(s, axis=-1, keepdims=True)
        p = jnp.exp(s - m)
        l = jnp.sum(p, axis=-1, keepdims=True)
        pv = _dot(p.astype(BF16), vbuf[r0:r0 + K_WIN, :])
        on = pv / l
        ls = m + jnp.log(l)
        o = jnp.zeros((Q_SUB, BRANCH_W), F32)
        lo = jnp.zeros((Q_SUB, BRANCH_W), F32)
        for h in range(HEADS_PER_GROUP):
            sel = lane_h == h
            o = jnp.where(sel, on[h * Q_SUB:(h + 1) * Q_SUB, :], o)
            lo = jnp.where(sel, ls[h * Q_SUB:(h + 1) * Q_SUB, :], lo)
        if per_out >= Q_SUB:
            o_ref[0, r0 // per_out, r0 % per_out:r0 % per_out + Q_SUB, :] = o.astype(o_ref.dtype)
            lse_ref[0, r0 // per_out, r0 % per_out:r0 % per_out + Q_SUB, :] = lo
        for j in range(Q_SUB // per_out):
            o_ref[0, r0 // per_out + j] = o[j * per_out:(j + 1) * per_out, :].astype(o_ref.dtype)
            lse_ref[0, r0 // per_out + j] = lo[j * per_out:(j + 1) * per_out, :]


def _attn_bias(group):
    dil = ATTN_DILATIONS[group]
    i = np.arange(Q_SUB)[:, None]
    j = np.arange(K_WIN)[None, :]
    off = np.abs(j - HALF_KEYS - i).astype(np.float32)
    slopes = 2.0 ** (-8.0 * np.arange(1, N_ATTN_HEADS + 1, dtype=np.float32) / N_ATTN_HEADS)
    blocks = []
    for h in range(HEADS_PER_GROUP):
        slope = slopes[group * HEADS_PER_GROUP + h]
        blocks.append(np.where(off <= HALF_KEYS, -slope * (off * dil), NEG_INF))
    return jnp.asarray(np.concatenate(blocks, axis=0), F32)


def _attn_group(qkv, group, batch, seq):
    dil = ATTN_DILATIONS[group]
    seq_m = seq // dil
    tq = min(ATTN_ROWS, seq_m)
    nq = seq_m // tq
    per = PERM_ROWS // dil
    halo_per = min(per, HALF_KEYS)
    halo_split = per // halo_per
    last_halo = seq_m // HALF_KEYS - 1
    view = qkv.reshape(batch, seq // PERM_ROWS, dil, per, QKV_W)

    def col(which):
        return which * N_GROUPS + group

    def main(which):
        return pl.BlockSpec((1, tq // per, None, per, BRANCH_W),
                            lambda b, r, i: (b, i, r, 0, col(which)))

    def halo(which, index):
        if halo_split == 1:
            return pl.BlockSpec((1, HALF_KEYS // per, None, per, BRANCH_W),
                                lambda b, r, i: (b, index(i), r, 0, col(which)))
        return pl.BlockSpec((1, 1, None, HALF_KEYS, BRANCH_W),
                            lambda b, r, i: (b, index(i) // halo_split, r, index(i) % halo_split, col(which)))

    hb = tq // HALF_KEYS
    prev = lambda i: jnp.maximum(i * hb - 1, 0)
    nxt = lambda i: jnp.minimum((i + 1) * hb, last_halo)
    out_spec = pl.BlockSpec((1, tq // per, None, per, BRANCH_W), lambda b, r, i: (b, i, r, 0, 0))
    out_dims = (batch, seq // PERM_ROWS, dil, per, BRANCH_W)
    buf = lambda rows: pltpu.VMEM((rows, BRANCH_W), BF16)
    return pl.pallas_call(
        functools.partial(_attn_kernel, tq=tq, seq_m=seq_m),
        grid=(batch, dil, nq),
        in_specs=[main(0), halo(1, prev), main(1), halo(1, nxt), halo(2, prev), main(2), halo(2, nxt),
                  pl.BlockSpec((HEADS_PER_GROUP * Q_SUB, K_WIN), lambda b, r, i: (0, 0))],
        out_specs=[out_spec, out_spec],
        out_shape=[jax.ShapeDtypeStruct(out_dims, BF16), jax.ShapeDtypeStruct(out_dims, F32)],
        scratch_shapes=[buf(tq), buf(tq + 2 * HALF_KEYS), buf(tq + 2 * HALF_KEYS)],
        compiler_params=_params("parallel", "parallel", "parallel"),
        name=f"attn_g{group}",
    )(view, view, view, view, view, view, view, _attn_bias(group))


def _softplus(y):
    return jnp.maximum(y, 0.0) + jnp.log(1.0 + jnp.exp(-jnp.abs(y)))


def _stack_heads(xb, lane_h):
    return jnp.concatenate(
        [jnp.where(lane_h == h, xb, jnp.zeros_like(xb)) for h in range(RWKV_HEADS)], axis=0)


def _nt(a, b):
    return lax.dot_general(a, b, (((1,), (1,)), ((), ())), preferred_element_type=F32)


def _rwkv_kernel(rkvf_ref, rkvfh_ref, rkvb_ref, rkvbh_ref, latf_ref, latfh_ref, latb_ref, latbh_ref,
                 mur_ref, mul_ref, w0_ref, wup_ref, a0_ref, aup_ref, kk_ref, ka_ref, rk_ref,
                 lng_ref, lnb_ref, bdm_ref, bd1_ref, outf_ref, outb_ref,
                 st_ref, r_s, k_s, v_s, kk_s, b_s, lw_s, o_s, y_s, *, batch, tc):
    i = pl.program_id(0)
    c = RWKV_CHUNK
    unit = RWKV_UNIT
    n_units = tc // unit
    streams = [(d, b) for d in range(2) for b in range(batch)]
    ids = range(len(streams))
    back = [d == 1 for d, _ in streams]
    items = [(s, h) for s in ids for h in (reversed(range(unit // c)) if back[s] else range(unit // c))]
    wid = range(len(items))
    iback = [back[s] for s, _ in items]

    @pl.when(i == 0)
    def _():
        st_ref[...] = jnp.zeros_like(st_ref)

    lane_h = _lane_head(c)
    t_i = lax.broadcasted_iota(jnp.int32, (c, c), 0)
    s_i = lax.broadcasted_iota(jnp.int32, (c, c), 1)
    tau = lax.broadcasted_iota(jnp.int32, (c, BRANCH_W), 0)
    sig = lax.broadcasted_iota(jnp.int32, (c, BRANCH_W), 1) % c
    eye = sig == tau
    eye_b = jnp.where(eye, 1.0, 0.0).astype(BF16)
    tri_f = jnp.where(s_i <= t_i, 1.0, 0.0).astype(BF16)
    tri_b = jnp.where(s_i >= t_i, 1.0, 0.0).astype(BF16)
    strict = [(sig > tau) if bk else (sig < tau) for bk in iback]
    incl = [(sig >= tau) if bk else (sig <= tau) for bk in iback]
    bd1 = bd1_ref[...]
    bdm = bdm_ref[...]
    sm = lambda xb: _stack_heads(xb, lane_h)
    bf = lambda x: x.astype(BF16)
    top = lambda x: x[0:c]
    mid = lambda x: x[c:2 * c]
    rows_of = lambda x, k: x[k * unit:(k + 1) * unit]
    stack = lambda xs: jnp.concatenate(xs, axis=0)

    x_refs = [(rkvb_ref, rkvbh_ref, latb_ref, latbh_ref) if bk else (rkvf_ref, rkvfh_ref, latf_ref, latfh_ref)
              for bk in back]

    def unit_rows(j):
        start = [pl.multiple_of(((n_units - 1 - j) if bk else j) * unit, unit) for bk in back]
        return start, [pl.ds(st, unit) for st in start]

    def prepare(j, slot):
        start, rows = unit_rows(j)

        def mixed(s, x_ref, halo_ref, mu):
            d, b = streams[s]
            x = x_ref[b, rows[s], :]
            row = lax.broadcasted_iota(jnp.int32, x.shape, 0)
            if back[s]:
                near = x_ref[b, pl.ds(pl.multiple_of(jnp.minimum(start[s] + unit, tc - 8), 8), 8), :][0:1, :]
                edge = jnp.where(j == 0, jnp.where(i == 0, 0.0, halo_ref[b][0:1, :]), near)
                sh = jnp.where(row == unit - 1, edge, pltpu.roll(x, unit - 1, 0))
            else:
                near = x_ref[b, pl.ds(pl.multiple_of(jnp.maximum(start[s] - 8, 0), 8), 8), :][7:8, :]
                edge = jnp.where(j == 0, jnp.where(i == 0, 0.0, halo_ref[b][7:8, :]), near)
                sh = jnp.where(row == 0, edge, pltpu.roll(x, 1, 0))
            return x + (sh - x) * mu[d]

        xr = [mixed(s, x_refs[s][0], x_refs[s][1], mur_ref) for s in ids]
        xl = [mixed(s, x_refs[s][2], x_refs[s][3], mul_ref) for s in ids]
        th = [jnp.tanh(stack([xl[s] for s in ids if streams[s][0] == d])) for d in range(2)]
        yield
        w_log, a = [], []
        for d in range(2):
            xl_d = stack([xl[s] for s in ids if streams[s][0] == d])
            wl = -_softplus(-(w0_ref[d] + _dot_split_both(th[d], wup_ref[d]))) - 0.5
            al = jax.nn.sigmoid(a0_ref[d] + _dot_split_both(xl_d, aup_ref[d]))
            w_log += [rows_of(wl, k) for k in range(batch)]
            a += [rows_of(al, k) for k in range(batch)]
        k = [xr[s][:, BRANCH_W:2 * BRANCH_W] for s in ids]
        kk = [k[s] * kk_ref[streams[s][0]] for s in ids]
        yield
        kk_sq = _dot_split_lhs(stack([kk[s] * kk[s] for s in ids]), bd1)
        yield
        for s in ids:
            kks = kk[s] * lax.rsqrt(rows_of(kk_sq, s) + 1e-12)
            r_s[slot, s] = xr[s][:, 0:BRANCH_W]
            k_s[slot, s] = k[s] * (1.0 + (a[s] - 1.0) * ka_ref[streams[s][0]])
            v_s[slot, s] = xr[s][:, 2 * BRANCH_W:3 * BRANCH_W]
            kk_s[slot, s] = kks
            b_s[slot, s] = kks * a[s]
            lw_s[slot, s] = -jnp.exp(w_log[s])

    def advance(slot):
        ids = wid
        back = iback
        part = lambda ref: [ref[slot, s, h * c:(h + 1) * c, :] for s, h in items]
        lw, r, k2, v, kk, bq = part(lw_s), part(r_s), part(k_s), part(v_s), part(kk_s), part(b_s)

        def cumsum(s):
            lw_hi = bf(lw[s])
            lw_r = lw[s] - lw_hi.astype(F32)
            lw_mid = bf(lw_r)
            lw_lo = bf(lw_r - lw_mid.astype(F32))
            tri = tri_b if back[s] else tri_f
            return _dot(tri, lw_hi) + _dot(tri, lw_mid) + _dot(tri, lw_lo)

        cumi = [cumsum(s) for s in ids]
        yield
        tot = [cumi[s][0:1, :] if back[s] else cumi[s][c - 1:c, :] for s in ids]
        e_m = [jnp.exp(-cumi[s]) for s in ids]
        e_t = [jnp.exp(tot[s] - cumi[s]) for s in ids]
        pt = [bf(-kk[s] * jnp.exp(cumi[s] - lw[s])) for s in ids]
        rt = [r[s] * jnp.exp(cumi[s]) for s in ids]
        pr = [stack([pt[s], bf(rt[s])]) for s in ids]
        ab = [_nt(pr[s], sm(bf(bq[s] * e_m[s]))) for s in ids]
        ak = [_nt(pr[s], sm(bf(k2[s] * e_m[s]))) for s in ids]
        yield
        a_pb = [jnp.where(strict[s], top(ab[s]), 0.0) for s in ids]
        a_rb = [bf(jnp.where(incl[s], mid(ab[s]), 0.0)) for s in ids]
        a_pk = [bf(jnp.where(strict[s], top(ak[s]), 0.0)) for s in ids]
        a_rk = [bf(jnp.where(incl[s], mid(ak[s]), 0.0)) for s in ids]
        l_b = [bf(_nt(eye_b, sm(bf(bq[s] * e_t[s])))) for s in ids]
        l_k = [bf(_nt(eye_b, sm(bf(k2[s] * e_t[s])))) for s in ids]
        yield

        tinv = [jnp.where(eye, 1.0, a_pb[s]) for s in ids]
        apow = [bf(a_pb[s]) for s in ids]
        a2 = [_dot(apow[s], sm(apow[s])) for s in ids]
        av = [_dot(stack([a_pk[s], a_rk[s], l_k[s]]), sm(bf(v[s]))) for s in ids]
        yield
        for _ in range(NEUMANN_DOUBLINGS - 1):
            apow = [bf(a2[s]) for s in ids]
            both = [_dot(stack([bf(tinv[s]), apow[s]]), sm(apow[s])) for s in ids]
            yield
            tinv = [tinv[s] + top(both[s]) for s in ids]
            a2 = [mid(both[s]) for s in ids]
        tinv = [tinv[s] + _dot(bf(tinv[s]), sm(bf(a2[s]))) for s in ids]
        yield
        tb = [bf(t) for t in tinv]

        ah = [_dot(tb[s], sm(pt[s])) for s in ids]
        uh = [_dot(tb[s], sm(bf(top(av[s])))) for s in ids]
        yield
        lhs = [stack([a_rb[s], l_b[s]]) for s in ids]
        o_a = [_dot(lhs[s], sm(bf(ah[s]))) for s in ids]
        o_u = [_dot(lhs[s], sm(bf(uh[s]))) for s in ids]
        yield
        rh = [rt[s] + top(o_a[s]) for s in ids]
        mt = [jnp.where(eye, jnp.exp(tot[s]), 0.0) + mid(o_a[s]) for s in ids]
        oh = [mid(av[s]) + top(o_u[s]) for s in ids]
        gt = [mid(o_u[s]) + av[s][2 * c:3 * c] for s in ids]

        bonus = _dot(bf(stack([r[w] * k2[w] * rk_ref[streams[items[w][0]][0]] for w in ids])), bd1) * stack(v)
        for w, (s, h) in enumerate(items):
            y_s[slot, s, h * c:(h + 1) * c, :] = bonus[w * c:(w + 1) * c]
        state = [st_ref[s] for s in range(len(streams))]
        for step in range(unit // c):
            now = [w for w in ids if w % (unit // c) == step]
            out = {w: _dot(stack([bf(rh[w]), bf(mt[w])]), sm(bf(state[items[w][0]]))) for w in now}
            yield
            for w in now:
                s, h = items[w]
                state[s] = mid(out[w]) + gt[w]
                o_s[slot, s, h * c:(h + 1) * c, :] = top(out[w]) + oh[w]
        for s in range(len(streams)):
            st_ref[s] = state[s]

    def finish(j, slot):
        _, rows = unit_rows(j)
        o = stack([o_s[slot, s] for s in ids])
        mean = _dot_split_lhs(o, bdm)
        yield
        xc = o - mean
        var = _dot(bf(xc * xc), bdm)
        yield
        y = xc * lax.rsqrt(var + GN_EPS) * lng_ref[...] + lnb_ref[...] + stack([y_s[slot, s] for s in ids])
        for s, (d, b) in enumerate(streams):
            out_ref = outb_ref if d == 1 else outf_ref
            out_ref[b, rows[s], :] = rows_of(y, s)

    def interleave(*phases):
        live = list(phases)
        while live:
            for phase in list(live):
                if next(phase, "done") == "done":
                    live.remove(phase)

    o_s[1] = jnp.zeros(o_s.shape[1:], F32)
    y_s[1] = jnp.zeros(y_s.shape[1:], F32)
    interleave(prepare(0, 0))

    def pipeline(j, carry):
        interleave(finish(jnp.maximum(j - 1, 0), (j + 1) % 2),
                   advance(j % 2),
                   prepare(jnp.minimum(j + 1, n_units - 1), (j + 1) % 2))
        return carry

    lax.fori_loop(0, n_units, pipeline, 0)
    interleave(finish(n_units - 1, (n_units - 1) % 2))


def _rwkv(rkv, lat, lp, batch, seq):
    tc = min(RWKV_ROWS, seq)
    nt = seq // tc
    hb = tc // 8
    last_halo = seq // 8 - 1
    fwd = lambda i: i
    bwd = lambda i: nt - 1 - i
    fwd_halo = lambda i: jnp.maximum(i * hb - 1, 0)
    bwd_halo = lambda i: jnp.minimum((bwd(i) + 1) * hb, last_halo)
    full3 = lambda shape: pl.BlockSpec(shape, lambda i: (0, 0, 0))
    full2 = lambda shape: pl.BlockSpec(shape, lambda i: (0, 0))
    n_streams = 2 * batch
    out_shape = jax.ShapeDtypeStruct((batch, seq, BRANCH_W), F32)
    return pl.pallas_call(
        functools.partial(_rwkv_kernel, batch=batch, tc=tc),
        grid=(nt,),
        in_specs=[
            pl.BlockSpec((batch, tc, RKV_W), lambda i: (0, fwd(i), 0)),
            pl.BlockSpec((batch, 8, RKV_W), lambda i: (0, fwd_halo(i), 0)),
            pl.BlockSpec((batch, tc, RKV_W), lambda i: (0, bwd(i), 0)),
            pl.BlockSpec((batch, 8, RKV_W), lambda i: (0, bwd_halo(i), 0)),
            pl.BlockSpec((batch, tc, LAT_W), lambda i: (0, fwd(i), 0)),
            pl.BlockSpec((batch, 8, LAT_W), lambda i: (0, fwd_halo(i), 0)),
            pl.BlockSpec((batch, tc, LAT_W), lambda i: (0, bwd(i), 1)),
            pl.BlockSpec((batch, 8, LAT_W), lambda i: (0, bwd_halo(i), 1)),
            full3((2, 1, RKV_W)), full3((2, 1, LAT_W)), full3((2, 1, BRANCH_W)),
            full3((2, LAT_W, BRANCH_W)), full3((2, 1, BRANCH_W)), full3((2, LAT_W, BRANCH_W)),
            full3((2, 1, BRANCH_W)), full3((2, 1, BRANCH_W)), full3((2, 1, BRANCH_W)),
            full2((1, BRANCH_W)), full2((1, BRANCH_W)),
            full2((BRANCH_W, BRANCH_W)), full2((BRANCH_W, BRANCH_W)),
        ],
        out_specs=[pl.BlockSpec((batch, tc, BRANCH_W), lambda i: (0, fwd(i), 0)),
                   pl.BlockSpec((batch, tc, BRANCH_W), lambda i: (0, bwd(i), 0))],
        out_shape=[out_shape, out_shape],
        scratch_shapes=[pltpu.VMEM((n_streams, RWKV_CHUNK, BRANCH_W), F32)]
        + [pltpu.VMEM((2, n_streams, RWKV_UNIT, BRANCH_W), F32)] * 8,
        compiler_params=_params("arbitrary"),
        name="rwkv",
    )(rkv, rkv, rkv, rkv, lat, lat, lat, lat, lp["mu_rkv"], lp["mu_lat"], lp["w0"], lp["w_up_pad"],
      lp["a0"], lp["a_up_pad"], lp["k_k"], lp["k_a"], lp["r_k"], lp["ln_g"], lp["ln_b"],
      lp["bd_mean"], lp["bd_ones"])


def _silu(z):
    return z * jax.nn.sigmoid(z)


def _merge_kernel(x_ref, g_ref, o0_ref, l0_ref, o1_ref, l1_ref, o2_ref, l2_ref, z_ref,
                  ubp_ref, bc_ref, ubn_ref, yd0_ref, yd1_ref,
                  unperm_ref, poolw_ref, pools_ref, sgg_ref, sgw_ref, sgb_ref, wg_ref, wb_ref, wo_ref,
                  out_ref, *, tm, seq):
    i = pl.program_id(1)
    n_tiles = seq // tm
    x = x_ref[0]
    ms = jnp.mean(x * x, axis=-1, keepdims=True)
    h = (x * lax.rsqrt(ms + EPS) * g_ref[...]).astype(BF16)
    z = z_ref[0]
    lane_g = _lane_head(tm)
    parts = [slice(p * PERM_ROWS, (p + 1) * PERM_ROWS) for p in range(tm // PERM_ROWS)]
    gate_logits = [[_dot(h[p], wg_ref[:, n * D_MODEL:(n + 1) * D_MODEL]) for n in range(N_BRANCH)]
                   for p in parts]

    def natural(ref, group, exact):
        tiles = []
        for t in range(tm // PERM_ROWS):
            if group == 0:
                tiles.append(ref[0, t, 0].astype(F32))
                continue
            v = jnp.concatenate([ref[0, t, r] for r in range(ATTN_DILATIONS[group])], axis=0)
            unperm = unperm_ref[group - 1]
            tiles.append(_dot_split_lhs_rhs(unperm, v) if exact else _dot(unperm, v))
        return jnp.concatenate(tiles, axis=0)

    l0, l1, l2 = natural(l0_ref, 0, True), natural(l1_ref, 1, True), natural(l2_ref, 2, True)
    mx = jnp.maximum(jnp.maximum(l0, l1), l2)
    e0, e1, e2 = jnp.exp(l0 - mx), jnp.exp(l1 - mx), jnp.exp(l2 - mx)
    y_a = (e0 * natural(o0_ref, 0, False) + e1 * natural(o1_ref, 1, False)
           + e2 * natural(o2_ref, 2, False)) / (e0 + e1 + e2)

    bc = bc_ref[0]
    u = bc[:, 0:BRANCH_W]
    prev = jnp.where(i == 0, 0.0, ubp_ref[0][:, 0:BRANCH_W])
    nxt = jnp.where(i == n_tiles - 1, 0.0, ubn_ref[0][:, 0:BRANCH_W])
    ext = jnp.concatenate([prev, u, nxt], axis=0)
    n_ext = tm + 2 * POOL_HALO
    f1 = ext + pltpu.roll(ext, 1, 0)
    f2 = pltpu.roll(f1, 1, 0) + pltpu.roll(f1, n_ext - 1, 0)
    f4 = pltpu.roll(f2, 2, 0) + pltpu.roll(f2, n_ext - 2, 0)
    f8 = pltpu.roll(f4, 4, 0) + pltpu.roll(f4, n_ext - 4, 0)
    lane_e = _lane_head(n_ext)
    win = jnp.where(lane_e == 0, f1, jnp.where(lane_e == 1, f2, jnp.where(lane_e == 2, f4, f8)))
    win = win[POOL_HALO:POOL_HALO + tm, :]
    pos = i * tm + lax.broadcasted_iota(jnp.int32, (tm, BRANCH_W), 0)
    half = jnp.where(lane_g == 0, POOL_HALF[0],
                     jnp.where(lane_g == 1, POOL_HALF[1], jnp.where(lane_g == 2, POOL_HALF[2], POOL_HALF[3])))
    cnt = (jnp.minimum(pos + half, seq) - jnp.maximum(pos - half, 0)).astype(F32)
    y_b = _dot((win / cnt - u).astype(BF16), poolw_ref[...]) * pools_ref[...]

    u_c = bc[:, BRANCH_W:2 * BRANCH_W]
    v_c = bc[:, 2 * BRANCH_W:3 * BRANCH_W]
    msv = jnp.mean(v_c * v_c, axis=-1, keepdims=True)
    v_n = (v_c * lax.rsqrt(msv + EPS) * sgg_ref[...]).astype(BF16)
    lane_c = _lane_head(CHUNK)
    sv_chunks = []
    for ch in range(tm // CHUNK):
        vch = v_n[ch * CHUNK:(ch + 1) * CHUNK, :]
        sv = jnp.zeros((CHUNK, BRANCH_W), F32)
        for g in range(SG_GROUPS):
            sv = jnp.where(lane_c == g, _dot(sgw_ref[g], vch), sv)
        sv_chunks.append(sv + sgb_ref[...])
    y_c = u_c * jnp.concatenate(sv_chunks, axis=0)

    y_d = yd0_ref[0] + yd1_ref[0]

    ys = [(y * _silu(z[:, n * BRANCH_W:(n + 1) * BRANCH_W])).astype(BF16)
          for n, y in enumerate((y_a, y_b, y_c, y_d))]
    branch = [[_dot(ys[n][p], wb_ref[n]) for n in range(N_BRANCH)] for p in parts]
    merged = []
    for gl, br in zip(gate_logits, branch):
        acc = jax.nn.sigmoid(gl[0]) * br[0]
        for n in range(1, N_BRANCH):
            acc = acc + jax.nn.sigmoid(gl[n]) * br[n]
        merged.append(acc.astype(BF16))
    for p, m in zip(parts, merged):
        out_ref[0, p, :] = x[p] + _dot(m, wo_ref[...])


def _merge(x, attn, z, bc, yd0, yd1, lp, batch, seq):
    tm = MERGE_ROWS
    hb = tm // POOL_HALO
    last_halo = seq // POOL_HALO - 1
    row = lambda w: pl.BlockSpec((1, tm, w), lambda b, i: (b, i, 0))
    const2 = lambda shape: pl.BlockSpec(shape, lambda b, i: (0, 0))
    const3 = lambda shape: pl.BlockSpec(shape, lambda b, i: (0, 0, 0))
    assert tm % PERM_ROWS == 0
    in_specs = [row(D_MODEL), const2((1, D_MODEL))]
    for dil in ATTN_DILATIONS:
        slab = pl.BlockSpec((1, tm // PERM_ROWS, dil, PERM_ROWS // dil, BRANCH_W), lambda b, i: (b, i, 0, 0, 0))
        in_specs += [slab, slab]
    in_specs += [
        row(Z_W),
        pl.BlockSpec((1, POOL_HALO, BRANCH_W), lambda b, i: (b, jnp.maximum(i * hb - 1, 0), 0)),
        row(BC_W),
        pl.BlockSpec((1, POOL_HALO, BRANCH_W), lambda b, i: (b, jnp.minimum((i + 1) * hb, last_halo), 0)),
        row(BRANCH_W), row(BRANCH_W),
        const3((N_GROUPS - 1, PERM_ROWS, PERM_ROWS)),
        const2((BRANCH_W, BRANCH_W)), const2((1, BRANCH_W)), const2((1, BRANCH_W)),
        const3((SG_GROUPS, CHUNK, CHUNK)), const2((CHUNK, BRANCH_W)),
        const2((D_MODEL, N_BRANCH * D_MODEL)), const3((N_BRANCH, BRANCH_W, D_MODEL)),
        const2((D_MODEL, D_MODEL)),
    ]
    flat_attn = [a for pair in attn for a in pair]
    return pl.pallas_call(
        functools.partial(_merge_kernel, tm=tm, seq=seq),
        grid=(batch, seq // tm),
        in_specs=in_specs,
        out_specs=row(D_MODEL),
        out_shape=jax.ShapeDtypeStruct((batch, seq, D_MODEL), F32),
        compiler_params=_params("parallel", "parallel"),
        name="merge",
    )(x, lp["norm_g"], *flat_attn, z, bc, bc, bc, yd0, yd1, _residue_perms(True),
      lp["pool_w_bd"], lp["pool_scale"], lp["sg_norm_g"], lp["sg_w"], lp["sg_bias"],
      lp["w_gate"], lp["w_branch"], lp["w_out"])


def _block_diag(blocks):
    n = blocks.shape[0]
    w = blocks.shape[1]
    out = jnp.zeros((n * w, n * w), blocks.dtype)
    for g in range(n):
        out = out.at[g * w:(g + 1) * w, g * w:(g + 1) * w].set(blocks[g])
    return out


def _layer_params(l, norm_g, w_in, q_norm_g, k_norm_g, pool_w, pool_scale, sg_norm_g, sg_w, sg_b,
                  mu_rkv, mu_lat, w0, w_up, a0, a_up, k_k, k_a, r_k, ln_g, ln_b, w_branch, w_out):
    w = w_in[l]
    seg = lambda lo, width: w[:, lo:lo + width]
    w_small = jnp.concatenate([
        seg(0, QKV_W), seg(3840, RKV_W), seg(4608, 2 * LAT_W), seg(2560, BRANCH_W), seg(3072, 2 * BRANCH_W),
        seg(2304, BRANCH_W), seg(2816, BRANCH_W), seg(3584, BRANCH_W), seg(4864, BRANCH_W)], axis=1)
    ones_blocks = jnp.ones((RWKV_HEADS, HEAD_DIM, HEAD_DIM), F32)
    zeros_lat = jnp.zeros((2, LAT_W // 2, BRANCH_W), F32)
    vec3 = lambda p: p[l][:, None, :]
    return {
        "norm_g": norm_g[l][None, :],
        "w_small": w_small.astype(BF16),
        "w_gate": w[:, PROJ_SMALL_W:].astype(BF16),
        "q_gain": jnp.tile(q_norm_g[l], HEADS_PER_GROUP)[None, :],
        "k_gain": jnp.tile(k_norm_g[l], HEADS_PER_GROUP)[None, :],
        "bd_mean": (_block_diag(ones_blocks) / HEAD_DIM).astype(BF16),
        "bd_ones": _block_diag(ones_blocks).astype(BF16),
        "pool_w_bd": _block_diag(pool_w[l]).astype(BF16),
        "pool_scale": pool_scale[l][None, :],
        "sg_norm_g": sg_norm_g[l][None, :],
        "sg_w": sg_w[l].astype(BF16),
        "sg_bias": jnp.repeat(sg_b[l].T, HEAD_DIM, axis=1),
        "mu_rkv": vec3(mu_rkv), "mu_lat": vec3(mu_lat), "w0": vec3(w0), "a0": vec3(a0),
        "k_k": vec3(k_k), "k_a": vec3(k_a), "r_k": vec3(r_k),
        "w_up_pad": jnp.concatenate([w_up[l], zeros_lat], axis=1),
        "a_up_pad": jnp.concatenate([zeros_lat, a_up[l]], axis=1),
        "ln_g": ln_g[l][None, :], "ln_b": ln_b[l][None, :],
        "w_branch": w_branch[l].astype(BF16),
        "w_out": w_out[l].astype(BF16),
    }


def _layer(x, lp):
    batch, seq, _ = x.shape
    qkv, rkv, lat, bc, z = _proj(x.reshape(batch * seq, D_MODEL), lp)
    attn = [_attn_group(qkv, g, batch, seq) for g in range(N_GROUPS)]
    rkv = rkv.reshape(batch, seq, RKV_W)
    lat = lat.reshape(batch, seq, 2 * LAT_W)
    yd0, yd1 = _rwkv(rkv, lat, lp, batch, seq)
    return _merge(x, attn, z.reshape(batch, seq, Z_W), bc.reshape(batch, seq, BC_W), yd0, yd1, lp, batch, seq)


def kernel(x_prompt, x_sample, norm_g, w_in, q_norm_g, k_norm_g, pool_w, pool_scale, sg_norm_g, sg_w, sg_b,
           mu_rkv, mu_lat, w0, w_up, a0, a_up, k_k, k_a, r_k, ln_g, ln_b, w_branch, w_out):
    weights = (norm_g, w_in, q_norm_g, k_norm_g, pool_w, pool_scale, sg_norm_g, sg_w, sg_b,
               mu_rkv, mu_lat, w0, w_up, a0, a_up, k_k, k_a, r_k, ln_g, ln_b, w_branch, w_out)
    layers = [_layer_params(l, *weights) for l in range(norm_g.shape[0])]
    outs = []
    for x in (x_prompt, x_sample):
        for lp in layers:
            x = _layer(x, lp)
        outs.append(x)
    return tuple(outs)
```

```python
import functools

import jax
import jax.numpy as jnp
import numpy as np
from jax import lax
from jax.experimental import pallas as pl
from jax.experimental.pallas import tpu as pltpu

F32 = jnp.float32
BF16 = jnp.bfloat16

D_MODEL = 1024
BRANCH_W = 256
N_BRANCH = 4
EPS = 1e-6
NEG_INF = -1e30

HEAD_DIM = 64
ATTN_DILATIONS = (1, 4, 16)
N_GROUPS = 3
HEADS_PER_GROUP = 4
N_ATTN_HEADS = N_GROUPS * HEADS_PER_GROUP
ATTN_QKV = N_ATTN_HEADS * HEAD_DIM
HALF_KEYS = 64
Q_SUB = 128
K_WIN = Q_SUB + 2 * HALF_KEYS

POOL_HALF = (1, 2, 4, 8)
POOL_HALO = 8
CHUNK = 128
SG_GROUPS = 4

RWKV_HEADS = 4
RWKV_N = 64
RWKV_CHUNK = 64
RWKV_UNIT = 128
LAT_W = 128
GN_EPS = 64e-5
NEUMANN_DOUBLINGS = 5

QKV_W = 3 * ATTN_QKV
RKV_W = 3 * BRANCH_W
BC_W = 3 * BRANCH_W
Z_W = 4 * BRANCH_W
PROJ_SMALL_W = QKV_W + RKV_W + 2 * LAT_W + BC_W + Z_W

PROJ_ROWS = 512
MERGE_ROWS = 512
PERM_ROWS = 256
ATTN_ROWS = 1024
RWKV_ROWS = 512
VMEM_LIMIT = 56 * 1024 * 1024


def _params(*sem):
    return pltpu.CompilerParams(dimension_semantics=sem, vmem_limit_bytes=VMEM_LIMIT)


def _dot(a, b):
    return jnp.dot(a, b, preferred_element_type=F32)


def _split_bf16(x):
    hi = x.astype(BF16)
    lo = (x - hi.astype(F32)).astype(BF16)
    return hi, lo


def _dot_split_lhs(x, m_bf16):
    hi, lo = _split_bf16(x)
    return _dot(hi, m_bf16) + _dot(lo, m_bf16)


def _dot_split_lhs_rhs(m_bf16, x):
    hi, lo = _split_bf16(x)
    return _dot(m_bf16, hi) + _dot(m_bf16, lo)


def _dot_split_both(a, b):
    a_hi, a_lo = _split_bf16(a)
    b_hi, b_lo = _split_bf16(b)
    return _dot(a_hi, b_hi) + _dot(a_lo, b_hi) + _dot(a_hi, b_lo)


def _lane_head(rows):
    return lax.broadcasted_iota(jnp.int32, (rows, BRANCH_W), 1) // HEAD_DIM


def _proj_kernel(x_ref, g_ref, w_ref, qg_ref, kg_ref, perm_ref, qkv_ref, rkv_ref, lat_ref, bc_ref, z_ref):
    x = x_ref[...]
    ms = jnp.mean(x * x, axis=-1, keepdims=True)
    h = (x * lax.rsqrt(ms + EPS) * g_ref[...]).astype(BF16)
    n_chunks = 3 * N_GROUPS

    def project(c):
        return _dot(h, w_ref[:, c * BRANCH_W:(c + 1) * BRANCH_W])

    lane_h = _lane_head(x.shape[0])

    def normalize(c, y):
        if c < 2 * N_GROUPS:
            sq = y * y
            msq = jnp.zeros_like(sq)
            for hd in range(HEADS_PER_GROUP):
                mine = lane_h == hd
                tot = jnp.sum(jnp.where(mine, sq, 0.0), axis=-1, keepdims=True)
                msq = jnp.where(mine, tot * (1.0 / HEAD_DIM), msq)
            if c < N_GROUPS:
                y = y * lax.rsqrt(msq + EPS) * qg_ref[...] * (HEAD_DIM ** -0.5)
            else:
                y = y * lax.rsqrt(msq + EPS) * kg_ref[...]
        return y.astype(BF16)

    def store(c, yb):
        group = c % N_GROUPS
        if group > 0:
            perm = perm_ref[group - 1]
            yb = jnp.concatenate([_dot(perm, yb[t * PERM_ROWS:(t + 1) * PERM_ROWS]).astype(BF16)
                                  for t in range(yb.shape[0] // PERM_ROWS)], axis=0)
        qkv_ref[:, c * BRANCH_W:(c + 1) * BRANCH_W] = yb

    others = []
    o = QKV_W
    for ref, width in ((rkv_ref, RKV_W), (lat_ref, 2 * LAT_W), (bc_ref, BC_W), (z_ref, Z_W)):
        others.append((ref, o, width))
        o += width

    projected, normalized = {}, {}
    for step in range(n_chunks + 2):
        if step < n_chunks:
            projected[step] = project(step)
        elif others:
            ref, lo, width = others.pop(0)
            ref[...] = _dot(h, w_ref[:, lo:lo + width])
        if 1 <= step <= n_chunks:
            normalized[step - 1] = normalize(step - 1, projected.pop(step - 1))
        if step >= 2:
            store(step - 2, normalized.pop(step - 2))
    for ref, lo, width in others:
        ref[...] = _dot(h, w_ref[:, lo:lo + width])


def _residue_perms(transpose):
    mats = []
    for dil in ATTN_DILATIONS[1:]:
        per = PERM_ROWS // dil
        p = np.zeros((PERM_ROWS, PERM_ROWS), np.float32)
        old = np.arange(PERM_ROWS)
        p[(old % dil) * per + old // dil, old] = 1.0
        mats.append(p.T if transpose else p)
    return jnp.asarray(np.stack(mats), BF16)


def _proj(x2d, lp):
    rows = x2d.shape[0]
    tm = PROJ_ROWS
    assert tm % PERM_ROWS == 0
    const = lambda i: (0, 0)
    row = lambda i: (i, 0)
    widths = (QKV_W, RKV_W, 2 * LAT_W, BC_W, Z_W)
    dtypes = (BF16, F32, F32, F32, F32)
    return pl.pallas_call(
        _proj_kernel,
        grid=(rows // tm,),
        in_specs=[
            pl.BlockSpec((tm, D_MODEL), row),
            pl.BlockSpec((1, D_MODEL), const),
            pl.BlockSpec((D_MODEL, PROJ_SMALL_W), const),
            pl.BlockSpec((1, BRANCH_W), const),
            pl.BlockSpec((1, BRANCH_W), const),
            pl.BlockSpec((N_GROUPS - 1, PERM_ROWS, PERM_ROWS), lambda i: (0, 0, 0)),
        ],
        out_specs=[pl.BlockSpec((tm, w), row) for w in widths],
        out_shape=[jax.ShapeDtypeStruct((rows, w), dt) for w, dt in zip(widths, dtypes)],
        compiler_params=_params("parallel"),
        name="proj",
    )(x2d, lp["norm_g"], lp["w_small"], lp["q_gain"], lp["k_gain"], _residue_perms(False))


def _attn_kernel(q_ref, kp_ref, k_ref, kn_ref, vp_ref, v_ref, vn_ref, bias_ref, o_ref, lse_ref,
                 qbuf, kbuf, vbuf, *, tq, seq_m):
    i = pl.program_id(2)

    def gather(dst, at, src):
        per = src.shape[2]
        for j in range(src.shape[1]):
            dst[at + j * per:at + (j + 1) * per, :] = src[0, j]

    gather(qbuf, 0, q_ref)
    gather(kbuf, 0, kp_ref)
    gather(kbuf, HALF_KEYS, k_ref)
    gather(kbuf, HALF_KEYS + tq, kn_ref)
    gather(vbuf, 0, vp_ref)
    gather(vbuf, HALF_KEYS, v_ref)
    gather(vbuf, HALF_KEYS + tq, vn_ref)
    per_out = o_ref.shape[2]
    lane_h = _lane_head(Q_SUB)
    col = lax.broadcasted_iota(jnp.int32, (1, K_WIN), 1)
    bias = bias_ref[...]
    for sb in range(tq // Q_SUB):
        r0 = sb * Q_SUB
        qb = qbuf[r0:r0 + Q_SUB, :]
        qs = jnp.concatenate(
            [jnp.where(lane_h == h, qb, jnp.zeros_like(qb)) for h in range(HEADS_PER_GROUP)], axis=0)
        s = lax.dot_general(qs, kbuf[r0:r0 + K_WIN, :], (((1,), (1,)), ((), ())),
                            preferred_element_type=F32)
        kpos = i * tq + (r0 - HALF_KEYS) + col
        s = s + bias + jnp.where((kpos >= 0) & (kpos < seq_m), 0.0, NEG_INF)
        m = jnp.max(s, axis=-1, keepdims=True)
        p = jnp.exp(s - m)
        l = jnp.sum(p, axis=-1, keepdims=True)
        pv = _dot(p.astype(BF16), vbuf[r0:r0 + K_WIN, :])
        on = pv / l
        ls = m + jnp.log(l)
        o = jnp.zeros((Q_SUB, BRANCH_W), F32)
        lo = jnp.zeros((Q_SUB, BRANCH_W), F32)
        for h in range(HEADS_PER_GROUP):
            sel = lane_h == h
            o = jnp.where(sel, on[h * Q_SUB:(h + 1) * Q_SUB, :], o)
            lo = jnp.where(sel, ls[h * Q_SUB:(h + 1) * Q_SUB, :], lo)
        if per_out >= Q_SUB:
            o_ref[0, r0 // per_out, r0 % per_out:r0 % per_out + Q_SUB, :] = o.astype(o_ref.dtype)
            lse_ref[0, r0 // per_out, r0 % per_out:r0 % per_out + Q_SUB, :] = lo
        for j in range(Q_SUB // per_out):
            o_ref[0, r0 // per_out + j] = o[j * per_out:(j + 1) * per_out, :].astype(o_ref.dtype)
            lse_ref[0, r0 // per_out + j] = lo[j * per_out:(j + 1) * per_out, :]


def _attn_bias(group):
    dil = ATTN_DILATIONS[group]
    i = np.arange(Q_SUB)[:, None]
    j = np.arange(K_WIN)[None, :]
    off = np.abs(j - HALF_KEYS - i).astype(np.float32)
    slopes = 2.0 ** (-8.0 * np.arange(1, N_ATTN_HEADS + 1, dtype=np.float32) / N_ATTN_HEADS)
    blocks = []
    for h in range(HEADS_PER_GROUP):
        slope = slopes[group * HEADS_PER_GROUP + h]
        blocks.append(np.where(off <= HALF_KEYS, -slope * (off * dil), NEG_INF))
    return jnp.asarray(np.concatenate(blocks, axis=0), F32)


def _attn_group(qkv, group, batch, seq):
    dil = ATTN_DILATIONS[group]
    seq_m = seq // dil
    tq = min(ATTN_ROWS, seq_m)
    nq = seq_m // tq
    per = PERM_ROWS // dil
    halo_per = min(per, HALF_KEYS)
    halo_split = per // halo_per
    last_halo = seq_m // HALF_KEYS - 1
    view = qkv.reshape(batch, seq // PERM_ROWS, dil, per, QKV_W)

    def col(which):
        return which * N_GROUPS + group

    def main(which):
        return pl.BlockSpec((1, tq // per, None, per, BRANCH_W),
                            lambda b, r, i: (b, i, r, 0, col(which)))

    def halo(which, index):
        if halo_split == 1:
            return pl.BlockSpec((1, HALF_KEYS // per, None, per, BRANCH_W),
                                lambda b, r, i: (b, index(i), r, 0, col(which)))
        return pl.BlockSpec((1, 1, None, HALF_KEYS, BRANCH_W),
                            lambda b, r, i: (b, index(i) // halo_split, r, index(i) % halo_split, col(which)))

    hb = tq // HALF_KEYS
    prev = lambda i: jnp.maximum(i * hb - 1, 0)
    nxt = lambda i: jnp.minimum((i + 1) * hb, last_halo)
    out_spec = pl.BlockSpec((1, tq // per, None, per, BRANCH_W), lambda b, r, i: (b, i, r, 0, 0))
    out_dims = (batch, seq // PERM_ROWS, dil, per, BRANCH_W)
    buf = lambda rows: pltpu.VMEM((rows, BRANCH_W), BF16)
    return pl.pallas_call(
        functools.partial(_attn_kernel, tq=tq, seq_m=seq_m),
        grid=(batch, dil, nq),
        in_specs=[main(0), halo(1, prev), main(1), halo(1, nxt), halo(2, prev), main(2), halo(2, nxt),
                  pl.BlockSpec((HEADS_PER_GROUP * Q_SUB, K_WIN), lambda b, r, i: (0, 0))],
        out_specs=[out_spec, out_spec],
        out_shape=[jax.ShapeDtypeStruct(out_dims, BF16), jax.ShapeDtypeStruct(out_dims, F32)],
        scratch_shapes=[buf(tq), buf(tq + 2 * HALF_KEYS), buf(tq + 2 * HALF_KEYS)],
        compiler_params=_params("parallel", "parallel", "parallel"),
        name=f"attn_g{group}",
    )(view, view, view, view, view, view, view, _attn_bias(group))


def _softplus(y):
    return jnp.maximum(y, 0.0) + jnp.log(1.0 + jnp.exp(-jnp.abs(y)))


def _stack_heads(xb, lane_h):
    return jnp.concatenate(
        [jnp.where(lane_h == h, xb, jnp.zeros_like(xb)) for h in range(RWKV_HEADS)], axis=0)


def _nt(a, b):
    return lax.dot_general(a, b, (((1,), (1,)), ((), ())), preferred_element_type=F32)


def _rwkv_kernel(rkvf_ref, rkvfh_ref, rkvb_ref, rkvbh_ref, latf_ref, latfh_ref, latb_ref, latbh_ref,
                 mur_ref, mul_ref, w0_ref, wup_ref, a0_ref, aup_ref, kk_ref, ka_ref, rk_ref,
                 lng_ref, lnb_ref, bdm_ref, bd1_ref, outf_ref, outb_ref,
                 st_ref, r_s, k_s, v_s, kk_s, b_s, lw_s, o_s, y_s, *, batch, tc):
    i = pl.program_id(0)
    c = RWKV_CHUNK
    unit = RWKV_UNIT
    n_units = tc // unit
    streams = [(d, b) for d in range(2) for b in range(batch)]
    ids = range(len(streams))
    back = [d == 1 for d, _ in streams]
    items = [(s, h) for s in ids for h in (reversed(range(unit // c)) if back[s] else range(unit // c))]
    wid = range(len(items))
    iback = [back[s] for s, _ in items]

    @pl.when(i == 0)
    def _():
        st_ref[...] = jnp.zeros_like(st_ref)

    lane_h = _lane_head(c)
    t_i = lax.broadcasted_iota(jnp.int32, (c, c), 0)
    s_i = lax.broadcasted_iota(jnp.int32, (c, c), 1)
    tau = lax.broadcasted_iota(jnp.int32, (c, BRANCH_W), 0)
    sig = lax.broadcasted_iota(jnp.int32, (c, BRANCH_W), 1) % c
    eye = sig == tau
    eye_b = jnp.where(eye, 1.0, 0.0).astype(BF16)
    tri_f = jnp.where(s_i <= t_i, 1.0, 0.0).astype(BF16)
    tri_b = jnp.where(s_i >= t_i, 1.0, 0.0).astype(BF16)
    strict = [(sig > tau) if bk else (sig < tau) for bk in iback]
    incl = [(sig >= tau) if bk else (sig <= tau) for bk in iback]
    bd1 = bd1_ref[...]
    bdm = bdm_ref[...]
    sm = lambda xb: _stack_heads(xb, lane_h)
    bf = lambda x: x.astype(BF16)
    top = lambda x: x[0:c]
    mid = lambda x: x[c:2 * c]
    rows_of = lambda x, k: x[k * unit:(k + 1) * unit]
    stack = lambda xs: jnp.concatenate(xs, axis=0)

    x_refs = [(rkvb_ref, rkvbh_ref, latb_ref, latbh_ref) if bk else (rkvf_ref, rkvfh_ref, latf_ref, latfh_ref)
              for bk in back]

    def unit_rows(j):
        start = [pl.multiple_of(((n_units - 1 - j) if bk else j) * unit, unit) for bk in back]
        return start, [pl.ds(st, unit) for st in start]

    def prepare(j, slot):
        start, rows = unit_rows(j)

        def mixed(s, x_ref, halo_ref, mu):
            d, b = streams[s]
            x = x_ref[b, rows[s], :]
            row = lax.broadcasted_iota(jnp.int32, x.shape, 0)
            if back[s]:
                near = x_ref[b, pl.ds(pl.multiple_of(jnp.minimum(start[s] + unit, tc - 8), 8), 8), :][0:1, :]
                edge = jnp.where(j == 0, jnp.where(i == 0, 0.0, halo_ref[b][0:1, :]), near)
                sh = jnp.where(row == unit - 1, edge, pltpu.roll(x, unit - 1, 0))
            else:
                near = x_ref[b, pl.ds(pl.multiple_of(jnp.maximum(start[s] - 8, 0), 8), 8), :][7:8, :]
                edge = jnp.where(j == 0, jnp.where(i == 0, 0.0, halo_ref[b][7:8, :]), near)
                sh = jnp.where(row == 0, edge, pltpu.roll(x, 1, 0))
            return x + (sh - x) * mu[d]

        xr = [mixed(s, x_refs[s][0], x_refs[s][1], mur_ref) for s in ids]
        xl = [mixed(s, x_refs[s][2], x_refs[s][3], mul_ref) for s in ids]
        th = [jnp.tanh(stack([xl[s] for s in ids if streams[s][0] == d])) for d in range(2)]
        yield
        w_log, a = [], []
        for d in range(2):
            xl_d = stack([xl[s] for s in ids if streams[s][0] == d])
            wl = -_softplus(-(w0_ref[d] + _dot(bf(th[d]), bf(wup_ref[d])))) - 0.5
            al = jax.nn.sigmoid(a0_ref[d] + _dot(bf(xl_d), bf(aup_ref[d])))
            w_log += [rows_of(wl, k) for k in range(batch)]
            a += [rows_of(al, k) for k in range(batch)]
        k = [xr[s][:, BRANCH_W:2 * BRANCH_W] for s in ids]
        kk = [k[s] * kk_ref[streams[s][0]] for s in ids]
        yield
        kk_sq = _dot(bf(stack([kk[s] * kk[s] for s in ids])), bd1)
        yield
        for s in ids:
            kks = kk[s] * lax.rsqrt(rows_of(kk_sq, s) + 1e-12)
            r_s[slot, s] = xr[s][:, 0:BRANCH_W]
            k_s[slot, s] = k[s] * (1.0 + (a[s] - 1.0) * ka_ref[streams[s][0]])
            v_s[slot, s] = xr[s][:, 2 * BRANCH_W:3 * BRANCH_W]
            kk_s[slot, s] = kks
            b_s[slot, s] = kks * a[s]
            lw_s[slot, s] = -jnp.exp(w_log[s])

    def advance(slot):
        ids = wid
        back = iback
        part = lambda ref: [ref[slot, s, h * c:(h + 1) * c, :] for s, h in items]
        lw, r, k2, v, kk, bq = part(lw_s), part(r_s), part(k_s), part(v_s), part(kk_s), part(b_s)

        def cumsum(s):
            lw_hi, lw_lo = _split_bf16(lw[s])
            tri = tri_b if back[s] else tri_f
            return _dot(tri, lw_hi) + _dot(tri, lw_lo)

        cumi = [cumsum(s) for s in ids]
        yield
        tot = [cumi[s][0:1, :] if back[s] else cumi[s][c - 1:c, :] for s in ids]
        e_m = [jnp.exp(-cumi[s]) for s in ids]
        e_t = [jnp.exp(tot[s] - cumi[s]) for s in ids]
        pt = [bf(-kk[s] * jnp.exp(cumi[s] - lw[s])) for s in ids]
        rt = [r[s] * jnp.exp(cumi[s]) for s in ids]
        pr = [stack([pt[s], bf(rt[s])]) for s in ids]
        ab = [_nt(pr[s], sm(bf(bq[s] * e_m[s]))) for s in ids]
        ak = [_nt(pr[s], sm(bf(k2[s] * e_m[s]))) for s in ids]
        yield
        a_pb = [jnp.where(strict[s], top(ab[s]), 0.0) for s in ids]
        a_rb = [bf(jnp.where(incl[s], mid(ab[s]), 0.0)) for s in ids]
        a_pk = [bf(jnp.where(strict[s], top(ak[s]), 0.0)) for s in ids]
        a_rk = [bf(jnp.where(incl[s], mid(ak[s]), 0.0)) for s in ids]
        l_b = [bf(_nt(eye_b, sm(bf(bq[s] * e_t[s])))) for s in ids]
        l_k = [bf(_nt(eye_b, sm(bf(k2[s] * e_t[s])))) for s in ids]
        yield

        tinv = [jnp.where(eye, 1.0, a_pb[s]) for s in ids]
        apow = [bf(a_pb[s]) for s in ids]
        a2 = [_dot(apow[s], sm(apow[s])) for s in ids]
        av = [_dot(stack([a_pk[s], a_rk[s], l_k[s]]), sm(bf(v[s]))) for s in ids]
        yield
        for _ in range(NEUMANN_DOUBLINGS - 1):
            apow = [bf(a2[s]) for s in ids]
            both = [_dot(stack([bf(tinv[s]), apow[s]]), sm(apow[s])) for s in ids]
            yield
            tinv = [tinv[s] + top(both[s]) for s in ids]
            a2 = [mid(both[s]) for s in ids]
        tinv = [tinv[s] + _dot(bf(tinv[s]), sm(bf(a2[s]))) for s in ids]
        yield
        tb = [bf(t) for t in tinv]

        ah = [_dot(tb[s], sm(pt[s])) for s in ids]
        uh = [_dot(tb[s], sm(bf(top(av[s])))) for s in ids]
        yield
        lhs = [stack([a_rb[s], l_b[s]]) for s in ids]
        o_a = [_dot(lhs[s], sm(bf(ah[s]))) for s in ids]
        o_u = [_dot(lhs[s], sm(bf(uh[s]))) for s in ids]
        yield
        rh = [rt[s] + top(o_a[s]) for s in ids]
        mt = [jnp.where(eye, jnp.exp(tot[s]), 0.0) + mid(o_a[s]) for s in ids]
        oh = [mid(av[s]) + top(o_u[s]) for s in ids]
        gt = [mid(o_u[s]) + av[s][2 * c:3 * c] for s in ids]

        bonus = _dot(bf(stack([r[w] * k2[w] * rk_ref[streams[items[w][0]][0]] for w in ids])), bd1) * stack(v)
        for w, (s, h) in enumerate(items):
            y_s[slot, s, h * c:(h + 1) * c, :] = bonus[w * c:(w + 1) * c]
        state = [st_ref[s] for s in range(len(streams))]
        for step in range(unit // c):
            now = [w for w in ids if w % (unit // c) == step]
            out = {w: _dot(stack([bf(rh[w]), bf(mt[w])]), sm(bf(state[items[w][0]]))) for w in now}
            yield
            for w in now:
                s, h = items[w]
                state[s] = mid(out[w]) + gt[w]
                o_s[slot, s, h * c:(h + 1) * c, :] = top(out[w]) + oh[w]
        for s in range(len(streams)):
            st_ref[s] = state[s]

    def finish(j, slot):
        _, rows = unit_rows(j)
        o = stack([o_s[slot, s] for s in ids])
        mean = _dot_split_lhs(o, bdm)
        yield
        xc = o - mean
        var = _dot(bf(xc * xc), bdm)
        yield
        y = xc * lax.rsqrt(var + GN_EPS) * lng_ref[...] + lnb_ref[...] + stack([y_s[slot, s] for s in ids])
        for s, (d, b) in enumerate(streams):
            out_ref = outb_ref if d == 1 else outf_ref
            out_ref[b, rows[s], :] = rows_of(y, s)

    def interleave(*phases):
        live = list(phases)
        while live:
            for phase in list(live):
                if next(phase, "done") == "done":
                    live.remove(phase)

    o_s[1] = jnp.zeros(o_s.shape[1:], F32)
    y_s[1] = jnp.zeros(y_s.shape[1:], F32)
    interleave(prepare(0, 0))

    def pipeline(j, carry):
        interleave(finish(jnp.maximum(j - 1, 0), (j + 1) % 2),
                   advance(j % 2),
                   prepare(jnp.minimum(j + 1, n_units - 1), (j + 1) % 2))
        return carry

    lax.fori_loop(0, n_units, pipeline, 0)
    interleave(finish(n_units - 1, (n_units - 1) % 2))


def _rwkv(rkv, lat, lp, batch, seq):
    tc = min(RWKV_ROWS, seq)
    nt = seq // tc
    hb = tc // 8
    last_halo = seq // 8 - 1
    fwd = lambda i: i
    bwd = lambda i: nt - 1 - i
    fwd_halo = lambda i: jnp.maximum(i * hb - 1, 0)
    bwd_halo = lambda i: jnp.minimum((bwd(i) + 1) * hb, last_halo)
    full3 = lambda shape: pl.BlockSpec(shape, lambda i: (0, 0, 0))
    full2 = lambda shape: pl.BlockSpec(shape, lambda i: (0, 0))
    n_streams = 2 * batch
    out_shape = jax.ShapeDtypeStruct((batch, seq, BRANCH_W), F32)
    return pl.pallas_call(
        functools.partial(_rwkv_kernel, batch=batch, tc=tc),
        grid=(nt,),
        in_specs=[
            pl.BlockSpec((batch, tc, RKV_W), lambda i: (0, fwd(i), 0)),
            pl.BlockSpec((batch, 8, RKV_W), lambda i: (0, fwd_halo(i), 0)),
            pl.BlockSpec((batch, tc, RKV_W), lambda i: (0, bwd(i), 0)),
            pl.BlockSpec((batch, 8, RKV_W), lambda i: (0, bwd_halo(i), 0)),
            pl.BlockSpec((batch, tc, LAT_W), lambda i: (0, fwd(i), 0)),
            pl.BlockSpec((batch, 8, LAT_W), lambda i: (0, fwd_halo(i), 0)),
            pl.BlockSpec((batch, tc, LAT_W), lambda i: (0, bwd(i), 1)),
            pl.BlockSpec((batch, 8, LAT_W), lambda i: (0, bwd_halo(i), 1)),
            full3((2, 1, RKV_W)), full3((2, 1, LAT_W)), full3((2, 1, BRANCH_W)),
            full3((2, LAT_W, BRANCH_W)), full3((2, 1, BRANCH_W)), full3((2, LAT_W, BRANCH_W)),
            full3((2, 1, BRANCH_W)), full3((2, 1, BRANCH_W)), full3((2, 1, BRANCH_W)),
            full2((1, BRANCH_W)), full2((1, BRANCH_W)),
            full2((BRANCH_W, BRANCH_W)), full2((BRANCH_W, BRANCH_W)),
        ],
        out_specs=[pl.BlockSpec((batch, tc, BRANCH_W), lambda i: (0, fwd(i), 0)),
                   pl.BlockSpec((batch, tc, BRANCH_W), lambda i: (0, bwd(i), 0))],
        out_shape=[out_shape, out_shape],
        scratch_shapes=[pltpu.VMEM((n_streams, RWKV_CHUNK, BRANCH_W), F32)]
        + [pltpu.VMEM((2, n_streams, RWKV_UNIT, BRANCH_W), F32)] * 8,
        compiler_params=_params("arbitrary"),
        name="rwkv",
    )(rkv, rkv, rkv, rkv, lat, lat, lat, lat, lp["mu_rkv"], lp["mu_lat"], lp["w0"], lp["w_up_pad"],
      lp["a0"], lp["a_up_pad"], lp["k_k"], lp["k_a"], lp["r_k"], lp["ln_g"], lp["ln_b"],
      lp["bd_mean"], lp["bd_ones"])


def _silu(z):
    return z * jax.nn.sigmoid(z)


def _merge_kernel(x_ref, g_ref, o0_ref, l0_ref, o1_ref, l1_ref, o2_ref, l2_ref, z_ref,
                  ubp_ref, bc_ref, ubn_ref, yd0_ref, yd1_ref,
                  unperm_ref, poolw_ref, pools_ref, sgg_ref, sgw_ref, sgb_ref, wg_ref, wb_ref, wo_ref,
                  out_ref, *, tm, seq):
    i = pl.program_id(1)
    n_tiles = seq // tm
    x = x_ref[0]
    ms = jnp.mean(x * x, axis=-1, keepdims=True)
    h = (x * lax.rsqrt(ms + EPS) * g_ref[...]).astype(BF16)
    z = z_ref[0]
    lane_g = _lane_head(tm)
    parts = [slice(p * PERM_ROWS, (p + 1) * PERM_ROWS) for p in range(tm // PERM_ROWS)]
    gate_logits = [[_dot(h[p], wg_ref[:, n * D_MODEL:(n + 1) * D_MODEL]) for n in range(N_BRANCH)]
                   for p in parts]

    def natural(ref, group, exact):
        tiles = []
        for t in range(tm // PERM_ROWS):
            if group == 0:
                tiles.append(ref[0, t, 0].astype(F32))
                continue
            v = jnp.concatenate([ref[0, t, r] for r in range(ATTN_DILATIONS[group])], axis=0)
            unperm = unperm_ref[group - 1]
            tiles.append(_dot_split_lhs_rhs(unperm, v) if exact else _dot(unperm, v))
        return jnp.concatenate(tiles, axis=0)

    l0, l1, l2 = natural(l0_ref, 0, True), natural(l1_ref, 1, True), natural(l2_ref, 2, True)
    mx = jnp.maximum(jnp.maximum(l0, l1), l2)
    e0, e1, e2 = jnp.exp(l0 - mx), jnp.exp(l1 - mx), jnp.exp(l2 - mx)
    y_a = (e0 * natural(o0_ref, 0, False) + e1 * natural(o1_ref, 1, False)
           + e2 * natural(o2_ref, 2, False)) / (e0 + e1 + e2)

    bc = bc_ref[0]
    u = bc[:, 0:BRANCH_W]
    prev = jnp.where(i == 0, 0.0, ubp_ref[0][:, 0:BRANCH_W])
    nxt = jnp.where(i == n_tiles - 1, 0.0, ubn_ref[0][:, 0:BRANCH_W])
    ext = jnp.concatenate([prev, u, nxt], axis=0)
    n_ext = tm + 2 * POOL_HALO
    f1 = ext + pltpu.roll(ext, 1, 0)
    f2 = pltpu.roll(f1, 1, 0) + pltpu.roll(f1, n_ext - 1, 0)
    f4 = pltpu.roll(f2, 2, 0) + pltpu.roll(f2, n_ext - 2, 0)
    f8 = pltpu.roll(f4, 4, 0) + pltpu.roll(f4, n_ext - 4, 0)
    lane_e = _lane_head(n_ext)
    win = jnp.where(lane_e == 0, f1, jnp.where(lane_e == 1, f2, jnp.where(lane_e == 2, f4, f8)))
    win = win[POOL_HALO:POOL_HALO + tm, :]
    pos = i * tm + lax.broadcasted_iota(jnp.int32, (tm, BRANCH_W), 0)
    half = jnp.where(lane_g == 0, POOL_HALF[0],
                     jnp.where(lane_g == 1, POOL_HALF[1], jnp.where(lane_g == 2, POOL_HALF[2], POOL_HALF[3])))
    cnt = (jnp.minimum(pos + half, seq) - jnp.maximum(pos - half, 0)).astype(F32)
    y_b = _dot((win / cnt - u).astype(BF16), poolw_ref[...]) * pools_ref[...]

    u_c = bc[:, BRANCH_W:2 * BRANCH_W]
    v_c = bc[:, 2 * BRANCH_W:3 * BRANCH_W]
    msv = jnp.mean(v_c * v_c, axis=-1, keepdims=True)
    v_n = (v_c * lax.rsqrt(msv + EPS) * sgg_ref[...]).astype(BF16)
    lane_c = _lane_head(CHUNK)
    sv_chunks = []
    for ch in range(tm // CHUNK):
        vch = v_n[ch * CHUNK:(ch + 1) * CHUNK, :]
        v_groups = jnp.concatenate([jnp.where(lane_c == g, vch, jnp.zeros_like(vch)) for g in range(SG_GROUPS)],
                                   axis=0)
        sv_chunks.append(_dot(sgw_ref[...], v_groups) + sgb_ref[...])
    y_c = u_c * jnp.concatenate(sv_chunks, axis=0)

    y_d = yd0_ref[0] + yd1_ref[0]

    ys = [(y * _silu(z[:, n * BRANCH_W:(n + 1) * BRANCH_W])).astype(BF16)
          for n, y in enumerate((y_a, y_b, y_c, y_d))]
    branch = [[_dot(ys[n][p], wb_ref[n]) for n in range(N_BRANCH)] for p in parts]
    merged = []
    for gl, br in zip(gate_logits, branch):
        acc = jax.nn.sigmoid(gl[0]) * br[0]
        for n in range(1, N_BRANCH):
            acc = acc + jax.nn.sigmoid(gl[n]) * br[n]
        merged.append(acc.astype(BF16))
    for p, m in zip(parts, merged):
        out_ref[0, p, :] = x[p] + _dot(m, wo_ref[...])


def _merge(x, attn, z, bc, yd0, yd1, lp, batch, seq):
    tm = MERGE_ROWS
    hb = tm // POOL_HALO
    last_halo = seq // POOL_HALO - 1
    row = lambda w: pl.BlockSpec((1, tm, w), lambda b, i: (b, i, 0))
    const2 = lambda shape: pl.BlockSpec(shape, lambda b, i: (0, 0))
    const3 = lambda shape: pl.BlockSpec(shape, lambda b, i: (0, 0, 0))
    assert tm % PERM_ROWS == 0
    in_specs = [row(D_MODEL), const2((1, D_MODEL))]
    for dil in ATTN_DILATIONS:
        slab = pl.BlockSpec((1, tm // PERM_ROWS, dil, PERM_ROWS // dil, BRANCH_W), lambda b, i: (b, i, 0, 0, 0))
        in_specs += [slab, slab]
    in_specs += [
        row(Z_W),
        pl.BlockSpec((1, POOL_HALO, BRANCH_W), lambda b, i: (b, jnp.maximum(i * hb - 1, 0), 0)),
        row(BC_W),
        pl.BlockSpec((1, POOL_HALO, BRANCH_W), lambda b, i: (b, jnp.minimum((i + 1) * hb, last_halo), 0)),
        row(BRANCH_W), row(BRANCH_W),
        const3((N_GROUPS - 1, PERM_ROWS, PERM_ROWS)),
        const2((BRANCH_W, BRANCH_W)), const2((1, BRANCH_W)), const2((1, BRANCH_W)),
        const2((CHUNK, SG_GROUPS * CHUNK)), const2((CHUNK, BRANCH_W)),
        const2((D_MODEL, N_BRANCH * D_MODEL)), const3((N_BRANCH, BRANCH_W, D_MODEL)),
        const2((D_MODEL, D_MODEL)),
    ]
    flat_attn = [a for pair in attn for a in pair]
    return pl.pallas_call(
        functools.partial(_merge_kernel, tm=tm, seq=seq),
        grid=(batch, seq // tm),
        in_specs=in_specs,
        out_specs=row(D_MODEL),
        out_shape=jax.ShapeDtypeStruct((batch, seq, D_MODEL), F32),
        compiler_params=_params("parallel", "parallel"),
        name="merge",
    )(x, lp["norm_g"], *flat_attn, z, bc, bc, bc, yd0, yd1, _residue_perms(True),
      lp["pool_w_bd"], lp["pool_scale"], lp["sg_norm_g"], lp["sg_w"], lp["sg_bias"],
      lp["w_gate"], lp["w_branch"], lp["w_out"])


def _block_diag(blocks):
    n = blocks.shape[0]
    w = blocks.shape[1]
    out = jnp.zeros((n * w, n * w), blocks.dtype)
    for g in range(n):
        out = out.at[g * w:(g + 1) * w, g * w:(g + 1) * w].set(blocks[g])
    return out


def _layer_params(l, norm_g, w_in, q_norm_g, k_norm_g, pool_w, pool_scale, sg_norm_g, sg_w, sg_b,
                  mu_rkv, mu_lat, w0, w_up, a0, a_up, k_k, k_a, r_k, ln_g, ln_b, w_branch, w_out):
    w = w_in[l]
    seg = lambda lo, width: w[:, lo:lo + width]
    w_small = jnp.concatenate([
        seg(0, QKV_W), seg(3840, RKV_W), seg(4608, 2 * LAT_W), seg(2560, BRANCH_W), seg(3072, 2 * BRANCH_W),
        seg(2304, BRANCH_W), seg(2816, BRANCH_W), seg(3584, BRANCH_W), seg(4864, BRANCH_W)], axis=1)
    ones_blocks = jnp.ones((RWKV_HEADS, HEAD_DIM, HEAD_DIM), F32)
    zeros_lat = jnp.zeros((2, LAT_W // 2, BRANCH_W), F32)
    vec3 = lambda p: p[l][:, None, :]
    return {
        "norm_g": norm_g[l][None, :],
        "w_small": w_small.astype(BF16),
        "w_gate": w[:, PROJ_SMALL_W:].astype(BF16),
        "q_gain": jnp.tile(q_norm_g[l], HEADS_PER_GROUP)[None, :],
        "k_gain": jnp.tile(k_norm_g[l], HEADS_PER_GROUP)[None, :],
        "bd_mean": (_block_diag(ones_blocks) / HEAD_DIM).astype(BF16),
        "bd_ones": _block_diag(ones_blocks).astype(BF16),
        "pool_w_bd": _block_diag(pool_w[l]).astype(BF16),
        "pool_scale": pool_scale[l][None, :],
        "sg_norm_g": sg_norm_g[l][None, :],
        "sg_w": jnp.concatenate([sg_w[l][g] for g in range(SG_GROUPS)], axis=1).astype(BF16),
        "sg_bias": jnp.repeat(sg_b[l].T, HEAD_DIM, axis=1),
        "mu_rkv": vec3(mu_rkv), "mu_lat": vec3(mu_lat), "w0": vec3(w0), "a0": vec3(a0),
        "k_k": vec3(k_k), "k_a": vec3(k_a), "r_k": vec3(r_k),
        "w_up_pad": jnp.concatenate([w_up[l], zeros_lat], axis=1),
        "a_up_pad": jnp.concatenate([zeros_lat, a_up[l]], axis=1),
        "ln_g": ln_g[l][None, :], "ln_b": ln_b[l][None, :],
        "w_branch": w_branch[l].astype(BF16),
        "w_out": w_out[l].astype(BF16),
    }


def _layer(x, lp):
    batch, seq, _ = x.shape
    qkv, rkv, lat, bc, z = _proj(x.reshape(batch * seq, D_MODEL), lp)
    attn = [_attn_group(qkv, g, batch, seq) for g in range(N_GROUPS)]
    rkv = rkv.reshape(batch, seq, RKV_W)
    lat = lat.reshape(batch, seq, 2 * LAT_W)
    yd0, yd1 = _rwkv(rkv, lat, lp, batch, seq)
    return _merge(x, attn, z.reshape(batch, seq, Z_W), bc.reshape(batch, seq, BC_W), yd0, yd1, lp, batch, seq)


def kernel(x_prompt, x_sample, norm_g, w_in, q_norm_g, k_norm_g, pool_w, pool_scale, sg_norm_g, sg_w, sg_b,
           mu_rkv, mu_lat, w0, w_up, a0, a_up, k_k, k_a, r_k, ln_g, ln_b, w_branch, w_out):
    weights = (norm_g, w_in, q_norm_g, k_norm_g, pool_w, pool_scale, sg_norm_g, sg_w, sg_b,
               mu_rkv, mu_lat, w0, w_up, a0, a_up, k_k, k_a, r_k, ln_g, ln_b, w_branch, w_out)
    layers = [_layer_params(l, *weights) for l in range(norm_g.shape[0])]
    outs = []
    for x in (x_prompt, x_sample):
        for lp in layers:
            x = _layer(x, lp)
        outs.append(x)
    return tuple(outs)
```

```python
import functools

import jax
import jax.numpy as jnp
import numpy as np
from jax import lax
from jax.experimental import pallas as pl
from jax.experimental.pallas import tpu as pltpu

F32 = jnp.float32
BF16 = jnp.bfloat16

D_MODEL = 1024
BRANCH_W = 256
N_BRANCH = 4
EPS = 1e-6
NEG_INF = -1e30

HEAD_DIM = 64
ATTN_DILATIONS = (1, 4, 16)
N_GROUPS = 3
HEADS_PER_GROUP = 4
N_ATTN_HEADS = N_GROUPS * HEADS_PER_GROUP
ATTN_QKV = N_ATTN_HEADS * HEAD_DIM
HALF_KEYS = 64
Q_SUB = 128
K_WIN = Q_SUB + 2 * HALF_KEYS

POOL_HALF = (1, 2, 4, 8)
POOL_HALO = 8
CHUNK = 128
SG_GROUPS = 4

RWKV_HEADS = 4
RWKV_N = 64
RWKV_CHUNK = 64
RWKV_UNIT = 128
LAT_W = 128
GN_EPS = 64e-5
NEUMANN_DOUBLINGS = 5

QKV_W = 3 * ATTN_QKV
RKV_W = 3 * BRANCH_W
BC_W = 3 * BRANCH_W
Z_W = 4 * BRANCH_W
PROJ_SMALL_W = QKV_W + RKV_W + 2 * LAT_W + BC_W + Z_W

PROJ_ROWS = 512
MERGE_ROWS = 512
PERM_ROWS = 256
RWKV_ROWS = 512
VMEM_LIMIT = 62 * 1024 * 1024


def _params(*sem):
    return pltpu.CompilerParams(dimension_semantics=sem, vmem_limit_bytes=VMEM_LIMIT)


def _dot(a, b):
    return jnp.dot(a, b, preferred_element_type=F32)


def _split_bf16(x):
    hi = x.astype(BF16)
    lo = (x - hi.astype(F32)).astype(BF16)
    return hi, lo


def _dot_split_lhs(x, m_bf16):
    hi, lo = _split_bf16(x)
    return _dot(hi, m_bf16) + _dot(lo, m_bf16)


def _dot_split_lhs_rhs(m_bf16, x):
    hi, lo = _split_bf16(x)
    return _dot(m_bf16, hi) + _dot(m_bf16, lo)


def _dot_split_both(a, b):
    a_hi, a_lo = _split_bf16(a)
    b_hi, b_lo = _split_bf16(b)
    return _dot(a_hi, b_hi) + _dot(a_lo, b_hi) + _dot(a_hi, b_lo)


def _lane_head(rows):
    return lax.broadcasted_iota(jnp.int32, (rows, BRANCH_W), 1) // HEAD_DIM


def _proj_kernel(x_ref, g_ref, w_ref, qg_ref, kg_ref, perm_ref, qkv_ref, rkv_ref, lat_ref, bc_ref, z_ref):
    x = x_ref[...]
    ms = jnp.mean(x * x, axis=-1, keepdims=True)
    h = (x * lax.rsqrt(ms + EPS) * g_ref[...]).astype(BF16)
    n_chunks = 3 * N_GROUPS

    def project(c):
        return _dot(h, w_ref[:, c * BRANCH_W:(c + 1) * BRANCH_W])

    lane_h = _lane_head(x.shape[0])

    def normalize(c, y):
        if c < 2 * N_GROUPS:
            sq = y * y
            msq = jnp.zeros_like(sq)
            for hd in range(HEADS_PER_GROUP):
                mine = lane_h == hd
                tot = jnp.sum(jnp.where(mine, sq, 0.0), axis=-1, keepdims=True)
                msq = jnp.where(mine, tot * (1.0 / HEAD_DIM), msq)
            if c < N_GROUPS:
                y = y * lax.rsqrt(msq + EPS) * qg_ref[...] * (HEAD_DIM ** -0.5)
            else:
                y = y * lax.rsqrt(msq + EPS) * kg_ref[...]
        return y.astype(BF16)

    def store(c, yb):
        group = c % N_GROUPS
        if group > 0:
            perm = perm_ref[group - 1]
            yb = jnp.concatenate([_dot(perm, yb[t * PERM_ROWS:(t + 1) * PERM_ROWS]).astype(BF16)
                                  for t in range(yb.shape[0] // PERM_ROWS)], axis=0)
        qkv_ref[:, c * BRANCH_W:(c + 1) * BRANCH_W] = yb

    others = []
    o = QKV_W
    for ref, width in ((rkv_ref, RKV_W), (lat_ref, 2 * LAT_W), (bc_ref, BC_W), (z_ref, Z_W)):
        others.append((ref, o, width))
        o += width

    projected, normalized = {}, {}
    for step in range(n_chunks + 2):
        if step < n_chunks:
            projected[step] = project(step)
        elif others:
            ref, lo, width = others.pop(0)
            ref[...] = _dot(h, w_ref[:, lo:lo + width])
        if 1 <= step <= n_chunks:
            normalized[step - 1] = normalize(step - 1, projected.pop(step - 1))
        if step >= 2:
            store(step - 2, normalized.pop(step - 2))
    for ref, lo, width in others:
        ref[...] = _dot(h, w_ref[:, lo:lo + width])


def _residue_perms(transpose):
    mats = []
    for dil in ATTN_DILATIONS[1:]:
        per = PERM_ROWS // dil
        p = np.zeros((PERM_ROWS, PERM_ROWS), np.float32)
        old = np.arange(PERM_ROWS)
        p[(old % dil) * per + old // dil, old] = 1.0
        mats.append(p.T if transpose else p)
    return jnp.asarray(np.stack(mats), BF16)


def _proj(x2d, lp):
    rows = x2d.shape[0]
    tm = PROJ_ROWS
    assert tm % PERM_ROWS == 0
    const = lambda i: (0, 0)
    row = lambda i: (i, 0)
    widths = (QKV_W, RKV_W, 2 * LAT_W, BC_W, Z_W)
    dtypes = (BF16, F32, F32, F32, F32)
    return pl.pallas_call(
        _proj_kernel,
        grid=(rows // tm,),
        in_specs=[
            pl.BlockSpec((tm, D_MODEL), row),
            pl.BlockSpec((1, D_MODEL), const),
            pl.BlockSpec((D_MODEL, PROJ_SMALL_W), const),
            pl.BlockSpec((1, BRANCH_W), const),
            pl.BlockSpec((1, BRANCH_W), const),
            pl.BlockSpec((N_GROUPS - 1, PERM_ROWS, PERM_ROWS), lambda i: (0, 0, 0)),
        ],
        out_specs=[pl.BlockSpec((tm, w), row) for w in widths],
        out_shape=[jax.ShapeDtypeStruct((rows, w), dt) for w, dt in zip(widths, dtypes)],
        compiler_params=_params("parallel"),
        name="proj",
    )(x2d, lp["norm_g"], lp["w_small"], lp["q_gain"], lp["k_gain"], _residue_perms(False))


class _AttnPlan:
    def __init__(self, group, batch, seq, n_steps):
        self.group = group
        self.dil = ATTN_DILATIONS[group]
        self.seq_m = seq // self.dil
        self.per = PERM_ROWS // self.dil
        rows = batch * seq // n_steps
        self.res = max(1, rows // self.seq_m)
        self.tq = rows // self.res
        self.nq = self.seq_m // self.tq
        assert self.dil % self.res == 0 and batch * self.dil * self.nq == n_steps * self.res

    def decode(self, i):
        t = i * self.res
        return t // (self.dil * self.nq), ((t // self.nq) % self.dil) // self.res, t % self.nq

    def specs(self):
        per, res, tq = self.per, self.res, self.tq
        hb = tq // HALF_KEYS
        last_halo = self.seq_m // HALF_KEYS - 1
        prev = lambda iq: jnp.maximum(iq * hb - 1, 0)
        nxt = lambda iq: jnp.minimum((iq + 1) * hb, last_halo)

        def col(which):
            return which * N_GROUPS + self.group

        def main(which):
            def index(i):
                b, rb, iq = self.decode(i)
                return b, iq, rb, 0, col(which)
            return pl.BlockSpec((1, tq // per, res, per, BRANCH_W), index)

        def halo(which, pick):
            split = max(1, per // HALF_KEYS)

            def index(i):
                b, rb, iq = self.decode(i)
                h = pick(iq)
                return b, h // split, rb, h % split, col(which)
            if split == 1:
                return pl.BlockSpec((1, HALF_KEYS // per, res, per, BRANCH_W), index)
            return pl.BlockSpec((1, 1, res, HALF_KEYS, BRANCH_W), index)

        def out_index(i):
            b, rb, iq = self.decode(i)
            return b, iq, rb, 0, 0

        ins = [main(0), halo(1, prev), main(1), halo(1, nxt), halo(2, prev), main(2), halo(2, nxt),
               pl.BlockSpec((HEADS_PER_GROUP * Q_SUB, K_WIN), lambda i: (0, 0))]
        outs = [pl.BlockSpec((1, tq // per, res, per, BRANCH_W), out_index)] * 2
        return ins, outs

    def scratch(self):
        buf = lambda rows: pltpu.VMEM((self.res, rows, BRANCH_W), BF16)
        return [buf(self.tq + 2 * HALF_KEYS), buf(self.tq + 2 * HALF_KEYS)]


def _attn_gather(plan, refs, bufs):
    _, kp_ref, k_ref, kn_ref, vp_ref, v_ref, vn_ref = refs
    kbuf, vbuf = bufs

    def gather(dst, t, at, src):
        per = src.shape[3]
        for j in range(src.shape[1]):
            dst[t, at + j * per:at + (j + 1) * per, :] = src[0, j, t]

    for t in range(plan.res):
        gather(kbuf, t, 0, kp_ref)
        gather(kbuf, t, HALF_KEYS, k_ref)
        gather(kbuf, t, HALF_KEYS + plan.tq, kn_ref)
        gather(vbuf, t, 0, vp_ref)
        gather(vbuf, t, HALF_KEYS, v_ref)
        gather(vbuf, t, HALF_KEYS + plan.tq, vn_ref)


def _attn_blocks(plan, i, sub_blocks, q_ref, bufs, bias_ref, o_ref, lse_ref):
    kbuf, vbuf = bufs
    per_out = o_ref.shape[3]
    lane_h = _lane_head(Q_SUB)
    col = lax.broadcasted_iota(jnp.int32, (1, K_WIN), 1)
    _, _, iq = plan.decode(i)

    def scores(t, r0):
        if per_out >= Q_SUB:
            qb = q_ref[0, r0 // per_out, t, pl.ds(pl.multiple_of(r0 % per_out, Q_SUB), Q_SUB), :]
        else:
            qb = jnp.concatenate([q_ref[0, r0 // per_out + j, t] for j in range(Q_SUB // per_out)], axis=0)
        qs = jnp.concatenate(
            [jnp.where(lane_h == h, qb, jnp.zeros_like(qb)) for h in range(HEADS_PER_GROUP)], axis=0)
        return lax.dot_general(qs, kbuf[t, pl.ds(r0, K_WIN), :], (((1,), (1,)), ((), ())),
                               preferred_element_type=F32)

    def weigh(t, r0, s):
        kpos = iq * plan.tq + (r0 - HALF_KEYS) + col
        s = s + bias_ref[...] + jnp.where((kpos >= 0) & (kpos < plan.seq_m), 0.0, NEG_INF)
        m = jnp.max(s, axis=-1, keepdims=True)
        p = jnp.exp(s - m)
        l = jnp.sum(p, axis=-1, keepdims=True)
        return _dot(p.astype(BF16), vbuf[t, pl.ds(r0, K_WIN), :]), l, m

    def store(t, r0, pv, l, m):
        on = pv / l
        ls = m + jnp.log(l)
        o = jnp.zeros((Q_SUB, BRANCH_W), F32)
        lo = jnp.zeros((Q_SUB, BRANCH_W), F32)
        for h in range(HEADS_PER_GROUP):
            sel = lane_h == h
            o = jnp.where(sel, on[h * Q_SUB:(h + 1) * Q_SUB, :], o)
            lo = jnp.where(sel, ls[h * Q_SUB:(h + 1) * Q_SUB, :], lo)
        if per_out >= Q_SUB:
            at = pl.ds(pl.multiple_of(r0 % per_out, Q_SUB), Q_SUB)
            o_ref[0, r0 // per_out, t, at, :] = o.astype(o_ref.dtype)
            lse_ref[0, r0 // per_out, t, at, :] = lo
        for j in range(Q_SUB // per_out):
            o_ref[0, r0 // per_out + j, t] = o[j * per_out:(j + 1) * per_out, :].astype(o_ref.dtype)
            lse_ref[0, r0 // per_out + j, t] = lo[j * per_out:(j + 1) * per_out, :]

    scored = weighed = None
    for step in range(len(sub_blocks) + 2):
        if weighed is not None:
            store(*weighed)
        weighed = None if scored is None else (scored[0], scored[1]) + weigh(*scored)
        scored = sub_blocks[step] + (scores(*sub_blocks[step]),) if step < len(sub_blocks) else None
        if step < len(sub_blocks) + 1:
            yield


def _attn_bias(group):
    dil = ATTN_DILATIONS[group]
    i = np.arange(Q_SUB)[:, None]
    j = np.arange(K_WIN)[None, :]
    off = np.abs(j - HALF_KEYS - i).astype(np.float32)
    slopes = 2.0 ** (-8.0 * np.arange(1, N_ATTN_HEADS + 1, dtype=np.float32) / N_ATTN_HEADS)
    blocks = []
    for h in range(HEADS_PER_GROUP):
        slope = slopes[group * HEADS_PER_GROUP + h]
        blocks.append(np.where(off <= HALF_KEYS, -slope * (off * dil), NEG_INF))
    return jnp.asarray(np.concatenate(blocks, axis=0), F32)


def _softplus(y):
    return jnp.maximum(y, 0.0) + jnp.log(1.0 + jnp.exp(-jnp.abs(y)))


def _stack_heads(xb, lane_h):
    return jnp.concatenate(
        [jnp.where(lane_h == h, xb, jnp.zeros_like(xb)) for h in range(RWKV_HEADS)], axis=0)


def _nt(a, b):
    return lax.dot_general(a, b, (((1,), (1,)), ((), ())), preferred_element_type=F32)


N_RWKV_IN = 21
N_ATTN_IN = 8


def _mixer_kernel(*refs, batch, tc, plans):
    n_in = N_RWKV_IN + N_ATTN_IN * len(plans)
    n_out = 2 + 2 * len(plans)
    (rkvf_ref, rkvfh_ref, rkvb_ref, rkvbh_ref, latf_ref, latfh_ref, latb_ref, latbh_ref,
     mur_ref, mul_ref, w0_ref, wup_ref, a0_ref, aup_ref, kk_ref, ka_ref, rk_ref,
     lng_ref, lnb_ref, bdm_ref, bd1_ref) = refs[:N_RWKV_IN]
    attn_in = [refs[N_RWKV_IN + N_ATTN_IN * g:N_RWKV_IN + N_ATTN_IN * (g + 1)] for g in range(len(plans))]
    outf_ref, outb_ref = refs[n_in:n_in + 2]
    attn_out = [refs[n_in + 2 + 2 * g:n_in + 4 + 2 * g] for g in range(len(plans))]
    st_ref, r_s, k_s, v_s, kk_s, b_s, lw_s, o_s, y_s = refs[n_in + n_out:n_in + n_out + 9]
    attn_buf = [refs[n_in + n_out + 9 + 2 * g:n_in + n_out + 11 + 2 * g] for g in range(len(plans))]

    i = pl.program_id(0)
    c = RWKV_CHUNK
    unit = RWKV_UNIT
    n_units = tc // unit
    streams = [(d, b) for d in range(2) for b in range(batch)]
    ids = range(len(streams))
    back = [d == 1 for d, _ in streams]
    items = [(s, h) for s in ids for h in (reversed(range(unit // c)) if back[s] else range(unit // c))]
    wid = range(len(items))
    iback = [back[s] for s, _ in items]

    @pl.when(i == 0)
    def _():
        st_ref[...] = jnp.zeros_like(st_ref)

    lane_h = _lane_head(c)
    t_i = lax.broadcasted_iota(jnp.int32, (c, c), 0)
    s_i = lax.broadcasted_iota(jnp.int32, (c, c), 1)
    tau = lax.broadcasted_iota(jnp.int32, (c, BRANCH_W), 0)
    sig = lax.broadcasted_iota(jnp.int32, (c, BRANCH_W), 1) % c
    eye = sig == tau
    eye_b = jnp.where(eye, 1.0, 0.0).astype(BF16)
    tri_f = jnp.where(s_i <= t_i, 1.0, 0.0).astype(BF16)
    tri_b = jnp.where(s_i >= t_i, 1.0, 0.0).astype(BF16)
    strict = [(sig > tau) if bk else (sig < tau) for bk in iback]
    incl = [(sig >= tau) if bk else (sig <= tau) for bk in iback]
    bd1 = bd1_ref[...]
    bdm = bdm_ref[...]
    sm = lambda xb: _stack_heads(xb, lane_h)
    bf = lambda x: x.astype(BF16)
    top = lambda x: x[0:c]
    mid = lambda x: x[c:2 * c]
    rows_of = lambda x, k: x[k * unit:(k + 1) * unit]
    stack = lambda xs: jnp.concatenate(xs, axis=0)

    x_refs = [(rkvb_ref, rkvbh_ref, latb_ref, latbh_ref) if bk else (rkvf_ref, rkvfh_ref, latf_ref, latfh_ref)
              for bk in back]

    def unit_rows(j):
        start = [pl.multiple_of(((n_units - 1 - j) if bk else j) * unit, unit) for bk in back]
        return start, [pl.ds(st, unit) for st in start]

    def prepare(j, slot):
        start, rows = unit_rows(j)

        def mixed(s, x_ref, halo_ref, mu):
            d, b = streams[s]
            x = x_ref[b, rows[s], :]
            row = lax.broadcasted_iota(jnp.int32, x.shape, 0)
            if back[s]:
                near = x_ref[b, pl.ds(pl.multiple_of(jnp.minimum(start[s] + unit, tc - 8), 8), 8), :][0:1, :]
                edge = jnp.where(j == 0, jnp.where(i == 0, 0.0, halo_ref[b][0:1, :]), near)
                sh = jnp.where(row == unit - 1, edge, pltpu.roll(x, unit - 1, 0))
            else:
                near = x_ref[b, pl.ds(pl.multiple_of(jnp.maximum(start[s] - 8, 0), 8), 8), :][7:8, :]
                edge = jnp.where(j == 0, jnp.where(i == 0, 0.0, halo_ref[b][7:8, :]), near)
                sh = jnp.where(row == 0, edge, pltpu.roll(x, 1, 0))
            return x + (sh - x) * mu[d]

        xr = [mixed(s, x_refs[s][0], x_refs[s][1], mur_ref) for s in ids]
        xl = [mixed(s, x_refs[s][2], x_refs[s][3], mul_ref) for s in ids]
        th = [jnp.tanh(stack([xl[s] for s in ids if streams[s][0] == d])) for d in range(2)]
        yield
        w_log, a = [], []
        for d in range(2):
            xl_d = stack([xl[s] for s in ids if streams[s][0] == d])
            wl = -_softplus(-(w0_ref[d] + _dot(bf(th[d]), bf(wup_ref[d])))) - 0.5
            al = jax.nn.sigmoid(a0_ref[d] + _dot(bf(xl_d), bf(aup_ref[d])))
            w_log += [rows_of(wl, k) for k in range(batch)]
            a += [rows_of(al, k) for k in range(batch)]
        k = [xr[s][:, BRANCH_W:2 * BRANCH_W] for s in ids]
        kk = [k[s] * kk_ref[streams[s][0]] for s in ids]
        yield
        kk_sq = _dot(bf(stack([kk[s] * kk[s] for s in ids])), bd1)
        yield
        for s in ids:
            kks = kk[s] * lax.rsqrt(rows_of(kk_sq, s) + 1e-12)
            r_s[slot, s] = xr[s][:, 0:BRANCH_W]
            k_s[slot, s] = k[s] * (1.0 + (a[s] - 1.0) * ka_ref[streams[s][0]])
            v_s[slot, s] = xr[s][:, 2 * BRANCH_W:3 * BRANCH_W]
            kk_s[slot, s] = kks
            b_s[slot, s] = kks * a[s]
            lw_s[slot, s] = -jnp.exp(w_log[s])

    def advance(slot):
        ids = wid
        back = iback
        part = lambda ref: [ref[slot, s, h * c:(h + 1) * c, :] for s, h in items]
        lw, r, k2, v, kk, bq = part(lw_s), part(r_s), part(k_s), part(v_s), part(kk_s), part(b_s)

        def cumsum(s):
            lw_hi, lw_lo = _split_bf16(lw[s])
            tri = tri_b if back[s] else tri_f
            return _dot(tri, lw_hi) + _dot(tri, lw_lo)

        cumi = [cumsum(s) for s in ids]
        yield
        tot = [cumi[s][0:1, :] if back[s] else cumi[s][c - 1:c, :] for s in ids]
        e_m = [jnp.exp(-cumi[s]) for s in ids]
        e_t = [jnp.exp(tot[s] - cumi[s]) for s in ids]
        pt = [bf(-kk[s] * jnp.exp(cumi[s] - lw[s])) for s in ids]
        rt = [r[s] * jnp.exp(cumi[s]) for s in ids]
        pr = [stack([pt[s], bf(rt[s])]) for s in ids]
        ab = [_nt(pr[s], sm(bf(bq[s] * e_m[s]))) for s in ids]
        ak = [_nt(pr[s], sm(bf(k2[s] * e_m[s]))) for s in ids]
        yield
        a_pb = [jnp.where(strict[s], top(ab[s]), 0.0) for s in ids]
        a_rb = [bf(jnp.where(incl[s], mid(ab[s]), 0.0)) for s in ids]
        a_pk = [bf(jnp.where(strict[s], top(ak[s]), 0.0)) for s in ids]
        a_rk = [bf(jnp.where(incl[s], mid(ak[s]), 0.0)) for s in ids]
        l_b = [bf(_nt(eye_b, sm(bf(bq[s] * e_t[s])))) for s in ids]
        l_k = [bf(_nt(eye_b, sm(bf(k2[s] * e_t[s])))) for s in ids]
        yield

        tinv = [jnp.where(eye, 1.0, a_pb[s]) for s in ids]
        apow = [bf(a_pb[s]) for s in ids]
        a2 = [_dot(apow[s], sm(apow[s])) for s in ids]
        av = [_dot(stack([a_pk[s], a_rk[s], l_k[s]]), sm(bf(v[s]))) for s in ids]
        yield
        for _ in range(NEUMANN_DOUBLINGS - 1):
            apow = [bf(a2[s]) for s in ids]
            both = [_dot(stack([bf(tinv[s]), apow[s]]), sm(apow[s])) for s in ids]
            yield
            tinv = [tinv[s] + top(both[s]) for s in ids]
            a2 = [mid(both[s]) for s in ids]
        tinv = [tinv[s] + _dot(bf(tinv[s]), sm(bf(a2[s]))) for s in ids]
        yield
        tb = [bf(t) for t in tinv]

        ah = [_dot(tb[s], sm(pt[s])) for s in ids]
        uh = [_dot(tb[s], sm(bf(top(av[s])))) for s in ids]
        yield
        lhs = [stack([a_rb[s], l_b[s]]) for s in ids]
        o_a = [_dot(lhs[s], sm(bf(ah[s]))) for s in ids]
        o_u = [_dot(lhs[s], sm(bf(uh[s]))) for s in ids]
        yield
        rh = [rt[s] + top(o_a[s]) for s in ids]
        mt = [jnp.where(eye, jnp.exp(tot[s]), 0.0) + mid(o_a[s]) for s in ids]
        oh = [mid(av[s]) + top(o_u[s]) for s in ids]
        gt = [mid(o_u[s]) + av[s][2 * c:3 * c] for s in ids]

        bonus = _dot(bf(stack([r[w] * k2[w] * rk_ref[streams[items[w][0]][0]] for w in ids])), bd1) * stack(v)
        for w, (s, h) in enumerate(items):
            y_s[slot, s, h * c:(h + 1) * c, :] = bonus[w * c:(w + 1) * c]
        state = [st_ref[s] for s in range(len(streams))]
        for step in range(unit // c):
            now = [w for w in ids if w % (unit // c) == step]
            out = {w: _dot(stack([bf(rh[w]), bf(mt[w])]), sm(bf(state[items[w][0]]))) for w in now}
            yield
            for w in now:
                s, h = items[w]
                state[s] = mid(out[w]) + gt[w]
                o_s[slot, s, h * c:(h + 1) * c, :] = top(out[w]) + oh[w]
        for s in range(len(streams)):
            st_ref[s] = state[s]

    def finish(j, slot):
        _, rows = unit_rows(j)
        o = stack([o_s[slot, s] for s in ids])
        mean = _dot_split_lhs(o, bdm)
        yield
        xc = o - mean
        var = _dot(bf(xc * xc), bdm)
        yield
        y = xc * lax.rsqrt(var + GN_EPS) * lng_ref[...] + lnb_ref[...] + stack([y_s[slot, s] for s in ids])
        for s, (d, b) in enumerate(streams):
            out_ref = outb_ref if d == 1 else outf_ref
            out_ref[b, rows[s], :] = rows_of(y, s)

    def interleave(*phases):
        live = list(phases)
        while live:
            for phase in list(live):
                if next(phase, "done") == "done":
                    live.remove(phase)

    def attend(j):
        phases = []
        for plan, ins, outs, bufs in zip(plans, attn_in, attn_out, attn_buf):
            blocks = [(t, pl.multiple_of((j + n_units * u) * Q_SUB, Q_SUB))
                      for t in range(plan.res) for u in range(plan.tq // Q_SUB // n_units)]
            phases.append(_attn_blocks(plan, i, blocks, ins[0], bufs, ins[7], outs[0], outs[1]))
        return phases

    o_s[0] = jnp.zeros(o_s.shape[1:], F32)
    y_s[0] = jnp.zeros(y_s.shape[1:], F32)
    for plan, ins, bufs in zip(plans, attn_in, attn_buf):
        _attn_gather(plan, ins[:7], bufs)
    interleave(prepare(0, 0))

    def pipeline(j, carry):
        interleave(finish(jnp.maximum(j - 1, 0), 0),
                   advance(0),
                   prepare(jnp.minimum(j + 1, n_units - 1), 0),
                   *attend(j))
        return carry

    lax.fori_loop(0, n_units, pipeline, 0)
    interleave(finish(n_units - 1, 0))


def _mixers(qkv, rkv, lat, lp, batch, seq):
    tc = min(RWKV_ROWS, seq)
    nt = seq // tc
    hb = tc // 8
    last_halo = seq // 8 - 1
    fwd = lambda i: i
    bwd = lambda i: nt - 1 - i
    fwd_halo = lambda i: jnp.maximum(i * hb - 1, 0)
    bwd_halo = lambda i: jnp.minimum((bwd(i) + 1) * hb, last_halo)
    full3 = lambda shape: pl.BlockSpec(shape, lambda i: (0, 0, 0))
    full2 = lambda shape: pl.BlockSpec(shape, lambda i: (0, 0))
    n_streams = 2 * batch
    plans = [_AttnPlan(g, batch, seq, nt) for g in range(N_GROUPS)]
    in_specs = [
        pl.BlockSpec((batch, tc, RKV_W), lambda i: (0, fwd(i), 0)),
        pl.BlockSpec((batch, 8, RKV_W), lambda i: (0, fwd_halo(i), 0)),
        pl.BlockSpec((batch, tc, RKV_W), lambda i: (0, bwd(i), 0)),
        pl.BlockSpec((batch, 8, RKV_W), lambda i: (0, bwd_halo(i), 0)),
        pl.BlockSpec((batch, tc, LAT_W), lambda i: (0, fwd(i), 0)),
        pl.BlockSpec((batch, 8, LAT_W), lambda i: (0, fwd_halo(i), 0)),
        pl.BlockSpec((batch, tc, LAT_W), lambda i: (0, bwd(i), 1)),
        pl.BlockSpec((batch, 8, LAT_W), lambda i: (0, bwd_halo(i), 1)),
        full3((2, 1, RKV_W)), full3((2, 1, LAT_W)), full3((2, 1, BRANCH_W)),
        full3((2, LAT_W, BRANCH_W)), full3((2, 1, BRANCH_W)), full3((2, LAT_W, BRANCH_W)),
        full3((2, 1, BRANCH_W)), full3((2, 1, BRANCH_W)), full3((2, 1, BRANCH_W)),
        full2((1, BRANCH_W)), full2((1, BRANCH_W)),
        full2((BRANCH_W, BRANCH_W)), full2((BRANCH_W, BRANCH_W)),
    ]
    assert len(in_specs) == N_RWKV_IN
    operands = [rkv, rkv, rkv, rkv, lat, lat, lat, lat, lp["mu_rkv"], lp["mu_lat"], lp["w0"], lp["w_up_pad"],
                lp["a0"], lp["a_up_pad"], lp["k_k"], lp["k_a"], lp["r_k"], lp["ln_g"], lp["ln_b"],
                lp["bd_mean"], lp["bd_ones"]]
    rwkv_out = jax.ShapeDtypeStruct((batch, seq, BRANCH_W), F32)
    out_specs = [pl.BlockSpec((batch, tc, BRANCH_W), lambda i: (0, fwd(i), 0)),
                 pl.BlockSpec((batch, tc, BRANCH_W), lambda i: (0, bwd(i), 0))]
    out_shape = [rwkv_out, rwkv_out]
    scratch = ([pltpu.VMEM((n_streams, RWKV_CHUNK, BRANCH_W), F32)]
               + [pltpu.VMEM((1, n_streams, RWKV_UNIT, BRANCH_W), F32)] * 8)
    for plan in plans:
        ins, outs = plan.specs()
        assert len(ins) == N_ATTN_IN
        view = qkv.reshape(batch, seq // PERM_ROWS, plan.dil, plan.per, QKV_W)
        in_specs += ins
        operands += [view] * 7 + [_attn_bias(plan.group)]
        out_specs += outs
        dims = (batch, seq // PERM_ROWS, plan.dil, plan.per, BRANCH_W)
        out_shape += [jax.ShapeDtypeStruct(dims, BF16), jax.ShapeDtypeStruct(dims, F32)]
        scratch += plan.scratch()
    res = pl.pallas_call(
        functools.partial(_mixer_kernel, batch=batch, tc=tc, plans=plans),
        grid=(nt,),
        in_specs=in_specs,
        out_specs=out_specs,
        out_shape=out_shape,
        scratch_shapes=scratch,
        compiler_params=_params("arbitrary"),
        name="mixers",
    )(*operands)
    attn = [(res[2 + 2 * g], res[3 + 2 * g]) for g in range(N_GROUPS)]
    return attn, res[0], res[1]


def _silu(z):
    return z * jax.nn.sigmoid(z)


def _merge_kernel(x_ref, g_ref, o0_ref, l0_ref, o1_ref, l1_ref, o2_ref, l2_ref, z_ref,
                  ubp_ref, bc_ref, ubn_ref, yd0_ref, yd1_ref,
                  unperm_ref, poolw_ref, pools_ref, sgg_ref, sgw_ref, sgb_ref, wg_ref, wb_ref, wo_ref,
                  out_ref, *, tm, seq):
    i = pl.program_id(1)
    n_tiles = seq // tm
    x = x_ref[0]
    ms = jnp.mean(x * x, axis=-1, keepdims=True)
    h = (x * lax.rsqrt(ms + EPS) * g_ref[...]).astype(BF16)
    z = z_ref[0]
    lane_g = _lane_head(tm)
    parts = [slice(p * PERM_ROWS, (p + 1) * PERM_ROWS) for p in range(tm // PERM_ROWS)]
    gate_logits = [[_dot(h[p], wg_ref[:, n * D_MODEL:(n + 1) * D_MODEL]) for n in range(N_BRANCH)]
                   for p in parts]

    def natural(ref, group, exact):
        tiles = []
        for t in range(tm // PERM_ROWS):
            if group == 0:
                tiles.append(ref[0, t, 0].astype(F32))
                continue
            v = jnp.concatenate([ref[0, t, r] for r in range(ATTN_DILATIONS[group])], axis=0)
            unperm = unperm_ref[group - 1]
            tiles.append(_dot_split_lhs_rhs(unperm, v) if exact else _dot(unperm, v))
        return jnp.concatenate(tiles, axis=0)

    l0, l1, l2 = natural(l0_ref, 0, True), natural(l1_ref, 1, True), natural(l2_ref, 2, True)
    mx = jnp.maximum(jnp.maximum(l0, l1), l2)
    e0, e1, e2 = jnp.exp(l0 - mx), jnp.exp(l1 - mx), jnp.exp(l2 - mx)
    y_a = (e0 * natural(o0_ref, 0, False) + e1 * natural(o1_ref, 1, False)
           + e2 * natural(o2_ref, 2, False)) / (e0 + e1 + e2)

    bc = bc_ref[0]
    u = bc[:, 0:BRANCH_W]
    prev = jnp.where(i == 0, 0.0, ubp_ref[0][:, 0:BRANCH_W])
    nxt = jnp.where(i == n_tiles - 1, 0.0, ubn_ref[0][:, 0:BRANCH_W])
    ext = jnp.concatenate([prev, u, nxt], axis=0)
    n_ext = tm + 2 * POOL_HALO
    f1 = ext + pltpu.roll(ext, 1, 0)
    f2 = pltpu.roll(f1, 1, 0) + pltpu.roll(f1, n_ext - 1, 0)
    f4 = pltpu.roll(f2, 2, 0) + pltpu.roll(f2, n_ext - 2, 0)
    f8 = pltpu.roll(f4, 4, 0) + pltpu.roll(f4, n_ext - 4, 0)
    lane_e = _lane_head(n_ext)
    win = jnp.where(lane_e == 0, f1, jnp.where(lane_e == 1, f2, jnp.where(lane_e == 2, f4, f8)))
    win = win[POOL_HALO:POOL_HALO + tm, :]
    pos = i * tm + lax.broadcasted_iota(jnp.int32, (tm, BRANCH_W), 0)
    half = jnp.where(lane_g == 0, POOL_HALF[0],
                     jnp.where(lane_g == 1, POOL_HALF[1], jnp.where(lane_g == 2, POOL_HALF[2], POOL_HALF[3])))
    cnt = (jnp.minimum(pos + half, seq) - jnp.maximum(pos - half, 0)).astype(F32)
    y_b = _dot((win / cnt - u).astype(BF16), poolw_ref[...]) * pools_ref[...]

    u_c = bc[:, BRANCH_W:2 * BRANCH_W]
    v_c = bc[:, 2 * BRANCH_W:3 * BRANCH_W]
    msv = jnp.mean(v_c * v_c, axis=-1, keepdims=True)
    v_n = (v_c * lax.rsqrt(msv + EPS) * sgg_ref[...]).astype(BF16)
    lane_c = _lane_head(CHUNK)
    sv_chunks = []
    for ch in range(tm // CHUNK):
        vch = v_n[ch * CHUNK:(ch + 1) * CHUNK, :]
        v_groups = jnp.concatenate([jnp.where(lane_c == g, vch, jnp.zeros_like(vch)) for g in range(SG_GROUPS)],
                                   axis=0)
        sv_chunks.append(_dot(sgw_ref[...], v_groups) + sgb_ref[...])
    y_c = u_c * jnp.concatenate(sv_chunks, axis=0)

    y_d = yd0_ref[0] + yd1_ref[0]

    ys = [(y * _silu(z[:, n * BRANCH_W:(n + 1) * BRANCH_W])).astype(BF16)
          for n, y in enumerate((y_a, y_b, y_c, y_d))]
    branch = [[_dot(ys[n][p], wb_ref[n]) for n in range(N_BRANCH)] for p in parts]
    merged = []
    for gl, br in zip(gate_logits, branch):
        acc = jax.nn.sigmoid(gl[0]) * br[0]
        for n in range(1, N_BRANCH):
            acc = acc + jax.nn.sigmoid(gl[n]) * br[n]
        merged.append(acc.astype(BF16))
    for p, m in zip(parts, merged):
        out_ref[0, p, :] = x[p] + _dot(m, wo_ref[...])


def _merge(x, attn, z, bc, yd0, yd1, lp, batch, seq):
    tm = MERGE_ROWS
    hb = tm // POOL_HALO
    last_halo = seq // POOL_HALO - 1
    row = lambda w: pl.BlockSpec((1, tm, w), lambda b, i: (b, i, 0))
    const2 = lambda shape: pl.BlockSpec(shape, lambda b, i: (0, 0))
    const3 = lambda shape: pl.BlockSpec(shape, lambda b, i: (0, 0, 0))
    assert tm % PERM_ROWS == 0
    in_specs = [row(D_MODEL), const2((1, D_MODEL))]
    for dil in ATTN_DILATIONS:
        slab = pl.BlockSpec((1, tm // PERM_ROWS, dil, PERM_ROWS // dil, BRANCH_W), lambda b, i: (b, i, 0, 0, 0))
        in_specs += [slab, slab]
    in_specs += [
        row(Z_W),
        pl.BlockSpec((1, POOL_HALO, BRANCH_W), lambda b, i: (b, jnp.maximum(i * hb - 1, 0), 0)),
        row(BC_W),
        pl.BlockSpec((1, POOL_HALO, BRANCH_W), lambda b, i: (b, jnp.minimum((i + 1) * hb, last_halo), 0)),
        row(BRANCH_W), row(BRANCH_W),
        const3((N_GROUPS - 1, PERM_ROWS, PERM_ROWS)),
        const2((BRANCH_W, BRANCH_W)), const2((1, BRANCH_W)), const2((1, BRANCH_W)),
        const2((CHUNK, SG_GROUPS * CHUNK)), const2((CHUNK, BRANCH_W)),
        const2((D_MODEL, N_BRANCH * D_MODEL)), const3((N_BRANCH, BRANCH_W, D_MODEL)),
        const2((D_MODEL, D_MODEL)),
    ]
    flat_attn = [a for pair in attn for a in pair]
    return pl.pallas_call(
        functools.partial(_merge_kernel, tm=tm, seq=seq),
        grid=(batch, seq // tm),
        in_specs=in_specs,
        out_specs=row(D_MODEL),
        out_shape=jax.ShapeDtypeStruct((batch, seq, D_MODEL), F32),
        compiler_params=_params("parallel", "parallel"),
        name="merge",
    )(x, lp["norm_g"], *flat_attn, z, bc, bc, bc, yd0, yd1, _residue_perms(True),
      lp["pool_w_bd"], lp["pool_scale"], lp["sg_norm_g"], lp["sg_w"], lp["sg_bias"],
      lp["w_gate"], lp["w_branch"], lp["w_out"])


def _block_diag(blocks):
    n = blocks.shape[0]
    w = blocks.shape[1]
    out = jnp.zeros((n * w, n * w), blocks.dtype)
    for g in range(n):
        out = out.at[g * w:(g + 1) * w, g * w:(g + 1) * w].set(blocks[g])
    return out


def _layer_params(l, norm_g, w_in, q_norm_g, k_norm_g, pool_w, pool_scale, sg_norm_g, sg_w, sg_b,
                  mu_rkv, mu_lat, w0, w_up, a0, a_up, k_k, k_a, r_k, ln_g, ln_b, w_branch, w_out):
    w = w_in[l]
    seg = lambda lo, width: w[:, lo:lo + width]
    w_small = jnp.concatenate([
        seg(0, QKV_W), seg(3840, RKV_W), seg(4608, 2 * LAT_W), seg(2560, BRANCH_W), seg(3072, 2 * BRANCH_W),
        seg(2304, BRANCH_W), seg(2816, BRANCH_W), seg(3584, BRANCH_W), seg(4864, BRANCH_W)], axis=1)
    ones_blocks = jnp.ones((RWKV_HEADS, HEAD_DIM, HEAD_DIM), F32)
    zeros_lat = jnp.zeros((2, LAT_W // 2, BRANCH_W), F32)
    vec3 = lambda p: p[l][:, None, :]
    return {
        "norm_g": norm_g[l][None, :],
        "w_small": w_small.astype(BF16),
        "w_gate": w[:, PROJ_SMALL_W:].astype(BF16),
        "q_gain": jnp.tile(q_norm_g[l], HEADS_PER_GROUP)[None, :],
        "k_gain": jnp.tile(k_norm_g[l], HEADS_PER_GROUP)[None, :],
        "bd_mean": (_block_diag(ones_blocks) / HEAD_DIM).astype(BF16),
        "bd_ones": _block_diag(ones_blocks).astype(BF16),
        "pool_w_bd": _block_diag(pool_w[l]).astype(BF16),
        "pool_scale": pool_scale[l][None, :],
        "sg_norm_g": sg_norm_g[l][None, :],
        "sg_w": jnp.concatenate([sg_w[l][g] for g in range(SG_GROUPS)], axis=1).astype(BF16),
        "sg_bias": jnp.repeat(sg_b[l].T, HEAD_DIM, axis=1),
        "mu_rkv": vec3(mu_rkv), "mu_lat": vec3(mu_lat), "w0": vec3(w0), "a0": vec3(a0),
        "k_k": vec3(k_k), "k_a": vec3(k_a), "r_k": vec3(r_k),
        "w_up_pad": jnp.concatenate([w_up[l], zeros_lat], axis=1),
        "a_up_pad": jnp.concatenate([zeros_lat, a_up[l]], axis=1),
        "ln_g": ln_g[l][None, :], "ln_b": ln_b[l][None, :],
        "w_branch": w_branch[l].astype(BF16),
        "w_out": w_out[l].astype(BF16),
    }


def _layer(x, lp):
    batch, seq, _ = x.shape
    qkv, rkv, lat, bc, z = _proj(x.reshape(batch * seq, D_MODEL), lp)
    rkv = rkv.reshape(batch, seq, RKV_W)
    lat = lat.reshape(batch, seq, 2 * LAT_W)
    attn, yd0, yd1 = _mixers(qkv, rkv, lat, lp, batch, seq)
    return _merge(x, attn, z.reshape(batch, seq, Z_W), bc.reshape(batch, seq, BC_W), yd0, yd1, lp, batch, seq)


def kernel(x_prompt, x_sample, norm_g, w_in, q_norm_g, k_norm_g, pool_w, pool_scale, sg_norm_g, sg_w, sg_b,
           mu_rkv, mu_lat, w0, w_up, a0, a_up, k_k, k_a, r_k, ln_g, ln_b, w_branch, w_out):
    weights = (norm_g, w_in, q_norm_g, k_norm_g, pool_w, pool_scale, sg_norm_g, sg_w, sg_b,
               mu_rkv, mu_lat, w0, w_up, a0, a_up, k_k, k_a, r_k, ln_g, ln_b, w_branch, w_out)
    layers = [_layer_params(l, *weights) for l in range(norm_g.shape[0])]
    outs = []
    for x in (x_prompt, x_sample):
        for lp in layers:
            x = _layer(x, lp)
        outs.append(x)
    return tuple(outs)
```

```python
import functools

import jax
import jax.numpy as jnp
import numpy as np
from jax import lax
from jax.experimental import pallas as pl
from jax.experimental.pallas import tpu as pltpu

F32 = jnp.float32
BF16 = jnp.bfloat16

D_MODEL = 1024
BRANCH_W = 256
N_BRANCH = 4
EPS = 1e-6
NEG_INF = -1e30

HEAD_DIM = 64
ATTN_DILATIONS = (1, 4, 16)
N_GROUPS = 3
HEADS_PER_GROUP = 4
N_ATTN_HEADS = N_GROUPS * HEADS_PER_GROUP
ATTN_QKV = N_ATTN_HEADS * HEAD_DIM
HALF_KEYS = 64
Q_SUB = 128
K_WIN = Q_SUB + 2 * HALF_KEYS

POOL_HALF = (1, 2, 4, 8)
POOL_HALO = 8
CHUNK = 128
SG_GROUPS = 4

RWKV_HEADS = 4
RWKV_N = 64
RWKV_CHUNK = 64
RWKV_UNIT = 128
LAT_W = 128
GN_EPS = 64e-5
NEUMANN_DOUBLINGS = 5

QKV_W = 3 * ATTN_QKV
RKV_W = 3 * BRANCH_W
BC_W = 3 * BRANCH_W
Z_W = 4 * BRANCH_W
PROJ_SMALL_W = QKV_W + RKV_W + 2 * LAT_W + BC_W + Z_W

PROJ_ROWS = 512
MERGE_ROWS = 512
PERM_ROWS = 256
RWKV_ROWS = 512
VMEM_LIMIT = 62 * 1024 * 1024


def _params(*sem):
    return pltpu.CompilerParams(dimension_semantics=sem, vmem_limit_bytes=VMEM_LIMIT)


def _dot(a, b):
    return jnp.dot(a, b, preferred_element_type=F32)


def _split_bf16(x):
    hi = x.astype(BF16)
    lo = (x - hi.astype(F32)).astype(BF16)
    return hi, lo


def _dot_split_lhs(x, m_bf16):
    hi, lo = _split_bf16(x)
    return _dot(hi, m_bf16) + _dot(lo, m_bf16)


def _dot_split_lhs_rhs(m_bf16, x):
    hi, lo = _split_bf16(x)
    return _dot(m_bf16, hi) + _dot(m_bf16, lo)


def _dot_split_both(a, b):
    a_hi, a_lo = _split_bf16(a)
    b_hi, b_lo = _split_bf16(b)
    return _dot(a_hi, b_hi) + _dot(a_lo, b_hi) + _dot(a_hi, b_lo)


def _lane_head(rows):
    return lax.broadcasted_iota(jnp.int32, (rows, BRANCH_W), 1) // HEAD_DIM


def _proj_kernel(x_ref, g_ref, w_ref, qg_ref, kg_ref, perm_ref, qkv_ref, rkv_ref, lat_ref, bc_ref, z_ref):
    x = x_ref[...]
    ms = jnp.mean(x * x, axis=-1, keepdims=True)
    h = (x * lax.rsqrt(ms + EPS) * g_ref[...]).astype(BF16)
    n_chunks = 3 * N_GROUPS

    def project(c):
        return _dot(h, w_ref[:, c * BRANCH_W:(c + 1) * BRANCH_W])

    lane_h = _lane_head(x.shape[0])

    def normalize(c, y):
        if c < 2 * N_GROUPS:
            sq = y * y
            msq = jnp.zeros_like(sq)
            for hd in range(HEADS_PER_GROUP):
                mine = lane_h == hd
                tot = jnp.sum(jnp.where(mine, sq, 0.0), axis=-1, keepdims=True)
                msq = jnp.where(mine, tot * (1.0 / HEAD_DIM), msq)
            if c < N_GROUPS:
                y = y * lax.rsqrt(msq + EPS) * qg_ref[...] * (HEAD_DIM ** -0.5)
            else:
                y = y * lax.rsqrt(msq + EPS) * kg_ref[...]
        return y.astype(BF16)

    def store(c, yb):
        group = c % N_GROUPS
        if group > 0:
            perm = perm_ref[group - 1]
            yb = jnp.concatenate([_dot(perm, yb[t * PERM_ROWS:(t + 1) * PERM_ROWS]).astype(BF16)
                                  for t in range(yb.shape[0] // PERM_ROWS)], axis=0)
        qkv_ref[:, c * BRANCH_W:(c + 1) * BRANCH_W] = yb

    others = []
    o = QKV_W
    for ref, width in ((rkv_ref, RKV_W), (lat_ref, 2 * LAT_W), (bc_ref, BC_W), (z_ref, Z_W)):
        others.append((ref, o, width))
        o += width

    projected, normalized = {}, {}
    for step in range(n_chunks + 2):
        if step < n_chunks:
            projected[step] = project(step)
        elif others:
            ref, lo, width = others.pop(0)
            ref[...] = _dot(h, w_ref[:, lo:lo + width])
        if 1 <= step <= n_chunks:
            normalized[step - 1] = normalize(step - 1, projected.pop(step - 1))
        if step >= 2:
            store(step - 2, normalized.pop(step - 2))
    for ref, lo, width in others:
        ref[...] = _dot(h, w_ref[:, lo:lo + width])


def _residue_perms(transpose):
    mats = []
    for dil in ATTN_DILATIONS[1:]:
        per = PERM_ROWS // dil
        p = np.zeros((PERM_ROWS, PERM_ROWS), np.float32)
        old = np.arange(PERM_ROWS)
        p[(old % dil) * per + old // dil, old] = 1.0
        mats.append(p.T if transpose else p)
    return jnp.asarray(np.stack(mats), BF16)


def _proj(x2d, lp):
    rows = x2d.shape[0]
    tm = PROJ_ROWS
    assert tm % PERM_ROWS == 0
    const = lambda i: (0, 0)
    row = lambda i: (i, 0)
    widths = (QKV_W, RKV_W, 2 * LAT_W, BC_W, Z_W)
    dtypes = (BF16, F32, F32, F32, F32)
    return pl.pallas_call(
        _proj_kernel,
        grid=(rows // tm,),
        in_specs=[
            pl.BlockSpec((tm, D_MODEL), row),
            pl.BlockSpec((1, D_MODEL), const),
            pl.BlockSpec((D_MODEL, PROJ_SMALL_W), const),
            pl.BlockSpec((1, BRANCH_W), const),
            pl.BlockSpec((1, BRANCH_W), const),
            pl.BlockSpec((N_GROUPS - 1, PERM_ROWS, PERM_ROWS), lambda i: (0, 0, 0)),
        ],
        out_specs=[pl.BlockSpec((tm, w), row) for w in widths],
        out_shape=[jax.ShapeDtypeStruct((rows, w), dt) for w, dt in zip(widths, dtypes)],
        compiler_params=_params("parallel"),
        name="proj",
    )(x2d, lp["norm_g"], lp["w_small"], lp["q_gain"], lp["k_gain"], _residue_perms(False))


class _AttnPlan:
    def __init__(self, group, batch, seq, n_steps):
        self.group = group
        self.dil = ATTN_DILATIONS[group]
        self.seq_m = seq // self.dil
        self.per = PERM_ROWS // self.dil
        rows = batch * seq // n_steps
        self.res = max(1, rows // self.seq_m)
        self.tq = rows // self.res
        self.nq = self.seq_m // self.tq
        assert self.dil % self.res == 0 and batch * self.dil * self.nq == n_steps * self.res

    def decode(self, i):
        t = i * self.res
        return t // (self.dil * self.nq), ((t // self.nq) % self.dil) // self.res, t % self.nq

    def specs(self):
        per, res, tq = self.per, self.res, self.tq
        hb = tq // HALF_KEYS
        last_halo = self.seq_m // HALF_KEYS - 1
        prev = lambda iq: jnp.maximum(iq * hb - 1, 0)
        nxt = lambda iq: jnp.minimum((iq + 1) * hb, last_halo)

        def col(which):
            return which * N_GROUPS + self.group

        def main(which):
            def index(i):
                b, rb, iq = self.decode(i)
                return b, iq, rb, 0, col(which)
            return pl.BlockSpec((1, tq // per, res, per, BRANCH_W), index)

        def halo(which, pick):
            split = max(1, per // HALF_KEYS)

            def index(i):
                b, rb, iq = self.decode(i)
                h = pick(iq)
                return b, h // split, rb, h % split, col(which)
            if split == 1:
                return pl.BlockSpec((1, HALF_KEYS // per, res, per, BRANCH_W), index)
            return pl.BlockSpec((1, 1, res, HALF_KEYS, BRANCH_W), index)

        def out_index(i):
            b, rb, iq = self.decode(i)
            return b, iq, rb, 0, 0

        ins = [main(0), halo(1, prev), main(1), halo(1, nxt), halo(2, prev), main(2), halo(2, nxt),
               pl.BlockSpec((HEADS_PER_GROUP * Q_SUB, K_WIN), lambda i: (0, 0))]
        outs = [pl.BlockSpec((1, tq // per, res, per, BRANCH_W), out_index)] * 2
        return ins, outs

    def scratch(self):
        buf = lambda rows: pltpu.VMEM((self.res, rows, BRANCH_W), BF16)
        return [buf(self.tq + 2 * HALF_KEYS), buf(self.tq + 2 * HALF_KEYS)]


def _attn_gather(plan, refs, bufs):
    _, kp_ref, k_ref, kn_ref, vp_ref, v_ref, vn_ref = refs
    kbuf, vbuf = bufs

    def gather(dst, t, at, src):
        per = src.shape[3]
        for j in range(src.shape[1]):
            dst[t, at + j * per:at + (j + 1) * per, :] = src[0, j, t]

    for t in range(plan.res):
        gather(kbuf, t, 0, kp_ref)
        gather(kbuf, t, HALF_KEYS, k_ref)
        gather(kbuf, t, HALF_KEYS + plan.tq, kn_ref)
        gather(vbuf, t, 0, vp_ref)
        gather(vbuf, t, HALF_KEYS, v_ref)
        gather(vbuf, t, HALF_KEYS + plan.tq, vn_ref)


def _attn_blocks(plan, i, sub_blocks, q_ref, bufs, bias_ref, o_ref, lse_ref):
    kbuf, vbuf = bufs
    per_out = o_ref.shape[3]
    lane_h = _lane_head(Q_SUB)
    col = lax.broadcasted_iota(jnp.int32, (1, K_WIN), 1)
    _, _, iq = plan.decode(i)

    def scores(t, r0):
        if per_out >= Q_SUB:
            qb = q_ref[0, r0 // per_out, t, pl.ds(pl.multiple_of(r0 % per_out, Q_SUB), Q_SUB), :]
        else:
            qb = jnp.concatenate([q_ref[0, r0 // per_out + j, t] for j in range(Q_SUB // per_out)], axis=0)
        qs = jnp.concatenate(
            [jnp.where(lane_h == h, qb, jnp.zeros_like(qb)) for h in range(HEADS_PER_GROUP)], axis=0)
        return lax.dot_general(qs, kbuf[t, pl.ds(r0, K_WIN), :], (((1,), (1,)), ((), ())),
                               preferred_element_type=F32)

    def weigh(t, r0, s):
        kpos = iq * plan.tq + (r0 - HALF_KEYS) + col
        s = s + bias_ref[...] + jnp.where((kpos >= 0) & (kpos < plan.seq_m), 0.0, NEG_INF)
        m = jnp.max(s, axis=-1, keepdims=True)
        p = jnp.exp(s - m)
        l = jnp.sum(p, axis=-1, keepdims=True)
        return _dot(p.astype(BF16), vbuf[t, pl.ds(r0, K_WIN), :]), l, m

    def store(t, r0, pv, l, m):
        on = pv / l
        ls = m + jnp.log(l)
        o = jnp.zeros((Q_SUB, BRANCH_W), F32)
        lo = jnp.zeros((Q_SUB, BRANCH_W), F32)
        for h in range(HEADS_PER_GROUP):
            sel = lane_h == h
            o = jnp.where(sel, on[h * Q_SUB:(h + 1) * Q_SUB, :], o)
            lo = jnp.where(sel, ls[h * Q_SUB:(h + 1) * Q_SUB, :], lo)
        if per_out >= Q_SUB:
            at = pl.ds(pl.multiple_of(r0 % per_out, Q_SUB), Q_SUB)
            o_ref[0, r0 // per_out, t, at, :] = o.astype(o_ref.dtype)
            lse_ref[0, r0 // per_out, t, at, :] = lo
        for j in range(Q_SUB // per_out):
            o_ref[0, r0 // per_out + j, t] = o[j * per_out:(j + 1) * per_out, :].astype(o_ref.dtype)
            lse_ref[0, r0 // per_out + j, t] = lo[j * per_out:(j + 1) * per_out, :]

    scored = weighed = None
    for step in range(len(sub_blocks) + 2):
        if weighed is not None:
            store(*weighed)
        weighed = None if scored is None else (scored[0], scored[1]) + weigh(*scored)
        scored = sub_blocks[step] + (scores(*sub_blocks[step]),) if step < len(sub_blocks) else None
        if step < len(sub_blocks) + 1:
            yield


def _attn_bias(group):
    dil = ATTN_DILATIONS[group]
    i = np.arange(Q_SUB)[:, None]
    j = np.arange(K_WIN)[None, :]
    off = np.abs(j - HALF_KEYS - i).astype(np.float32)
    slopes = 2.0 ** (-8.0 * np.arange(1, N_ATTN_HEADS + 1, dtype=np.float32) / N_ATTN_HEADS)
    blocks = []
    for h in range(HEADS_PER_GROUP):
        slope = slopes[group * HEADS_PER_GROUP + h]
        blocks.append(np.where(off <= HALF_KEYS, -slope * (off * dil), NEG_INF))
    return jnp.asarray(np.concatenate(blocks, axis=0), F32)


def _softplus(y):
    return jnp.maximum(y, 0.0) + jnp.log(1.0 + jnp.exp(-jnp.abs(y)))


def _stack_heads(xb, lane_h):
    return jnp.concatenate(
        [jnp.where(lane_h == h, xb, jnp.zeros_like(xb)) for h in range(RWKV_HEADS)], axis=0)


def _nt(a, b):
    return lax.dot_general(a, b, (((1,), (1,)), ((), ())), preferred_element_type=F32)


N_RWKV_IN = 21
N_ATTN_IN = 8


def _mixer_kernel(*refs, batch, tc, plans):
    n_in = N_RWKV_IN + N_ATTN_IN * len(plans)
    n_out = 2 + 2 * len(plans)
    (rkvf_ref, rkvfh_ref, rkvb_ref, rkvbh_ref, latf_ref, latfh_ref, latb_ref, latbh_ref,
     mur_ref, mul_ref, w0_ref, wup_ref, a0_ref, aup_ref, kk_ref, ka_ref, rk_ref,
     lng_ref, lnb_ref, bdm_ref, bd1_ref) = refs[:N_RWKV_IN]
    attn_in = [refs[N_RWKV_IN + N_ATTN_IN * g:N_RWKV_IN + N_ATTN_IN * (g + 1)] for g in range(len(plans))]
    outf_ref, outb_ref = refs[n_in:n_in + 2]
    attn_out = [refs[n_in + 2 + 2 * g:n_in + 4 + 2 * g] for g in range(len(plans))]
    st_ref, r_s, k_s, v_s, kk_s, b_s, lw_s, o_s, y_s = refs[n_in + n_out:n_in + n_out + 9]
    attn_buf = [refs[n_in + n_out + 9 + 2 * g:n_in + n_out + 11 + 2 * g] for g in range(len(plans))]

    i = pl.program_id(0)
    c = RWKV_CHUNK
    unit = RWKV_UNIT
    n_units = tc // unit
    streams = [(d, b) for d in range(2) for b in range(batch)]
    ids = range(len(streams))
    back = [d == 1 for d, _ in streams]
    items = [(s, h) for s in ids for h in (reversed(range(unit // c)) if back[s] else range(unit // c))]
    wid = range(len(items))
    iback = [back[s] for s, _ in items]

    @pl.when(i == 0)
    def _():
        st_ref[...] = jnp.zeros_like(st_ref)

    lane_h = _lane_head(c)
    t_i = lax.broadcasted_iota(jnp.int32, (c, c), 0)
    s_i = lax.broadcasted_iota(jnp.int32, (c, c), 1)
    tau = lax.broadcasted_iota(jnp.int32, (c, BRANCH_W), 0)
    sig = lax.broadcasted_iota(jnp.int32, (c, BRANCH_W), 1) % c
    eye = sig == tau
    eye_b = jnp.where(eye, 1.0, 0.0).astype(BF16)
    tri_f = jnp.where(s_i <= t_i, 1.0, 0.0).astype(BF16)
    tri_b = jnp.where(s_i >= t_i, 1.0, 0.0).astype(BF16)
    strict = [(sig > tau) if bk else (sig < tau) for bk in iback]
    incl = [(sig >= tau) if bk else (sig <= tau) for bk in iback]
    bd1 = bd1_ref[...]
    bdm = bdm_ref[...]
    sm = lambda xb: _stack_heads(xb, lane_h)
    bf = lambda x: x.astype(BF16)
    top = lambda x: x[0:c]
    mid = lambda x: x[c:2 * c]
    rows_of = lambda x, k: x[k * unit:(k + 1) * unit]
    stack = lambda xs: jnp.concatenate(xs, axis=0)

    x_refs = [(rkvb_ref, rkvbh_ref, latb_ref, latbh_ref) if bk else (rkvf_ref, rkvfh_ref, latf_ref, latfh_ref)
              for bk in back]

    def unit_rows(j):
        start = [pl.multiple_of(((n_units - 1 - j) if bk else j) * unit, unit) for bk in back]
        return start, [pl.ds(st, unit) for st in start]

    def prepare(j, slot):
        start, rows = unit_rows(j)

        def mixed(s, x_ref, halo_ref, mu):
            d, b = streams[s]
            x = x_ref[b, rows[s], :]
            row = lax.broadcasted_iota(jnp.int32, x.shape, 0)
            if back[s]:
                near = x_ref[b, pl.ds(pl.multiple_of(jnp.minimum(start[s] + unit, tc - 8), 8), 8), :][0:1, :]
                edge = jnp.where(j == 0, jnp.where(i == 0, 0.0, halo_ref[b][0:1, :]), near)
                sh = jnp.where(row == unit - 1, edge, pltpu.roll(x, unit - 1, 0))
            else:
                near = x_ref[b, pl.ds(pl.multiple_of(jnp.maximum(start[s] - 8, 0), 8), 8), :][7:8, :]
                edge = jnp.where(j == 0, jnp.where(i == 0, 0.0, halo_ref[b][7:8, :]), near)
                sh = jnp.where(row == 0, edge, pltpu.roll(x, 1, 0))
            return x + (sh - x) * mu[d]

        xr = [mixed(s, x_refs[s][0], x_refs[s][1], mur_ref) for s in ids]
        xl = [mixed(s, x_refs[s][2], x_refs[s][3], mul_ref) for s in ids]
        th = [jnp.tanh(stack([xl[s] for s in ids if streams[s][0] == d])) for d in range(2)]
        yield
        w_log, a = [], []
        for d in range(2):
            xl_d = stack([xl[s] for s in ids if streams[s][0] == d])
            wl = -_softplus(-(w0_ref[d] + _dot(bf(th[d]), bf(wup_ref[d])))) - 0.5
            al = jax.nn.sigmoid(a0_ref[d] + _dot(bf(xl_d), bf(aup_ref[d])))
            w_log += [rows_of(wl, k) for k in range(batch)]
            a += [rows_of(al, k) for k in range(batch)]
        k = [xr[s][:, BRANCH_W:2 * BRANCH_W] for s in ids]
        kk = [k[s] * kk_ref[streams[s][0]] for s in ids]
        yield
        kk_sq = _dot(bf(stack([kk[s] * kk[s] for s in ids])), bd1)
        yield
        for s in ids:
            kks = kk[s] * lax.rsqrt(rows_of(kk_sq, s) + 1e-12)
            r_s[slot, s] = xr[s][:, 0:BRANCH_W]
            k_s[slot, s] = k[s] * (1.0 + (a[s] - 1.0) * ka_ref[streams[s][0]])
            v_s[slot, s] = xr[s][:, 2 * BRANCH_W:3 * BRANCH_W]
            kk_s[slot, s] = kks
            b_s[slot, s] = kks * a[s]
            lw_s[slot, s] = -jnp.exp(w_log[s])

    def advance(slot):
        ids = wid
        back = iback
        part = lambda ref: [ref[slot, s, h * c:(h + 1) * c, :] for s, h in items]
        lw, r, k2, v, kk, bq = part(lw_s), part(r_s), part(k_s), part(v_s), part(kk_s), part(b_s)

        def cumsum(s):
            lw_hi, lw_lo = _split_bf16(lw[s])
            tri = tri_b if back[s] else tri_f
            return _dot(tri, lw_hi) + _dot(tri, lw_lo)

        cumi = [cumsum(s) for s in ids]
        yield
        tot = [cumi[s][0:1, :] if back[s] else cumi[s][c - 1:c, :] for s in ids]
        e_m = [jnp.exp(-cumi[s]) for s in ids]
        e_t = [jnp.exp(tot[s] - cumi[s]) for s in ids]
        pt = [bf(-kk[s] * jnp.exp(cumi[s] - lw[s])) for s in ids]
        rt = [r[s] * jnp.exp(cumi[s]) for s in ids]
        pr = [stack([pt[s], bf(rt[s])]) for s in ids]
        ab = [_nt(pr[s], sm(bf(bq[s] * e_m[s]))) for s in ids]
        ak = [_nt(pr[s], sm(bf(k2[s] * e_m[s]))) for s in ids]
        yield
        a_pb = [jnp.where(strict[s], top(ab[s]), 0.0) for s in ids]
        a_rb = [bf(jnp.where(incl[s], mid(ab[s]), 0.0)) for s in ids]
        a_pk = [bf(jnp.where(strict[s], top(ak[s]), 0.0)) for s in ids]
        a_rk = [bf(jnp.where(incl[s], mid(ak[s]), 0.0)) for s in ids]
        l_b = [bf(_nt(eye_b, sm(bf(bq[s] * e_t[s])))) for s in ids]
        l_k = [bf(_nt(eye_b, sm(bf(k2[s] * e_t[s])))) for s in ids]
        yield

        tinv = [jnp.where(eye, 1.0, a_pb[s]) for s in ids]
        apow = [bf(a_pb[s]) for s in ids]
        a2 = [_dot(apow[s], sm(apow[s])) for s in ids]
        av = [_dot(stack([a_pk[s], a_rk[s], l_k[s]]), sm(bf(v[s]))) for s in ids]
        yield
        for _ in range(NEUMANN_DOUBLINGS - 1):
            apow = [bf(a2[s]) for s in ids]
            both = [_dot(stack([bf(tinv[s]), apow[s]]), sm(apow[s])) for s in ids]
            yield
            tinv = [tinv[s] + top(both[s]) for s in ids]
            a2 = [mid(both[s]) for s in ids]
        tinv = [tinv[s] + _dot(bf(tinv[s]), sm(bf(a2[s]))) for s in ids]
        yield
        tb = [bf(t) for t in tinv]

        ah = [_dot(tb[s], sm(pt[s])) for s in ids]
        uh = [_dot(tb[s], sm(bf(top(av[s])))) for s in ids]
        yield
        lhs = [stack([a_rb[s], l_b[s]]) for s in ids]
        o_a = [_dot(lhs[s], sm(bf(ah[s]))) for s in ids]
        o_u = [_dot(lhs[s], sm(bf(uh[s]))) for s in ids]
        yield
        rh = [rt[s] + top(o_a[s]) for s in ids]
        mt = [jnp.where(eye, jnp.exp(tot[s]), 0.0) + mid(o_a[s]) for s in ids]
        oh = [mid(av[s]) + top(o_u[s]) for s in ids]
        gt = [mid(o_u[s]) + av[s][2 * c:3 * c] for s in ids]

        bonus = _dot(bf(stack([r[w] * k2[w] * rk_ref[streams[items[w][0]][0]] for w in ids])), bd1) * stack(v)
        for w, (s, h) in enumerate(items):
            y_s[slot, s, h * c:(h + 1) * c, :] = bonus[w * c:(w + 1) * c]
        state = [st_ref[s] for s in range(len(streams))]
        for step in range(unit // c):
            now = [w for w in ids if w % (unit // c) == step]
            out = {w: _dot(stack([bf(rh[w]), bf(mt[w])]), sm(bf(state[items[w][0]]))) for w in now}
            yield
            for w in now:
                s, h = items[w]
                state[s] = mid(out[w]) + gt[w]
                o_s[slot, s, h * c:(h + 1) * c, :] = top(out[w]) + oh[w]
        for s in range(len(streams)):
            st_ref[s] = state[s]

    def finish(j, slot):
        _, rows = unit_rows(j)
        o = stack([o_s[slot, s] for s in ids])
        mean = _dot_split_lhs(o, bdm)
        yield
        xc = o - mean
        var = _dot(bf(xc * xc), bdm)
        yield
        y = xc * lax.rsqrt(var + GN_EPS) * lng_ref[...] + lnb_ref[...] + stack([y_s[slot, s] for s in ids])
        for s, (d, b) in enumerate(streams):
            out_ref = outb_ref if d == 1 else outf_ref
            out_ref[b, rows[s], :] = rows_of(y, s)

    def interleave(*phases):
        live = list(phases)
        while live:
            for phase in list(live):
                if next(phase, "done") == "done":
                    live.remove(phase)

    def attend(j):
        phases = []
        for plan, ins, outs, bufs in zip(plans, attn_in, attn_out, attn_buf):
            assert plan.tq % (Q_SUB * n_units) == 0
            blocks = [(t, pl.multiple_of((j + n_units * u) * Q_SUB, Q_SUB))
                      for t in range(plan.res) for u in range(plan.tq // Q_SUB // n_units)]
            phases.append(_attn_blocks(plan, i, blocks, ins[0], bufs, ins[7], outs[0], outs[1]))
        return phases

    o_s[0] = jnp.zeros(o_s.shape[1:], F32)
    y_s[0] = jnp.zeros(y_s.shape[1:], F32)
    for plan, ins, bufs in zip(plans, attn_in, attn_buf):
        _attn_gather(plan, ins[:7], bufs)
    interleave(prepare(0, 0))

    def pipeline(j, carry):
        interleave(finish(jnp.maximum(j - 1, 0), 0),
                   advance(0),
                   prepare(jnp.minimum(j + 1, n_units - 1), 0),
                   *attend(j))
        return carry

    lax.fori_loop(0, n_units, pipeline, 0)
    interleave(finish(n_units - 1, 0))


def _mixers(qkv, rkv, lat, lp, batch, seq):
    tc = min(RWKV_ROWS, seq)
    nt = seq // tc
    hb = tc // 8
    last_halo = seq // 8 - 1
    fwd = lambda i: i
    bwd = lambda i: nt - 1 - i
    fwd_halo = lambda i: jnp.maximum(i * hb - 1, 0)
    bwd_halo = lambda i: jnp.minimum((bwd(i) + 1) * hb, last_halo)
    full3 = lambda shape: pl.BlockSpec(shape, lambda i: (0, 0, 0))
    full2 = lambda shape: pl.BlockSpec(shape, lambda i: (0, 0))
    n_streams = 2 * batch
    plans = [_AttnPlan(g, batch, seq, nt) for g in range(N_GROUPS)]
    in_specs = [
        pl.BlockSpec((batch, tc, RKV_W), lambda i: (0, fwd(i), 0)),
        pl.BlockSpec((batch, 8, RKV_W), lambda i: (0, fwd_halo(i), 0)),
        pl.BlockSpec((batch, tc, RKV_W), lambda i: (0, bwd(i), 0)),
        pl.BlockSpec((batch, 8, RKV_W), lambda i: (0, bwd_halo(i), 0)),
        pl.BlockSpec((batch, tc, LAT_W), lambda i: (0, fwd(i), 0)),
        pl.BlockSpec((batch, 8, LAT_W), lambda i: (0, fwd_halo(i), 0)),
        pl.BlockSpec((batch, tc, LAT_W), lambda i: (0, bwd(i), 1)),
        pl.BlockSpec((batch, 8, LAT_W), lambda i: (0, bwd_halo(i), 1)),
        full3((2, 1, RKV_W)), full3((2, 1, LAT_W)), full3((2, 1, BRANCH_W)),
        full3((2, LAT_W, BRANCH_W)), full3((2, 1, BRANCH_W)), full3((2, LAT_W, BRANCH_W)),
        full3((2, 1, BRANCH_W)), full3((2, 1, BRANCH_W)), full3((2, 1, BRANCH_W)),
        full2((1, BRANCH_W)), full2((1, BRANCH_W)),
        full2((BRANCH_W, BRANCH_W)), full2((BRANCH_W, BRANCH_W)),
    ]
    assert len(in_specs) == N_RWKV_IN
    operands = [rkv, rkv, rkv, rkv, lat, lat, lat, lat, lp["mu_rkv"], lp["mu_lat"], lp["w0"], lp["w_up_pad"],
                lp["a0"], lp["a_up_pad"], lp["k_k"], lp["k_a"], lp["r_k"], lp["ln_g"], lp["ln_b"],
                lp["bd_mean"], lp["bd_ones"]]
    rwkv_out = jax.ShapeDtypeStruct((batch, seq, BRANCH_W), F32)
    out_specs = [pl.BlockSpec((batch, tc, BRANCH_W), lambda i: (0, fwd(i), 0)),
                 pl.BlockSpec((batch, tc, BRANCH_W), lambda i: (0, bwd(i), 0))]
    out_shape = [rwkv_out, rwkv_out]
    scratch = ([pltpu.VMEM((n_streams, RWKV_CHUNK, BRANCH_W), F32)]
               + [pltpu.VMEM((1, n_streams, RWKV_UNIT, BRANCH_W), F32)] * 8)
    for plan in plans:
        ins, outs = plan.specs()
        assert len(ins) == N_ATTN_IN
        view = qkv.reshape(batch, seq // PERM_ROWS, plan.dil, plan.per, QKV_W)
        in_specs += ins
        operands += [view] * 7 + [_attn_bias(plan.group)]
        out_specs += outs
        dims = (batch, seq // PERM_ROWS, plan.dil, plan.per, BRANCH_W)
        out_shape += [jax.ShapeDtypeStruct(dims, BF16), jax.ShapeDtypeStruct(dims, F32)]
        scratch += plan.scratch()
    res = pl.pallas_call(
        functools.partial(_mixer_kernel, batch=batch, tc=tc, plans=plans),
        grid=(nt,),
        in_specs=in_specs,
        out_specs=out_specs,
        out_shape=out_shape,
        scratch_shapes=scratch,
        compiler_params=_params("arbitrary"),
        name="mixers",
    )(*operands)
    attn = [(res[2 + 2 * g], res[3 + 2 * g]) for g in range(N_GROUPS)]
    return attn, res[0], res[1]


def _silu(z):
    return z * jax.nn.sigmoid(z)


def _merge_kernel(x_ref, g_ref, o0_ref, l0_ref, o1_ref, l1_ref, o2_ref, l2_ref, z_ref,
                  ubp_ref, bc_ref, ubn_ref, yd0_ref, yd1_ref,
                  unperm_ref, poolw_ref, pools_ref, sgg_ref, sgw_ref, sgb_ref, wg_ref, wb_ref, wo_ref,
                  out_ref, *, tm, seq):
    i = pl.program_id(1)
    n_tiles = seq // tm
    x = x_ref[0]
    ms = jnp.mean(x * x, axis=-1, keepdims=True)
    h = (x * lax.rsqrt(ms + EPS) * g_ref[...]).astype(BF16)
    z = z_ref[0]
    lane_g = _lane_head(tm)
    parts = [slice(p * PERM_ROWS, (p + 1) * PERM_ROWS) for p in range(tm // PERM_ROWS)]

    def natural(ref, group, exact):
        tiles = []
        for t in range(tm // PERM_ROWS):
            if group == 0:
                tiles.append(ref[0, t, 0].astype(F32))
                continue
            v = jnp.concatenate([ref[0, t, r] for r in range(ATTN_DILATIONS[group])], axis=0)
            unperm = unperm_ref[group - 1]
            tiles.append(_dot_split_lhs_rhs(unperm, v) if exact else _dot(unperm, v))
        return jnp.concatenate(tiles, axis=0)

    l0, l1, l2 = natural(l0_ref, 0, True), natural(l1_ref, 1, True), natural(l2_ref, 2, True)
    o0, o1, o2 = natural(o0_ref, 0, False), natural(o1_ref, 1, False), natural(o2_ref, 2, False)
    gate_logits = [[_dot(h[p], wg_ref[:, n * D_MODEL:(n + 1) * D_MODEL]) for n in range(N_BRANCH)]
                   for p in parts]
    mx = jnp.maximum(jnp.maximum(l0, l1), l2)
    e0, e1, e2 = jnp.exp(l0 - mx), jnp.exp(l1 - mx), jnp.exp(l2 - mx)
    y_a = (e0 * o0 + e1 * o1 + e2 * o2) / (e0 + e1 + e2)

    bc = bc_ref[0]
    u = bc[:, 0:BRANCH_W]
    prev = jnp.where(i == 0, 0.0, ubp_ref[0][:, 0:BRANCH_W])
    nxt = jnp.where(i == n_tiles - 1, 0.0, ubn_ref[0][:, 0:BRANCH_W])
    ext = jnp.concatenate([prev, u, nxt], axis=0)
    n_ext = tm + 2 * POOL_HALO
    f1 = ext + pltpu.roll(ext, 1, 0)
    f2 = pltpu.roll(f1, 1, 0) + pltpu.roll(f1, n_ext - 1, 0)
    f4 = pltpu.roll(f2, 2, 0) + pltpu.roll(f2, n_ext - 2, 0)
    f8 = pltpu.roll(f4, 4, 0) + pltpu.roll(f4, n_ext - 4, 0)
    lane_e = _lane_head(n_ext)
    win = jnp.where(lane_e == 0, f1, jnp.where(lane_e == 1, f2, jnp.where(lane_e == 2, f4, f8)))
    win = win[POOL_HALO:POOL_HALO + tm, :]
    pos = i * tm + lax.broadcasted_iota(jnp.int32, (tm, BRANCH_W), 0)
    half = jnp.where(lane_g == 0, POOL_HALF[0],
                     jnp.where(lane_g == 1, POOL_HALF[1], jnp.where(lane_g == 2, POOL_HALF[2], POOL_HALF[3])))
    cnt = (jnp.minimum(pos + half, seq) - jnp.maximum(pos - half, 0)).astype(F32)
    y_b = _dot((win / cnt - u).astype(BF16), poolw_ref[...]) * pools_ref[...]

    u_c = bc[:, BRANCH_W:2 * BRANCH_W]
    v_c = bc[:, 2 * BRANCH_W:3 * BRANCH_W]
    msv = jnp.mean(v_c * v_c, axis=-1, keepdims=True)
    v_n = (v_c * lax.rsqrt(msv + EPS) * sgg_ref[...]).astype(BF16)
    lane_c = _lane_head(CHUNK)
    sv_chunks = []
    for ch in range(tm // CHUNK):
        vch = v_n[ch * CHUNK:(ch + 1) * CHUNK, :]
        v_groups = jnp.concatenate([jnp.where(lane_c == g, vch, jnp.zeros_like(vch)) for g in range(SG_GROUPS)],
                                   axis=0)
        sv_chunks.append(_dot(sgw_ref[...], v_groups) + sgb_ref[...])
    y_c = u_c * jnp.concatenate(sv_chunks, axis=0)

    y_d = yd0_ref[0] + yd1_ref[0]

    ys = [(y * _silu(z[:, n * BRANCH_W:(n + 1) * BRANCH_W])).astype(BF16)
          for n, y in enumerate((y_a, y_b, y_c, y_d))]
    branch = [[_dot(ys[n][p], wb_ref[n]) for n in range(N_BRANCH)] for p in parts]
    merged = []
    for gl, br in zip(gate_logits, branch):
        acc = jax.nn.sigmoid(gl[0]) * br[0]
        for n in range(1, N_BRANCH):
            acc = acc + jax.nn.sigmoid(gl[n]) * br[n]
        merged.append(acc.astype(BF16))
    for p, m in zip(parts, merged):
        out_ref[0, p, :] = x[p] + _dot(m, wo_ref[...])


def _merge(x, attn, z, bc, yd0, yd1, lp, batch, seq):
    tm = MERGE_ROWS
    hb = tm // POOL_HALO
    last_halo = seq // POOL_HALO - 1
    row = lambda w: pl.BlockSpec((1, tm, w), lambda b, i: (b, i, 0))
    const2 = lambda shape: pl.BlockSpec(shape, lambda b, i: (0, 0))
    const3 = lambda shape: pl.BlockSpec(shape, lambda b, i: (0, 0, 0))
    assert tm % PERM_ROWS == 0
    in_specs = [row(D_MODEL), const2((1, D_MODEL))]
    for dil in ATTN_DILATIONS:
        slab = pl.BlockSpec((1, tm // PERM_ROWS, dil, PERM_ROWS // dil, BRANCH_W), lambda b, i: (b, i, 0, 0, 0))
        in_specs += [slab, slab]
    in_specs += [
        row(Z_W),
        pl.BlockSpec((1, POOL_HALO, BRANCH_W), lambda b, i: (b, jnp.maximum(i * hb - 1, 0), 0)),
        row(BC_W),
        pl.BlockSpec((1, POOL_HALO, BRANCH_W), lambda b, i: (b, jnp.minimum((i + 1) * hb, last_halo), 0)),
        row(BRANCH_W), row(BRANCH_W),
        const3((N_GROUPS - 1, PERM_ROWS, PERM_ROWS)),
        const2((BRANCH_W, BRANCH_W)), const2((1, BRANCH_W)), const2((1, BRANCH_W)),
        const2((CHUNK, SG_GROUPS * CHUNK)), const2((CHUNK, BRANCH_W)),
        const2((D_MODEL, N_BRANCH * D_MODEL)), const3((N_BRANCH, BRANCH_W, D_MODEL)),
        const2((D_MODEL, D_MODEL)),
    ]
    flat_attn = [a for pair in attn for a in pair]
    return pl.pallas_call(
        functools.partial(_merge_kernel, tm=tm, seq=seq),
        grid=(batch, seq // tm),
        in_specs=in_specs,
        out_specs=row(D_MODEL),
        out_shape=jax.ShapeDtypeStruct((batch, seq, D_MODEL), F32),
        compiler_params=_params("parallel", "parallel"),
        name="merge",
    )(x, lp["norm_g"], *flat_attn, z, bc, bc, bc, yd0, yd1, _residue_perms(True),
      lp["pool_w_bd"], lp["pool_scale"], lp["sg_norm_g"], lp["sg_w"], lp["sg_bias"],
      lp["w_gate"], lp["w_branch"], lp["w_out"])


def _block_diag(blocks):
    n, w, _ = blocks.shape
    eye = np.eye(n, dtype=np.float32)
    return jnp.einsum("gij,gh->gihj", blocks, eye).reshape(n * w, n * w)


def _layer_params(l, norm_g, w_in, q_norm_g, k_norm_g, pool_w, pool_scale, sg_norm_g, sg_w, sg_b,
                  mu_rkv, mu_lat, w0, w_up, a0, a_up, k_k, k_a, r_k, ln_g, ln_b, w_branch, w_out):
    w = w_in[l]
    seg = lambda lo, width: w[:, lo:lo + width]
    w_small = jnp.concatenate([
        seg(0, QKV_W), seg(3840, RKV_W), seg(4608, 2 * LAT_W), seg(2560, BRANCH_W), seg(3072, 2 * BRANCH_W),
        seg(2304, BRANCH_W), seg(2816, BRANCH_W), seg(3584, BRANCH_W), seg(4864, BRANCH_W)], axis=1)
    head_blocks = np.kron(np.eye(RWKV_HEADS, dtype=np.float32), np.ones((HEAD_DIM, HEAD_DIM), np.float32))
    zeros_lat = jnp.zeros((2, LAT_W // 2, BRANCH_W), F32)
    vec3 = lambda p: p[l][:, None, :]
    return {
        "norm_g": norm_g[l][None, :],
        "w_small": w_small.astype(BF16),
        "w_gate": w[:, PROJ_SMALL_W:].astype(BF16),
        "q_gain": jnp.tile(q_norm_g[l], HEADS_PER_GROUP)[None, :],
        "k_gain": jnp.tile(k_norm_g[l], HEADS_PER_GROUP)[None, :],
        "bd_mean": jnp.asarray(head_blocks / HEAD_DIM, BF16),
        "bd_ones": jnp.asarray(head_blocks, BF16),
        "pool_w_bd": _block_diag(pool_w[l]).astype(BF16),
        "pool_scale": pool_scale[l][None, :],
        "sg_norm_g": sg_norm_g[l][None, :],
        "sg_w": jnp.concatenate([sg_w[l][g] for g in range(SG_GROUPS)], axis=1).astype(BF16),
        "sg_bias": jnp.repeat(sg_b[l].T, HEAD_DIM, axis=1),
        "mu_rkv": vec3(mu_rkv), "mu_lat": vec3(mu_lat), "w0": vec3(w0), "a0": vec3(a0),
        "k_k": vec3(k_k), "k_a": vec3(k_a), "r_k": vec3(r_k),
        "w_up_pad": jnp.concatenate([w_up[l], zeros_lat], axis=1),
        "a_up_pad": jnp.concatenate([zeros_lat, a_up[l]], axis=1),
        "ln_g": ln_g[l][None, :], "ln_b": ln_b[l][None, :],
        "w_branch": w_branch[l].astype(BF16),
        "w_out": w_out[l].astype(BF16),
    }


def _layer(x, lp):
    batch, seq, _ = x.shape
    qkv, rkv, lat, bc, z = _proj(x.reshape(batch * seq, D_MODEL), lp)
    rkv = rkv.reshape(batch, seq, RKV_W)
    lat = lat.reshape(batch, seq, 2 * LAT_W)
    attn, yd0, yd1 = _mixers(qkv, rkv, lat, lp, batch, seq)
    return _merge(x, attn, z.reshape(batch, seq, Z_W), bc.reshape(batch, seq, BC_W), yd0, yd1, lp, batch, seq)


def kernel(x_prompt, x_sample, norm_g, w_in, q_norm_g, k_norm_g, pool_w, pool_scale, sg_norm_g, sg_w, sg_b,
           mu_rkv, mu_lat, w0, w_up, a0, a_up, k_k, k_a, r_k, ln_g, ln_b, w_branch, w_out):
    weights = (norm_g, w_in, q_norm_g, k_norm_g, pool_w, pool_scale, sg_norm_g, sg_w, sg_b,
               mu_rkv, mu_lat, w0, w_up, a0, a_up, k_k, k_a, r_k, ln_g, ln_b, w_branch, w_out)
    layers = [_layer_params(l, *weights) for l in range(norm_g.shape[0])]
    outs = []
    for x in (x_prompt, x_sample):
        for lp in layers:
            x = _layer(x, lp)
        outs.append(x)
    return tuple(outs)
```

```python
import functools

import jax
import jax.numpy as jnp
import numpy as np
from jax import lax
from jax.experimental import pallas as pl
from jax.experimental.pallas import tpu as pltpu

F32 = jnp.float32
BF16 = jnp.bfloat16

D_MODEL = 1024
BRANCH_W = 256
N_BRANCH = 4
EPS = 1e-6
NEG_INF = -1e30

HEAD_DIM = 64
ATTN_DILATIONS = (1, 4, 16)
N_GROUPS = 3
HEADS_PER_GROUP = 4
N_ATTN_HEADS = N_GROUPS * HEADS_PER_GROUP
ATTN_QKV = N_ATTN_HEADS * HEAD_DIM
HALF_KEYS = 64
Q_SUB = 128
K_WIN = Q_SUB + 2 * HALF_KEYS

POOL_HALF = (1, 2, 4, 8)
POOL_HALO = 8
CHUNK = 128
SG_GROUPS = 4

RWKV_HEADS = 4
RWKV_CHUNK = 64
RWKV_UNIT = 128
LAT_W = 128
GN_EPS = 64e-5
NEUMANN_DOUBLINGS = 5

QKV_W = 3 * ATTN_QKV
RKV_W = 3 * BRANCH_W
BC_W = 3 * BRANCH_W
Z_W = 4 * BRANCH_W
PROJ_SMALL_W = QKV_W + RKV_W + 2 * LAT_W + BC_W + Z_W

PROJ_ROWS = 512
MERGE_ROWS = 512
PERM_ROWS = 256
RWKV_ROWS = 512
V7X_VMEM_BYTES = 64 * 1024 * 1024
VMEM_LIMIT = V7X_VMEM_BYTES - 2 * 1024 * 1024


def _params(*sem):
    return pltpu.CompilerParams(dimension_semantics=sem, vmem_limit_bytes=VMEM_LIMIT)


def _dot(a, b):
    return jnp.dot(a, b, preferred_element_type=F32)


def _split_bf16(x):
    hi = x.astype(BF16)
    lo = (x - hi.astype(F32)).astype(BF16)
    return hi, lo


def _dot_split_lhs(x, m_bf16):
    hi, lo = _split_bf16(x)
    return _dot(hi, m_bf16) + _dot(lo, m_bf16)


def _dot_split_lhs_rhs(m_bf16, x):
    hi, lo = _split_bf16(x)
    return _dot(m_bf16, hi) + _dot(m_bf16, lo)


def _lane_head(rows):
    return lax.broadcasted_iota(jnp.int32, (rows, BRANCH_W), 1) // HEAD_DIM


def _proj_kernel(x_ref, g_ref, w_ref, qg_ref, kg_ref, perm_ref, qkv_ref, rkv_ref, lat_ref, bc_ref, z_ref):
    x = x_ref[...]
    ms = jnp.mean(x * x, axis=-1, keepdims=True)
    h = (x * lax.rsqrt(ms + EPS) * g_ref[...]).astype(BF16)
    n_chunks = 3 * N_GROUPS

    def project(c):
        return _dot(h, w_ref[:, c * BRANCH_W:(c + 1) * BRANCH_W])

    lane_h = _lane_head(x.shape[0])

    def normalize(c, y):
        if c < 2 * N_GROUPS:
            sq = y * y
            msq = jnp.zeros_like(sq)
            for hd in range(HEADS_PER_GROUP):
                mine = lane_h == hd
                tot = jnp.sum(jnp.where(mine, sq, 0.0), axis=-1, keepdims=True)
                msq = jnp.where(mine, tot * (1.0 / HEAD_DIM), msq)
            if c < N_GROUPS:
                y = y * lax.rsqrt(msq + EPS) * qg_ref[...] * (HEAD_DIM ** -0.5)
            else:
                y = y * lax.rsqrt(msq + EPS) * kg_ref[...]
        return y.astype(BF16)

    def store(c, yb):
        group = c % N_GROUPS
        if group > 0:
            perm = perm_ref[group - 1]
            yb = jnp.concatenate([_dot(perm, yb[t * PERM_ROWS:(t + 1) * PERM_ROWS]).astype(BF16)
                                  for t in range(yb.shape[0] // PERM_ROWS)], axis=0)
        qkv_ref[:, c * BRANCH_W:(c + 1) * BRANCH_W] = yb

    others = []
    o = QKV_W
    for ref, width in ((rkv_ref, RKV_W), (lat_ref, 2 * LAT_W), (bc_ref, BC_W), (z_ref, Z_W)):
        others.append((ref, o, width))
        o += width

    projected, normalized = {}, {}
    for step in range(n_chunks + 2):
        if step < n_chunks:
            projected[step] = project(step)
        elif others:
            ref, lo, width = others.pop(0)
            ref[...] = _dot(h, w_ref[:, lo:lo + width])
        if 1 <= step <= n_chunks:
            normalized[step - 1] = normalize(step - 1, projected.pop(step - 1))
        if step >= 2:
            store(step - 2, normalized.pop(step - 2))
    for ref, lo, width in others:
        ref[...] = _dot(h, w_ref[:, lo:lo + width])


def _residue_perms(transpose):
    mats = []
    for dil in ATTN_DILATIONS[1:]:
        per = PERM_ROWS // dil
        p = np.zeros((PERM_ROWS, PERM_ROWS), np.float32)
        old = np.arange(PERM_ROWS)
        p[(old % dil) * per + old // dil, old] = 1.0
        mats.append(p.T if transpose else p)
    return jnp.asarray(np.stack(mats), BF16)


def _proj(x2d, lp):
    rows = x2d.shape[0]
    tm = PROJ_ROWS
    assert tm % PERM_ROWS == 0
    const = lambda i: (0, 0)
    row = lambda i: (i, 0)
    widths = (QKV_W, RKV_W, 2 * LAT_W, BC_W, Z_W)
    dtypes = (BF16, F32, F32, F32, F32)
    return pl.pallas_call(
        _proj_kernel,
        grid=(rows // tm,),
        in_specs=[
            pl.BlockSpec((tm, D_MODEL), row),
            pl.BlockSpec((1, D_MODEL), const),
            pl.BlockSpec((D_MODEL, PROJ_SMALL_W), const),
            pl.BlockSpec((1, BRANCH_W), const),
            pl.BlockSpec((1, BRANCH_W), const),
            pl.BlockSpec((N_GROUPS - 1, PERM_ROWS, PERM_ROWS), lambda i: (0, 0, 0)),
        ],
        out_specs=[pl.BlockSpec((tm, w), row) for w in widths],
        out_shape=[jax.ShapeDtypeStruct((rows, w), dt) for w, dt in zip(widths, dtypes)],
        compiler_params=_params("parallel"),
        name="proj",
    )(x2d, lp["norm_g"], lp["w_small"], lp["q_gain"], lp["k_gain"], _residue_perms(False))


class _AttnPlan:
    def __init__(self, group, batch, seq, n_steps):
        self.group = group
        self.dil = ATTN_DILATIONS[group]
        self.seq_m = seq // self.dil
        self.per = PERM_ROWS // self.dil
        rows = batch * seq // n_steps
        self.res = max(1, rows // self.seq_m)
        self.tq = rows // self.res
        self.nq = self.seq_m // self.tq
        assert self.dil % self.res == 0 and batch * self.dil * self.nq == n_steps * self.res

    def decode(self, i):
        t = i * self.res
        return t // (self.dil * self.nq), ((t // self.nq) % self.dil) // self.res, t % self.nq

    def specs(self):
        per, res, tq = self.per, self.res, self.tq
        hb = tq // HALF_KEYS
        last_halo = self.seq_m // HALF_KEYS - 1
        prev = lambda iq: jnp.maximum(iq * hb - 1, 0)
        nxt = lambda iq: jnp.minimum((iq + 1) * hb, last_halo)

        def col(which):
            return which * N_GROUPS + self.group

        def main(which):
            def index(i):
                b, rb, iq = self.decode(i)
                return b, iq, rb, 0, col(which)
            return pl.BlockSpec((1, tq // per, res, per, BRANCH_W), index)

        def halo(which, pick):
            split = max(1, per // HALF_KEYS)

            def index(i):
                b, rb, iq = self.decode(i)
                h = pick(iq)
                return b, h // split, rb, h % split, col(which)
            if split == 1:
                return pl.BlockSpec((1, HALF_KEYS // per, res, per, BRANCH_W), index)
            return pl.BlockSpec((1, 1, res, HALF_KEYS, BRANCH_W), index)

        def out_index(i):
            b, rb, iq = self.decode(i)
            return b, iq, rb, 0, 0

        ins = [main(0), halo(1, prev), main(1), halo(1, nxt), halo(2, prev), main(2), halo(2, nxt),
               pl.BlockSpec((HEADS_PER_GROUP * Q_SUB, K_WIN), lambda i: (0, 0))]
        outs = [pl.BlockSpec((1, tq // per, res, per, BRANCH_W), out_index)] * 2
        return ins, outs

    def scratch(self):
        buf = lambda rows: pltpu.VMEM((self.res, rows, BRANCH_W), BF16)
        return [buf(self.tq + 2 * HALF_KEYS), buf(self.tq + 2 * HALF_KEYS)]


def _attn_gather(plan, refs, bufs):
    _, kp_ref, k_ref, kn_ref, vp_ref, v_ref, vn_ref = refs
    kbuf, vbuf = bufs

    def gather(dst, t, at, src):
        per = src.shape[3]
        for j in range(src.shape[1]):
            dst[t, at + j * per:at + (j + 1) * per, :] = src[0, j, t]

    for t in range(plan.res):
        gather(kbuf, t, 0, kp_ref)
        gather(kbuf, t, HALF_KEYS, k_ref)
        gather(kbuf, t, HALF_KEYS + plan.tq, kn_ref)
        gather(vbuf, t, 0, vp_ref)
        gather(vbuf, t, HALF_KEYS, v_ref)
        gather(vbuf, t, HALF_KEYS + plan.tq, vn_ref)


def _attn_blocks(plan, i, sub_blocks, q_ref, bufs, bias_ref, o_ref, lse_ref):
    kbuf, vbuf = bufs
    per_out = o_ref.shape[3]
    lane_h = _lane_head(Q_SUB)
    col = lax.broadcasted_iota(jnp.int32, (1, K_WIN), 1)
    _, _, iq = plan.decode(i)

    def scores(t, r0):
        if per_out >= Q_SUB:
            qb = q_ref[0, r0 // per_out, t, pl.ds(pl.multiple_of(r0 % per_out, Q_SUB), Q_SUB), :]
        else:
            qb = jnp.concatenate([q_ref[0, r0 // per_out + j, t] for j in range(Q_SUB // per_out)], axis=0)
        qs = jnp.concatenate(
            [jnp.where(lane_h == h, qb, jnp.zeros_like(qb)) for h in range(HEADS_PER_GROUP)], axis=0)
        return lax.dot_general(qs, kbuf[t, pl.ds(r0, K_WIN), :], (((1,), (1,)), ((), ())),
                               preferred_element_type=F32)

    def weigh(t, r0, s):
        kpos = iq * plan.tq + (r0 - HALF_KEYS) + col
        s = s + bias_ref[...] + jnp.where((kpos >= 0) & (kpos < plan.seq_m), 0.0, NEG_INF)
        m = jnp.max(s, axis=-1, keepdims=True)
        p = jnp.exp(s - m)
        l = jnp.sum(p, axis=-1, keepdims=True)
        return _dot(p.astype(BF16), vbuf[t, pl.ds(r0, K_WIN), :]), l, m

    def store(t, r0, pv, l, m):
        on = pv / l
        ls = m + jnp.log(l)
        o = jnp.zeros((Q_SUB, BRANCH_W), F32)
        lo = jnp.zeros((Q_SUB, BRANCH_W), F32)
        for h in range(HEADS_PER_GROUP):
            sel = lane_h == h
            o = jnp.where(sel, on[h * Q_SUB:(h + 1) * Q_SUB, :], o)
            lo = jnp.where(sel, ls[h * Q_SUB:(h + 1) * Q_SUB, :], lo)
        if per_out >= Q_SUB:
            at = pl.ds(pl.multiple_of(r0 % per_out, Q_SUB), Q_SUB)
            o_ref[0, r0 // per_out, t, at, :] = o.astype(o_ref.dtype)
            lse_ref[0, r0 // per_out, t, at, :] = lo
        for j in range(Q_SUB // per_out):
            o_ref[0, r0 // per_out + j, t] = o[j * per_out:(j + 1) * per_out, :].astype(o_ref.dtype)
            lse_ref[0, r0 // per_out + j, t] = lo[j * per_out:(j + 1) * per_out, :]

    scored = weighed = None
    for step in range(len(sub_blocks) + 2):
        if weighed is not None:
            store(*weighed)
        weighed = None if scored is None else (scored[0], scored[1]) + weigh(*scored)
        scored = sub_blocks[step] + (scores(*sub_blocks[step]),) if step < len(sub_blocks) else None
        if step < len(sub_blocks) + 1:
            yield


def _attn_bias(group):
    dil = ATTN_DILATIONS[group]
    i = np.arange(Q_SUB)[:, None]
    j = np.arange(K_WIN)[None, :]
    off = np.abs(j - HALF_KEYS - i).astype(np.float32)
    slopes = 2.0 ** (-8.0 * np.arange(1, N_ATTN_HEADS + 1, dtype=np.float32) / N_ATTN_HEADS)
    blocks = []
    for h in range(HEADS_PER_GROUP):
        slope = slopes[group * HEADS_PER_GROUP + h]
        blocks.append(np.where(off <= HALF_KEYS, -slope * (off * dil), NEG_INF))
    return jnp.asarray(np.concatenate(blocks, axis=0), F32)


def _softplus(y):
    return jnp.maximum(y, 0.0) + jnp.log(1.0 + jnp.exp(-jnp.abs(y)))


def _stack_heads(xb, lane_h):
    return jnp.concatenate(
        [jnp.where(lane_h == h, xb, jnp.zeros_like(xb)) for h in range(RWKV_HEADS)], axis=0)


def _nt(a, b):
    return lax.dot_general(a, b, (((1,), (1,)), ((), ())), preferred_element_type=F32)


N_RWKV_IN = 21
N_ATTN_IN = 8


def _mixer_kernel(*refs, batch, tc, plans):
    n_in = N_RWKV_IN + N_ATTN_IN * len(plans)
    n_out = 2 + 2 * len(plans)
    (rkvf_ref, rkvfh_ref, rkvb_ref, rkvbh_ref, latf_ref, latfh_ref, latb_ref, latbh_ref,
     mur_ref, mul_ref, w0_ref, wup_ref, a0_ref, aup_ref, kk_ref, ka_ref, rk_ref,
     lng_ref, lnb_ref, bdm_ref, bd1_ref) = refs[:N_RWKV_IN]
    attn_in = [refs[N_RWKV_IN + N_ATTN_IN * g:N_RWKV_IN + N_ATTN_IN * (g + 1)] for g in range(len(plans))]
    outf_ref, outb_ref = refs[n_in:n_in + 2]
    attn_out = [refs[n_in + 2 + 2 * g:n_in + 4 + 2 * g] for g in range(len(plans))]
    st_ref, r_s, k_s, v_s, kk_s, b_s, lw_s, o_s, y_s = refs[n_in + n_out:n_in + n_out + 9]
    attn_buf = [refs[n_in + n_out + 9 + 2 * g:n_in + n_out + 11 + 2 * g] for g in range(len(plans))]

    i = pl.program_id(0)
    c = RWKV_CHUNK
    unit = RWKV_UNIT
    n_units = tc // unit
    streams = [(d, b) for d in range(2) for b in range(batch)]
    ids = range(len(streams))
    back = [d == 1 for d, _ in streams]
    items = [(s, h) for s in ids for h in (reversed(range(unit // c)) if back[s] else range(unit // c))]
    wid = range(len(items))
    iback = [back[s] for s, _ in items]

    @pl.when(i == 0)
    def _():
        st_ref[...] = jnp.zeros_like(st_ref)

    lane_h = _lane_head(c)
    t_i = lax.broadcasted_iota(jnp.int32, (c, c), 0)
    s_i = lax.broadcasted_iota(jnp.int32, (c, c), 1)
    tau = lax.broadcasted_iota(jnp.int32, (c, BRANCH_W), 0)
    sig = lax.broadcasted_iota(jnp.int32, (c, BRANCH_W), 1) % c
    eye = sig == tau
    eye_b = jnp.where(eye, 1.0, 0.0).astype(BF16)
    tri_f = jnp.where(s_i <= t_i, 1.0, 0.0).astype(BF16)
    tri_b = jnp.where(s_i >= t_i, 1.0, 0.0).astype(BF16)
    strict = [(sig > tau) if bk else (sig < tau) for bk in iback]
    incl = [(sig >= tau) if bk else (sig <= tau) for bk in iback]
    bd1 = bd1_ref[...]
    bdm = bdm_ref[...]
    sm = lambda xb: _stack_heads(xb, lane_h)
    bf = lambda x: x.astype(BF16)
    top = lambda x: x[0:c]
    mid = lambda x: x[c:2 * c]
    rows_of = lambda x, k: x[k * unit:(k + 1) * unit]
    stack = lambda xs: jnp.concatenate(xs, axis=0)

    x_refs = [(rkvb_ref, rkvbh_ref, latb_ref, latbh_ref) if bk else (rkvf_ref, rkvfh_ref, latf_ref, latfh_ref)
              for bk in back]

    def unit_rows(j):
        start = [pl.multiple_of(((n_units - 1 - j) if bk else j) * unit, unit) for bk in back]
        return start, [pl.ds(st, unit) for st in start]

    def prepare(j, slot):
        start, rows = unit_rows(j)

        def mixed(s, x_ref, halo_ref, mu):
            d, b = streams[s]
            x = x_ref[b, rows[s], :]
            row = lax.broadcasted_iota(jnp.int32, x.shape, 0)
            if back[s]:
                near = x_ref[b, pl.ds(pl.multiple_of(jnp.minimum(start[s] + unit, tc - 8), 8), 8), :][0:1, :]
                edge = jnp.where(j == 0, jnp.where(i == 0, 0.0, halo_ref[b][0:1, :]), near)
                sh = jnp.where(row == unit - 1, edge, pltpu.roll(x, unit - 1, 0))
            else:
                near = x_ref[b, pl.ds(pl.multiple_of(jnp.maximum(start[s] - 8, 0), 8), 8), :][7:8, :]
                edge = jnp.where(j == 0, jnp.where(i == 0, 0.0, halo_ref[b][7:8, :]), near)
                sh = jnp.where(row == 0, edge, pltpu.roll(x, 1, 0))
            return x + (sh - x) * mu[d]

        xr = [mixed(s, x_refs[s][0], x_refs[s][1], mur_ref) for s in ids]
        xl = [mixed(s, x_refs[s][2], x_refs[s][3], mul_ref) for s in ids]
        th = [jnp.tanh(stack([xl[s] for s in ids if streams[s][0] == d])) for d in range(2)]
        yield
        w_log, a = [], []
        for d in range(2):
            xl_d = stack([xl[s] for s in ids if streams[s][0] == d])
            wl = -_softplus(-(w0_ref[d] + _dot(bf(th[d]), bf(wup_ref[d])))) - 0.5
            al = jax.nn.sigmoid(a0_ref[d] + _dot(bf(xl_d), bf(aup_ref[d])))
            w_log += [rows_of(wl, k) for k in range(batch)]
            a += [rows_of(al, k) for k in range(batch)]
        k = [xr[s][:, BRANCH_W:2 * BRANCH_W] for s in ids]
        kk = [k[s] * kk_ref[streams[s][0]] for s in ids]
        yield
        kk_sq = _dot(bf(stack([kk[s] * kk[s] for s in ids])), bd1)
        yield
        for s in ids:
            kks = kk[s] * lax.rsqrt(rows_of(kk_sq, s) + 1e-12)
            r_s[slot, s] = xr[s][:, 0:BRANCH_W]
            k_s[slot, s] = k[s] * (1.0 + (a[s] - 1.0) * ka_ref[streams[s][0]])
            v_s[slot, s] = xr[s][:, 2 * BRANCH_W:3 * BRANCH_W]
            kk_s[slot, s] = kks
            b_s[slot, s] = kks * a[s]
            lw_s[slot, s] = -jnp.exp(w_log[s])

    def advance(slot):
        ids = wid
        back = iback
        part = lambda ref: [ref[slot, s, h * c:(h + 1) * c, :] for s, h in items]
        lw, r, k2, v, kk, bq = part(lw_s), part(r_s), part(k_s), part(v_s), part(kk_s), part(b_s)

        def cumsum(s):
            lw_hi, lw_lo = _split_bf16(lw[s])
            tri = tri_b if back[s] else tri_f
            return _dot(tri, lw_hi) + _dot(tri, lw_lo)

        cumi = [cumsum(s) for s in ids]
        yield
        tot = [cumi[s][0:1, :] if back[s] else cumi[s][c - 1:c, :] for s in ids]
        e_m = [jnp.exp(-cumi[s]) for s in ids]
        e_t = [jnp.exp(tot[s] - cumi[s]) for s in ids]
        pt = [bf(-kk[s] * jnp.exp(cumi[s] - lw[s])) for s in ids]
        rt = [r[s] * jnp.exp(cumi[s]) for s in ids]
        pr = [stack([pt[s], bf(rt[s])]) for s in ids]
        ab = [_nt(pr[s], sm(bf(bq[s] * e_m[s]))) for s in ids]
        ak = [_nt(pr[s], sm(bf(k2[s] * e_m[s]))) for s in ids]
        yield
        a_pb = [jnp.where(strict[s], top(ab[s]), 0.0) for s in ids]
        a_rb = [bf(jnp.where(incl[s], mid(ab[s]), 0.0)) for s in ids]
        a_pk = [bf(jnp.where(strict[s], top(ak[s]), 0.0)) for s in ids]
        a_rk = [bf(jnp.where(incl[s], mid(ak[s]), 0.0)) for s in ids]
        l_b = [bf(_nt(eye_b, sm(bf(bq[s] * e_t[s])))) for s in ids]
        l_k = [bf(_nt(eye_b, sm(bf(k2[s] * e_t[s])))) for s in ids]
        yield

        tinv = [jnp.where(eye, 1.0, a_pb[s]) for s in ids]
        apow = [bf(a_pb[s]) for s in ids]
        a2 = [_dot(apow[s], sm(apow[s])) for s in ids]
        av = [_dot(stack([a_pk[s], a_rk[s], l_k[s]]), sm(bf(v[s]))) for s in ids]
        yield
        for _ in range(NEUMANN_DOUBLINGS - 1):
            apow = [bf(a2[s]) for s in ids]
            both = [_dot(stack([bf(tinv[s]), apow[s]]), sm(apow[s])) for s in ids]
            yield
            tinv = [tinv[s] + top(both[s]) for s in ids]
            a2 = [mid(both[s]) for s in ids]
        tinv = [tinv[s] + _dot(bf(tinv[s]), sm(bf(a2[s]))) for s in ids]
        yield
        tb = [bf(t) for t in tinv]

        ah = [_dot(tb[s], sm(pt[s])) for s in ids]
        uh = [_dot(tb[s], sm(bf(top(av[s])))) for s in ids]
        yield
        lhs = [stack([a_rb[s], l_b[s]]) for s in ids]
        o_a = [_dot(lhs[s], sm(bf(ah[s]))) for s in ids]
        o_u = [_dot(lhs[s], sm(bf(uh[s]))) for s in ids]
        yield
        rh = [rt[s] + top(o_a[s]) for s in ids]
        mt = [jnp.where(eye, jnp.exp(tot[s]), 0.0) + mid(o_a[s]) for s in ids]
        oh = [mid(av[s]) + top(o_u[s]) for s in ids]
        gt = [mid(o_u[s]) + av[s][2 * c:3 * c] for s in ids]

        bonus = _dot(bf(stack([r[w] * k2[w] * rk_ref[streams[items[w][0]][0]] for w in ids])), bd1) * stack(v)
        for w, (s, h) in enumerate(items):
            y_s[slot, s, h * c:(h + 1) * c, :] = bonus[w * c:(w + 1) * c]
        state = [st_ref[s] for s in range(len(streams))]
        for step in range(unit // c):
            now = [w for w in ids if w % (unit // c) == step]
            out = {w: _dot(stack([bf(rh[w]), bf(mt[w])]), sm(bf(state[items[w][0]]))) for w in now}
            yield
            for w in now:
                s, h = items[w]
                state[s] = mid(out[w]) + gt[w]
                o_s[slot, s, h * c:(h + 1) * c, :] = top(out[w]) + oh[w]
        for s in range(len(streams)):
            st_ref[s] = state[s]

    def finish(j, slot):
        _, rows = unit_rows(j)
        o = stack([o_s[slot, s] for s in ids])
        mean = _dot_split_lhs(o, bdm)
        yield
        xc = o - mean
        var = _dot(bf(xc * xc), bdm)
        yield
        y = xc * lax.rsqrt(var + GN_EPS) * lng_ref[...] + lnb_ref[...] + stack([y_s[slot, s] for s in ids])
        for s, (d, b) in enumerate(streams):
            out_ref = outb_ref if d == 1 else outf_ref
            out_ref[b, rows[s], :] = rows_of(y, s)

    def interleave(*phases):
        live = list(phases)
        while live:
            for phase in list(live):
                if next(phase, "done") == "done":
                    live.remove(phase)

    def attend(j):
        phases = []
        for plan, ins, outs, bufs in zip(plans, attn_in, attn_out, attn_buf):
            assert plan.tq % (Q_SUB * n_units) == 0
            blocks = [(t, pl.multiple_of((j + n_units * u) * Q_SUB, Q_SUB))
                      for t in range(plan.res) for u in range(plan.tq // Q_SUB // n_units)]
            phases.append(_attn_blocks(plan, i, blocks, ins[0], bufs, ins[7], outs[0], outs[1]))
        return phases

    o_s[0] = jnp.zeros(o_s.shape[1:], F32)
    y_s[0] = jnp.zeros(y_s.shape[1:], F32)
    for plan, ins, bufs in zip(plans, attn_in, attn_buf):
        _attn_gather(plan, ins[:7], bufs)
    interleave(prepare(0, 0))

    def pipeline(j, carry):
        interleave(finish(jnp.maximum(j - 1, 0), 0),
                   advance(0),
                   prepare(jnp.minimum(j + 1, n_units - 1), 0),
                   *attend(j))
        return carry

    lax.fori_loop(0, n_units, pipeline, 0)
    interleave(finish(n_units - 1, 0))


def _mixers(qkv, rkv, lat, lp, batch, seq):
    tc = min(RWKV_ROWS, seq)
    nt = seq // tc
    hb = tc // 8
    last_halo = seq // 8 - 1
    fwd = lambda i: i
    bwd = lambda i: nt - 1 - i
    fwd_halo = lambda i: jnp.maximum(i * hb - 1, 0)
    bwd_halo = lambda i: jnp.minimum((bwd(i) + 1) * hb, last_halo)
    full3 = lambda shape: pl.BlockSpec(shape, lambda i: (0, 0, 0))
    full2 = lambda shape: pl.BlockSpec(shape, lambda i: (0, 0))
    n_streams = 2 * batch
    plans = [_AttnPlan(g, batch, seq, nt) for g in range(N_GROUPS)]
    in_specs = [
        pl.BlockSpec((batch, tc, RKV_W), lambda i: (0, fwd(i), 0)),
        pl.BlockSpec((batch, 8, RKV_W), lambda i: (0, fwd_halo(i), 0)),
        pl.BlockSpec((batch, tc, RKV_W), lambda i: (0, bwd(i), 0)),
        pl.BlockSpec((batch, 8, RKV_W), lambda i: (0, bwd_halo(i), 0)),
        pl.BlockSpec((batch, tc, LAT_W), lambda i: (0, fwd(i), 0)),
        pl.BlockSpec((batch, 8, LAT_W), lambda i: (0, fwd_halo(i), 0)),
        pl.BlockSpec((batch, tc, LAT_W), lambda i: (0, bwd(i), 1)),
        pl.BlockSpec((batch, 8, LAT_W), lambda i: (0, bwd_halo(i), 1)),
        full3((2, 1, RKV_W)), full3((2, 1, LAT_W)), full3((2, 1, BRANCH_W)),
        full3((2, LAT_W, BRANCH_W)), full3((2, 1, BRANCH_W)), full3((2, LAT_W, BRANCH_W)),
        full3((2, 1, BRANCH_W)), full3((2, 1, BRANCH_W)), full3((2, 1, BRANCH_W)),
        full2((1, BRANCH_W)), full2((1, BRANCH_W)),
        full2((BRANCH_W, BRANCH_W)), full2((BRANCH_W, BRANCH_W)),
    ]
    assert len(in_specs) == N_RWKV_IN
    operands = [rkv, rkv, rkv, rkv, lat, lat, lat, lat, lp["mu_rkv"], lp["mu_lat"], lp["w0"], lp["w_up_pad"],
                lp["a0"], lp["a_up_pad"], lp["k_k"], lp["k_a"], lp["r_k"], lp["ln_g"], lp["ln_b"],
                lp["bd_mean"], lp["bd_ones"]]
    rwkv_out = jax.ShapeDtypeStruct((batch, seq, BRANCH_W), F32)
    out_specs = [pl.BlockSpec((batch, tc, BRANCH_W), lambda i: (0, fwd(i), 0)),
                 pl.BlockSpec((batch, tc, BRANCH_W), lambda i: (0, bwd(i), 0))]
    out_shape = [rwkv_out, rwkv_out]
    scratch = ([pltpu.VMEM((n_streams, RWKV_CHUNK, BRANCH_W), F32)]
               + [pltpu.VMEM((1, n_streams, RWKV_UNIT, BRANCH_W), F32)] * 8)
    for plan in plans:
        ins, outs = plan.specs()
        assert len(ins) == N_ATTN_IN
        view = qkv.reshape(batch, seq // PERM_ROWS, plan.dil, plan.per, QKV_W)
        in_specs += ins
        operands += [view] * 7 + [_attn_bias(plan.group)]
        out_specs += outs
        dims = (batch, seq // PERM_ROWS, plan.dil, plan.per, BRANCH_W)
        out_shape += [jax.ShapeDtypeStruct(dims, BF16), jax.ShapeDtypeStruct(dims, F32)]
        scratch += plan.scratch()
    res = pl.pallas_call(
        functools.partial(_mixer_kernel, batch=batch, tc=tc, plans=plans),
        grid=(nt,),
        in_specs=in_specs,
        out_specs=out_specs,
        out_shape=out_shape,
        scratch_shapes=scratch,
        compiler_params=_params("arbitrary"),
        name="mixers",
    )(*operands)
    attn = [(res[2 + 2 * g], res[3 + 2 * g]) for g in range(N_GROUPS)]
    return attn, res[0], res[1]


def _silu(z):
    return z * jax.nn.sigmoid(z)


def _merge_kernel(x_ref, g_ref, o0_ref, l0_ref, o1_ref, l1_ref, o2_ref, l2_ref, z_ref,
                  ubp_ref, bc_ref, ubn_ref, yd0_ref, yd1_ref,
                  unperm_ref, poolw_ref, pools_ref, sgg_ref, sgw_ref, sgb_ref, wg_ref, wb_ref, wo_ref,
                  out_ref, *, tm, seq):
    i = pl.program_id(1)
    n_tiles = seq // tm
    x = x_ref[0]
    ms = jnp.mean(x * x, axis=-1, keepdims=True)
    h = (x * lax.rsqrt(ms + EPS) * g_ref[...]).astype(BF16)
    z = z_ref[0]
    lane_g = _lane_head(tm)
    parts = [slice(p * PERM_ROWS, (p + 1) * PERM_ROWS) for p in range(tm // PERM_ROWS)]

    def natural(ref, group, exact):
        tiles = []
        for t in range(tm // PERM_ROWS):
            if group == 0:
                tiles.append(ref[0, t, 0].astype(F32))
                continue
            v = jnp.concatenate([ref[0, t, r] for r in range(ATTN_DILATIONS[group])], axis=0)
            unperm = unperm_ref[group - 1]
            tiles.append(_dot_split_lhs_rhs(unperm, v) if exact else _dot(unperm, v))
        return jnp.concatenate(tiles, axis=0)

    l0, l1, l2 = natural(l0_ref, 0, True), natural(l1_ref, 1, True), natural(l2_ref, 2, True)
    o0, o1, o2 = natural(o0_ref, 0, False), natural(o1_ref, 1, False), natural(o2_ref, 2, False)
    gate_logits = [[_dot(h[p], wg_ref[:, n * D_MODEL:(n + 1) * D_MODEL]) for n in range(N_BRANCH)]
                   for p in parts]
    mx = jnp.maximum(jnp.maximum(l0, l1), l2)
    e0, e1, e2 = jnp.exp(l0 - mx), jnp.exp(l1 - mx), jnp.exp(l2 - mx)
    y_a = (e0 * o0 + e1 * o1 + e2 * o2) / (e0 + e1 + e2)

    bc = bc_ref[0]
    u = bc[:, 0:BRANCH_W]
    prev = jnp.where(i == 0, 0.0, ubp_ref[0][:, 0:BRANCH_W])
    nxt = jnp.where(i == n_tiles - 1, 0.0, ubn_ref[0][:, 0:BRANCH_W])
    ext = jnp.concatenate([prev, u, nxt], axis=0)
    n_ext = tm + 2 * POOL_HALO
    f1 = ext + pltpu.roll(ext, 1, 0)
    f2 = pltpu.roll(f1, 1, 0) + pltpu.roll(f1, n_ext - 1, 0)
    f4 = pltpu.roll(f2, 2, 0) + pltpu.roll(f2, n_ext - 2, 0)
    f8 = pltpu.roll(f4, 4, 0) + pltpu.roll(f4, n_ext - 4, 0)
    lane_e = _lane_head(n_ext)
    win = jnp.where(lane_e == 0, f1, jnp.where(lane_e == 1, f2, jnp.where(lane_e == 2, f4, f8)))
    win = win[POOL_HALO:POOL_HALO + tm, :]
    pos = i * tm + lax.broadcasted_iota(jnp.int32, (tm, BRANCH_W), 0)
    half = jnp.where(lane_g == 0, POOL_HALF[0],
                     jnp.where(lane_g == 1, POOL_HALF[1], jnp.where(lane_g == 2, POOL_HALF[2], POOL_HALF[3])))
    cnt = (jnp.minimum(pos + half, seq) - jnp.maximum(pos - half, 0)).astype(F32)
    y_b = _dot((win / cnt - u).astype(BF16), poolw_ref[...]) * pools_ref[...]

    u_c = bc[:, BRANCH_W:2 * BRANCH_W]
    v_c = bc[:, 2 * BRANCH_W:3 * BRANCH_W]
    msv = jnp.mean(v_c * v_c, axis=-1, keepdims=True)
    v_n = (v_c * lax.rsqrt(msv + EPS) * sgg_ref[...]).astype(BF16)
    lane_c = _lane_head(CHUNK)
    sv_chunks = []
    for ch in range(tm // CHUNK):
        vch = v_n[ch * CHUNK:(ch + 1) * CHUNK, :]
        v_groups = jnp.concatenate([jnp.where(lane_c == g, vch, jnp.zeros_like(vch)) for g in range(SG_GROUPS)],
                                   axis=0)
        sv_chunks.append(_dot(sgw_ref[...], v_groups) + sgb_ref[...])
    y_c = u_c * jnp.concatenate(sv_chunks, axis=0)

    y_d = yd0_ref[0] + yd1_ref[0]

    ys = [(y * _silu(z[:, n * BRANCH_W:(n + 1) * BRANCH_W])).astype(BF16)
          for n, y in enumerate((y_a, y_b, y_c, y_d))]
    branch = [[_dot(ys[n][p], wb_ref[n]) for n in range(N_BRANCH)] for p in parts]
    merged = []
    for gl, br in zip(gate_logits, branch):
        acc = jax.nn.sigmoid(gl[0]) * br[0]
        for n in range(1, N_BRANCH):
            acc = acc + jax.nn.sigmoid(gl[n]) * br[n]
        merged.append(acc.astype(BF16))
    for p, m in zip(parts, merged):
        out_ref[0, p, :] = x[p] + _dot(m, wo_ref[...])


def _merge(x, attn, z, bc, yd0, yd1, lp, batch, seq):
    tm = MERGE_ROWS
    hb = tm // POOL_HALO
    last_halo = seq // POOL_HALO - 1
    row = lambda w: pl.BlockSpec((1, tm, w), lambda b, i: (b, i, 0))
    const2 = lambda shape: pl.BlockSpec(shape, lambda b, i: (0, 0))
    const3 = lambda shape: pl.BlockSpec(shape, lambda b, i: (0, 0, 0))
    assert tm % PERM_ROWS == 0
    in_specs = [row(D_MODEL), const2((1, D_MODEL))]
    for dil in ATTN_DILATIONS:
        slab = pl.BlockSpec((1, tm // PERM_ROWS, dil, PERM_ROWS // dil, BRANCH_W), lambda b, i: (b, i, 0, 0, 0))
        in_specs += [slab, slab]
    in_specs += [
        row(Z_W),
        pl.BlockSpec((1, POOL_HALO, BRANCH_W), lambda b, i: (b, jnp.maximum(i * hb - 1, 0), 0)),
        row(BC_W),
        pl.BlockSpec((1, POOL_HALO, BRANCH_W), lambda b, i: (b, jnp.minimum((i + 1) * hb, last_halo), 0)),
        row(BRANCH_W), row(BRANCH_W),
        const3((N_GROUPS - 1, PERM_ROWS, PERM_ROWS)),
        const2((BRANCH_W, BRANCH_W)), const2((1, BRANCH_W)), const2((1, BRANCH_W)),
        const2((CHUNK, SG_GROUPS * CHUNK)), const2((CHUNK, BRANCH_W)),
        const2((D_MODEL, N_BRANCH * D_MODEL)), const3((N_BRANCH, BRANCH_W, D_MODEL)),
        const2((D_MODEL, D_MODEL)),
    ]
    flat_attn = [a for pair in attn for a in pair]
    return pl.pallas_call(
        functools.partial(_merge_kernel, tm=tm, seq=seq),
        grid=(batch, seq // tm),
        in_specs=in_specs,
        out_specs=row(D_MODEL),
        out_shape=jax.ShapeDtypeStruct((batch, seq, D_MODEL), F32),
        compiler_params=_params("parallel", "parallel"),
        name="merge",
    )(x, lp["norm_g"], *flat_attn, z, bc, bc, bc, yd0, yd1, _residue_perms(True),
      lp["pool_w_bd"], lp["pool_scale"], lp["sg_norm_g"], lp["sg_w"], lp["sg_bias"],
      lp["w_gate"], lp["w_branch"], lp["w_out"])


def _block_diag(blocks):
    n, w, _ = blocks.shape
    eye = np.eye(n, dtype=np.float32)
    return jnp.einsum("gij,gh->gihj", blocks, eye).reshape(n * w, n * w)


def _layer_params(l, norm_g, w_in, q_norm_g, k_norm_g, pool_w, pool_scale, sg_norm_g, sg_w, sg_b,
                  mu_rkv, mu_lat, w0, w_up, a0, a_up, k_k, k_a, r_k, ln_g, ln_b, w_branch, w_out):
    w = w_in[l]
    seg = lambda lo, width: w[:, lo:lo + width]
    w_small = jnp.concatenate([
        seg(0, QKV_W), seg(3840, RKV_W), seg(4608, 2 * LAT_W), seg(2560, BRANCH_W), seg(3072, 2 * BRANCH_W),
        seg(2304, BRANCH_W), seg(2816, BRANCH_W), seg(3584, BRANCH_W), seg(4864, BRANCH_W)], axis=1)
    head_blocks = np.kron(np.eye(RWKV_HEADS, dtype=np.float32), np.ones((HEAD_DIM, HEAD_DIM), np.float32))
    zeros_lat = jnp.zeros((2, LAT_W // 2, BRANCH_W), F32)
    vec3 = lambda p: p[l][:, None, :]
    return {
        "norm_g": norm_g[l][None, :],
        "w_small": w_small.astype(BF16),
        "w_gate": w[:, PROJ_SMALL_W:].astype(BF16),
        "q_gain": jnp.tile(q_norm_g[l], HEADS_PER_GROUP)[None, :],
        "k_gain": jnp.tile(k_norm_g[l], HEADS_PER_GROUP)[None, :],
        "bd_mean": jnp.asarray(head_blocks / HEAD_DIM, BF16),
        "bd_ones": jnp.asarray(head_blocks, BF16),
        "pool_w_bd": _block_diag(pool_w[l]).astype(BF16),
        "pool_scale": pool_scale[l][None, :],
        "sg_norm_g": sg_norm_g[l][None, :],
        "sg_w": jnp.concatenate([sg_w[l][g] for g in range(SG_GROUPS)], axis=1).astype(BF16),
        "sg_bias": jnp.repeat(sg_b[l].T, HEAD_DIM, axis=1),
        "mu_rkv": vec3(mu_rkv), "mu_lat": vec3(mu_lat), "w0": vec3(w0), "a0": vec3(a0),
        "k_k": vec3(k_k), "k_a": vec3(k_a), "r_k": vec3(r_k),
        "w_up_pad": jnp.concatenate([w_up[l], zeros_lat], axis=1),
        "a_up_pad": jnp.concatenate([zeros_lat, a_up[l]], axis=1),
        "ln_g": ln_g[l][None, :], "ln_b": ln_b[l][None, :],
        "w_branch": w_branch[l].astype(BF16),
        "w_out": w_out[l].astype(BF16),
    }


def _layer(x, lp):
    batch, seq, _ = x.shape
    qkv, rkv, lat, bc, z = _proj(x.reshape(batch * seq, D_MODEL), lp)
    rkv = rkv.reshape(batch, seq, RKV_W)
    lat = lat.reshape(batch, seq, 2 * LAT_W)
    attn, yd0, yd1 = _mixers(qkv, rkv, lat, lp, batch, seq)
    return _merge(x, attn, z.reshape(batch, seq, Z_W), bc.reshape(batch, seq, BC_W), yd0, yd1, lp, batch, seq)


def kernel(x_prompt, x_sample, norm_g, w_in, q_norm_g, k_norm_g, pool_w, pool_scale, sg_norm_g, sg_w, sg_b,
           mu_rkv, mu_lat, w0, w_up, a0, a_up, k_k, k_a, r_k, ln_g, ln_b, w_branch, w_out):
    weights = (norm_g, w_in, q_norm_g, k_norm_g, pool_w, pool_scale, sg_norm_g, sg_w, sg_b,
               mu_rkv, mu_lat, w0, w_up, a0, a_up, k_k, k_a, r_k, ln_g, ln_b, w_branch, w_out)
    layers = [_layer_params(l, *weights) for l in range(norm_g.shape[0])]
    outs = []
    for x in (x_prompt, x_sample):
        for lp in layers:
            x = _layer(x, lp)
        outs.append(x)
    return tuple(outs)
```

```python
import functools

import jax
import jax.numpy as jnp
import numpy as np
from jax import lax
from jax.experimental import pallas as pl
from jax.experimental.pallas import tpu as pltpu

F32 = jnp.float32
BF16 = jnp.bfloat16

D_MODEL = 1024
BRANCH_W = 256
N_BRANCH = 4
EPS = 1e-6
NEG_INF = -1e30

HEAD_DIM = 64
ATTN_DILATIONS = (1, 4, 16)
N_GROUPS = 3
HEADS_PER_GROUP = 4
N_ATTN_HEADS = N_GROUPS * HEADS_PER_GROUP
ATTN_QKV = N_ATTN_HEADS * HEAD_DIM
HALF_KEYS = 64
Q_SUB = 128
K_WIN = Q_SUB + 2 * HALF_KEYS

POOL_HALF = (1, 2, 4, 8)
POOL_HALO = 8
CHUNK = 128
SG_GROUPS = 4

RWKV_HEADS = 4
RWKV_CHUNK = 64
RWKV_UNIT = 128
LAT_W = 128
GN_EPS = 64e-5
NEUMANN_DOUBLINGS = 5

QKV_W = 3 * ATTN_QKV
RKV_W = 3 * BRANCH_W
BC_W = 3 * BRANCH_W
Z_W = 4 * BRANCH_W
PROJ_SMALL_W = QKV_W + RKV_W + 2 * LAT_W + BC_W + Z_W

PROJ_ROWS = 1024
MERGE_ROWS = 512
PERM_ROWS = 256
RWKV_ROWS = 512
V7X_VMEM_BYTES = 64 * 1024 * 1024
VMEM_LIMIT = V7X_VMEM_BYTES - 2 * 1024 * 1024


def _params(*sem):
    return pltpu.CompilerParams(dimension_semantics=sem, vmem_limit_bytes=VMEM_LIMIT)


def _dot(a, b):
    return jnp.dot(a, b, preferred_element_type=F32)


def _split_bf16(x):
    hi = x.astype(BF16)
    lo = (x - hi.astype(F32)).astype(BF16)
    return hi, lo


def _dot_split_lhs(x, m_bf16):
    hi, lo = _split_bf16(x)
    return _dot(hi, m_bf16) + _dot(lo, m_bf16)


def _dot_split_lhs_rhs(m_bf16, x):
    hi, lo = _split_bf16(x)
    return _dot(m_bf16, hi) + _dot(m_bf16, lo)


def _lane_head(rows):
    return lax.broadcasted_iota(jnp.int32, (rows, BRANCH_W), 1) // HEAD_DIM


def _proj_kernel(x_ref, g_ref, w_ref, qg_ref, kg_ref, perm_ref, qkv_ref, rkv_ref, lat_ref, bc_ref, z_ref):
    x = x_ref[...]
    ms = jnp.mean(x * x, axis=-1, keepdims=True)
    h = (x * lax.rsqrt(ms + EPS) * g_ref[...]).astype(BF16)
    n_chunks = 3 * N_GROUPS

    def project(c):
        return _dot(h, w_ref[:, c * BRANCH_W:(c + 1) * BRANCH_W])

    lane_h = _lane_head(x.shape[0])

    def normalize(c, y):
        if c < 2 * N_GROUPS:
            sq = y * y
            msq = jnp.zeros_like(sq)
            for hd in range(HEADS_PER_GROUP):
                mine = lane_h == hd
                tot = jnp.sum(jnp.where(mine, sq, 0.0), axis=-1, keepdims=True)
                msq = jnp.where(mine, tot * (1.0 / HEAD_DIM), msq)
            if c < N_GROUPS:
                y = y * lax.rsqrt(msq + EPS) * qg_ref[...] * (HEAD_DIM ** -0.5)
            else:
                y = y * lax.rsqrt(msq + EPS) * kg_ref[...]
        return y.astype(BF16)

    def store(c, yb):
        group = c % N_GROUPS
        if group > 0:
            perm = perm_ref[group - 1]
            yb = jnp.concatenate([_dot(perm, yb[t * PERM_ROWS:(t + 1) * PERM_ROWS]).astype(BF16)
                                  for t in range(yb.shape[0] // PERM_ROWS)], axis=0)
        qkv_ref[:, c * BRANCH_W:(c + 1) * BRANCH_W] = yb

    others = []
    o = QKV_W
    for ref, width in ((rkv_ref, RKV_W), (lat_ref, 2 * LAT_W), (bc_ref, BC_W), (z_ref, Z_W)):
        others.append((ref, o, width))
        o += width

    projected, normalized = {}, {}
    for step in range(n_chunks + 2):
        if step < n_chunks:
            projected[step] = project(step)
        elif others:
            ref, lo, width = others.pop(0)
            ref[...] = _dot(h, w_ref[:, lo:lo + width])
        if 1 <= step <= n_chunks:
            normalized[step - 1] = normalize(step - 1, projected.pop(step - 1))
        if step >= 2:
            store(step - 2, normalized.pop(step - 2))
    for ref, lo, width in others:
        ref[...] = _dot(h, w_ref[:, lo:lo + width])


def _residue_perms(transpose):
    mats = []
    for dil in ATTN_DILATIONS[1:]:
        per = PERM_ROWS // dil
        p = np.zeros((PERM_ROWS, PERM_ROWS), np.float32)
        old = np.arange(PERM_ROWS)
        p[(old % dil) * per + old // dil, old] = 1.0
        mats.append(p.T if transpose else p)
    return jnp.asarray(np.stack(mats), BF16)


def _proj(x2d, lp):
    rows = x2d.shape[0]
    tm = PROJ_ROWS
    assert tm % PERM_ROWS == 0
    const = lambda i: (0, 0)
    row = lambda i: (i, 0)
    widths = (QKV_W, RKV_W, 2 * LAT_W, BC_W, Z_W)
    dtypes = (BF16, F32, F32, F32, F32)
    return pl.pallas_call(
        _proj_kernel,
        grid=(rows // tm,),
        in_specs=[
            pl.BlockSpec((tm, D_MODEL), row),
            pl.BlockSpec((1, D_MODEL), const),
            pl.BlockSpec((D_MODEL, PROJ_SMALL_W), const, pipeline_mode=pl.Buffered(1)),
            pl.BlockSpec((1, BRANCH_W), const),
            pl.BlockSpec((1, BRANCH_W), const),
            pl.BlockSpec((N_GROUPS - 1, PERM_ROWS, PERM_ROWS), lambda i: (0, 0, 0)),
        ],
        out_specs=[pl.BlockSpec((tm, w), row) for w in widths],
        out_shape=[jax.ShapeDtypeStruct((rows, w), dt) for w, dt in zip(widths, dtypes)],
        compiler_params=_params("parallel"),
        name="proj",
    )(x2d, lp["norm_g"], lp["w_small"], lp["q_gain"], lp["k_gain"], _residue_perms(False))


class _AttnPlan:
    def __init__(self, group, batch, seq, n_steps):
        self.group = group
        self.dil = ATTN_DILATIONS[group]
        self.seq_m = seq // self.dil
        self.per = PERM_ROWS // self.dil
        rows = batch * seq // n_steps
        self.res = max(1, rows // self.seq_m)
        self.tq = rows // self.res
        self.nq = self.seq_m // self.tq
        assert self.dil % self.res == 0 and batch * self.dil * self.nq == n_steps * self.res

    def decode(self, i):
        t = i * self.res
        return t // (self.dil * self.nq), ((t // self.nq) % self.dil) // self.res, t % self.nq

    def specs(self):
        per, res, tq = self.per, self.res, self.tq
        hb = tq // HALF_KEYS
        last_halo = self.seq_m // HALF_KEYS - 1
        prev = lambda iq: jnp.maximum(iq * hb - 1, 0)
        nxt = lambda iq: jnp.minimum((iq + 1) * hb, last_halo)

        def col(which):
            return which * N_GROUPS + self.group

        def main(which):
            def index(i):
                b, rb, iq = self.decode(i)
                return b, iq, rb, 0, col(which)
            return pl.BlockSpec((1, tq // per, res, per, BRANCH_W), index)

        def halo(which, pick):
            split = max(1, per // HALF_KEYS)

            def index(i):
                b, rb, iq = self.decode(i)
                h = pick(iq)
                return b, h // split, rb, h % split, col(which)
            if split == 1:
                return pl.BlockSpec((1, HALF_KEYS // per, res, per, BRANCH_W), index)
            return pl.BlockSpec((1, 1, res, HALF_KEYS, BRANCH_W), index)

        def out_index(i):
            b, rb, iq = self.decode(i)
            return b, iq, rb, 0, 0

        ins = [main(0), halo(1, prev), main(1), halo(1, nxt), halo(2, prev), main(2), halo(2, nxt),
               pl.BlockSpec((HEADS_PER_GROUP * Q_SUB, K_WIN), lambda i: (0, 0))]
        outs = [pl.BlockSpec((1, tq // per, res, per, BRANCH_W), out_index)] * 2
        return ins, outs

    def scratch(self):
        buf = lambda rows: pltpu.VMEM((self.res, rows, BRANCH_W), BF16)
        return [buf(self.tq + 2 * HALF_KEYS), buf(self.tq + 2 * HALF_KEYS)]


def _attn_gather(plan, refs, bufs):
    _, kp_ref, k_ref, kn_ref, vp_ref, v_ref, vn_ref = refs
    kbuf, vbuf = bufs

    def gather(dst, t, at, src):
        per = src.shape[3]
        for j in range(src.shape[1]):
            dst[t, at + j * per:at + (j + 1) * per, :] = src[0, j, t]

    for t in range(plan.res):
        gather(kbuf, t, 0, kp_ref)
        gather(kbuf, t, HALF_KEYS, k_ref)
        gather(kbuf, t, HALF_KEYS + plan.tq, kn_ref)
        gather(vbuf, t, 0, vp_ref)
        gather(vbuf, t, HALF_KEYS, v_ref)
        gather(vbuf, t, HALF_KEYS + plan.tq, vn_ref)


def _attn_blocks(plan, i, sub_blocks, q_ref, bufs, bias_ref, o_ref, lse_ref):
    kbuf, vbuf = bufs
    per_out = o_ref.shape[3]
    lane_h = _lane_head(Q_SUB)
    col = lax.broadcasted_iota(jnp.int32, (1, K_WIN), 1)
    _, _, iq = plan.decode(i)

    def scores(t, r0):
        if per_out >= Q_SUB:
            qb = q_ref[0, r0 // per_out, t, pl.ds(pl.multiple_of(r0 % per_out, Q_SUB), Q_SUB), :]
        else:
            qb = jnp.concatenate([q_ref[0, r0 // per_out + j, t] for j in range(Q_SUB // per_out)], axis=0)
        qs = jnp.concatenate(
            [jnp.where(lane_h == h, qb, jnp.zeros_like(qb)) for h in range(HEADS_PER_GROUP)], axis=0)
        return lax.dot_general(qs, kbuf[t, pl.ds(r0, K_WIN), :], (((1,), (1,)), ((), ())),
                               preferred_element_type=F32)

    def weigh(t, r0, s):
        kpos = iq * plan.tq + (r0 - HALF_KEYS) + col
        s = s + bias_ref[...] + jnp.where((kpos >= 0) & (kpos < plan.seq_m), 0.0, NEG_INF)
        m = jnp.max(s, axis=-1, keepdims=True)
        p = jnp.exp(s - m)
        l = jnp.sum(p, axis=-1, keepdims=True)
        return _dot(p.astype(BF16), vbuf[t, pl.ds(r0, K_WIN), :]), l, m

    def store(t, r0, pv, l, m):
        on = pv / l
        ls = m + jnp.log(l)
        o = jnp.zeros((Q_SUB, BRANCH_W), F32)
        lo = jnp.zeros((Q_SUB, BRANCH_W), F32)
        for h in range(HEADS_PER_GROUP):
            sel = lane_h == h
            o = jnp.where(sel, on[h * Q_SUB:(h + 1) * Q_SUB, :], o)
            lo = jnp.where(sel, ls[h * Q_SUB:(h + 1) * Q_SUB, :], lo)
        if per_out >= Q_SUB:
            at = pl.ds(pl.multiple_of(r0 % per_out, Q_SUB), Q_SUB)
            o_ref[0, r0 // per_out, t, at, :] = o.astype(o_ref.dtype)
            lse_ref[0, r0 // per_out, t, at, :] = lo
        for j in range(Q_SUB // per_out):
            o_ref[0, r0 // per_out + j, t] = o[j * per_out:(j + 1) * per_out, :].astype(o_ref.dtype)
            lse_ref[0, r0 // per_out + j, t] = lo[j * per_out:(j + 1) * per_out, :]

    scored = weighed = None
    for step in range(len(sub_blocks) + 2):
        if weighed is not None:
            store(*weighed)
        weighed = None if scored is None else (scored[0], scored[1]) + weigh(*scored)
        scored = sub_blocks[step] + (scores(*sub_blocks[step]),) if step < len(sub_blocks) else None
        if step < len(sub_blocks) + 1:
            yield


def _attn_bias(group):
    dil = ATTN_DILATIONS[group]
    i = np.arange(Q_SUB)[:, None]
    j = np.arange(K_WIN)[None, :]
    off = np.abs(j - HALF_KEYS - i).astype(np.float32)
    slopes = 2.0 ** (-8.0 * np.arange(1, N_ATTN_HEADS + 1, dtype=np.float32) / N_ATTN_HEADS)
    blocks = []
    for h in range(HEADS_PER_GROUP):
        slope = slopes[group * HEADS_PER_GROUP + h]
        blocks.append(np.where(off <= HALF_KEYS, -slope * (off * dil), NEG_INF))
    return jnp.asarray(np.concatenate(blocks, axis=0), F32)


def _softplus(y):
    return jnp.maximum(y, 0.0) + jnp.log(1.0 + jnp.exp(-jnp.abs(y)))


def _stack_heads(xb, lane_h):
    return jnp.concatenate(
        [jnp.where(lane_h == h, xb, jnp.zeros_like(xb)) for h in range(RWKV_HEADS)], axis=0)


def _nt(a, b):
    return lax.dot_general(a, b, (((1,), (1,)), ((), ())), preferred_element_type=F32)


N_RWKV_IN = 21
N_ATTN_IN = 8


def _mixer_kernel(*refs, batch, tc, plans):
    n_in = N_RWKV_IN + N_ATTN_IN * len(plans)
    n_out = 2 + 2 * len(plans)
    (rkvf_ref, rkvfh_ref, rkvb_ref, rkvbh_ref, latf_ref, latfh_ref, latb_ref, latbh_ref,
     mur_ref, mul_ref, w0_ref, wup_ref, a0_ref, aup_ref, kk_ref, ka_ref, rk_ref,
     lng_ref, lnb_ref, bdm_ref, bd1_ref) = refs[:N_RWKV_IN]
    attn_in = [refs[N_RWKV_IN + N_ATTN_IN * g:N_RWKV_IN + N_ATTN_IN * (g + 1)] for g in range(len(plans))]
    outf_ref, outb_ref = refs[n_in:n_in + 2]
    attn_out = [refs[n_in + 2 + 2 * g:n_in + 4 + 2 * g] for g in range(len(plans))]
    st_ref, r_s, k_s, v_s, kk_s, b_s, lw_s, o_s, y_s = refs[n_in + n_out:n_in + n_out + 9]
    attn_buf = [refs[n_in + n_out + 9 + 2 * g:n_in + n_out + 11 + 2 * g] for g in range(len(plans))]

    i = pl.program_id(0)
    c = RWKV_CHUNK
    unit = RWKV_UNIT
    n_units = tc // unit
    streams = [(d, b) for d in range(2) for b in range(batch)]
    ids = range(len(streams))
    back = [d == 1 for d, _ in streams]
    items = [(s, h) for s in ids for h in (reversed(range(unit // c)) if back[s] else range(unit // c))]
    wid = range(len(items))
    iback = [back[s] for s, _ in items]

    @pl.when(i == 0)
    def _():
        st_ref[...] = jnp.zeros_like(st_ref)

    lane_h = _lane_head(c)
    t_i = lax.broadcasted_iota(jnp.int32, (c, c), 0)
    s_i = lax.broadcasted_iota(jnp.int32, (c, c), 1)
    tau = lax.broadcasted_iota(jnp.int32, (c, BRANCH_W), 0)
    sig = lax.broadcasted_iota(jnp.int32, (c, BRANCH_W), 1) % c
    eye = sig == tau
    eye_b = jnp.where(eye, 1.0, 0.0).astype(BF16)
    tri_f = jnp.where(s_i <= t_i, 1.0, 0.0).astype(BF16)
    tri_b = jnp.where(s_i >= t_i, 1.0, 0.0).astype(BF16)
    strict = [(sig > tau) if bk else (sig < tau) for bk in iback]
    incl = [(sig >= tau) if bk else (sig <= tau) for bk in iback]
    bd1 = bd1_ref[...]
    bdm = bdm_ref[...]
    sm = lambda xb: _stack_heads(xb, lane_h)
    bf = lambda x: x.astype(BF16)
    top = lambda x: x[0:c]
    mid = lambda x: x[c:2 * c]
    rows_of = lambda x, k: x[k * unit:(k + 1) * unit]
    stack = lambda xs: jnp.concatenate(xs, axis=0)

    x_refs = [(rkvb_ref, rkvbh_ref, latb_ref, latbh_ref) if bk else (rkvf_ref, rkvfh_ref, latf_ref, latfh_ref)
              for bk in back]

    def unit_rows(j):
        start = [pl.multiple_of(((n_units - 1 - j) if bk else j) * unit, unit) for bk in back]
        return start, [pl.ds(st, unit) for st in start]

    def prepare(j, slot):
        start, rows = unit_rows(j)

        def mixed(s, x_ref, halo_ref, mu):
            d, b = streams[s]
            x = x_ref[b, rows[s], :]
            row = lax.broadcasted_iota(jnp.int32, x.shape, 0)
            if back[s]:
                near = x_ref[b, pl.ds(pl.multiple_of(jnp.minimum(start[s] + unit, tc - 8), 8), 8), :][0:1, :]
                edge = jnp.where(j == 0, jnp.where(i == 0, 0.0, halo_ref[b][0:1, :]), near)
                sh = jnp.where(row == unit - 1, edge, pltpu.roll(x, unit - 1, 0))
            else:
                near = x_ref[b, pl.ds(pl.multiple_of(jnp.maximum(start[s] - 8, 0), 8), 8), :][7:8, :]
                edge = jnp.where(j == 0, jnp.where(i == 0, 0.0, halo_ref[b][7:8, :]), near)
                sh = jnp.where(row == 0, edge, pltpu.roll(x, 1, 0))
            return x + (sh - x) * mu[d]

        xr = [mixed(s, x_refs[s][0], x_refs[s][1], mur_ref) for s in ids]
        xl = [mixed(s, x_refs[s][2], x_refs[s][3], mul_ref) for s in ids]
        th = [jnp.tanh(stack([xl[s] for s in ids if streams[s][0] == d])) for d in range(2)]
        yield
        w_log, a = [], []
        for d in range(2):
            xl_d = stack([xl[s] for s in ids if streams[s][0] == d])
            wl = -_softplus(-(w0_ref[d] + _dot(bf(th[d]), bf(wup_ref[d])))) - 0.5
            al = jax.nn.sigmoid(a0_ref[d] + _dot(bf(xl_d), bf(aup_ref[d])))
            w_log += [rows_of(wl, k) for k in range(batch)]
            a += [rows_of(al, k) for k in range(batch)]
        k = [xr[s][:, BRANCH_W:2 * BRANCH_W] for s in ids]
        kk = [k[s] * kk_ref[streams[s][0]] for s in ids]
        yield
        kk_sq = _dot(bf(stack([kk[s] * kk[s] for s in ids])), bd1)
        yield
        for s in ids:
            kks = kk[s] * lax.rsqrt(rows_of(kk_sq, s) + 1e-12)
            r_s[slot, s] = xr[s][:, 0:BRANCH_W]
            k_s[slot, s] = k[s] * (1.0 + (a[s] - 1.0) * ka_ref[streams[s][0]])
            v_s[slot, s] = xr[s][:, 2 * BRANCH_W:3 * BRANCH_W]
            kk_s[slot, s] = kks
            b_s[slot, s] = kks * a[s]
            lw_s[slot, s] = -jnp.exp(w_log[s])

    def advance(slot):
        ids = wid
        back = iback
        part = lambda ref: [ref[slot, s, h * c:(h + 1) * c, :] for s, h in items]
        lw, r, k2, v, kk, bq = part(lw_s), part(r_s), part(k_s), part(v_s), part(kk_s), part(b_s)

        def cumsum(s):
            lw_hi, lw_lo = _split_bf16(lw[s])
            tri = tri_b if back[s] else tri_f
            return _dot(tri, lw_hi) + _dot(tri, lw_lo)

        cumi = [cumsum(s) for s in ids]
        yield
        tot = [cumi[s][0:1, :] if back[s] else cumi[s][c - 1:c, :] for s in ids]
        e_m = [jnp.exp(-cumi[s]) for s in ids]
        e_t = [jnp.exp(tot[s] - cumi[s]) for s in ids]
        pt = [bf(-kk[s] * jnp.exp(cumi[s] - lw[s])) for s in ids]
        rt = [r[s] * jnp.exp(cumi[s]) for s in ids]
        pr = [stack([pt[s], bf(rt[s])]) for s in ids]
        ab = [_nt(pr[s], sm(bf(bq[s] * e_m[s]))) for s in ids]
        ak = [_nt(pr[s], sm(bf(k2[s] * e_m[s]))) for s in ids]
        yield
        a_pb = [jnp.where(strict[s], top(ab[s]), 0.0) for s in ids]
        a_rb = [bf(jnp.where(incl[s], mid(ab[s]), 0.0)) for s in ids]
        a_pk = [bf(jnp.where(strict[s], top(ak[s]), 0.0)) for s in ids]
        a_rk = [bf(jnp.where(incl[s], mid(ak[s]), 0.0)) for s in ids]
        l_b = [bf(_nt(eye_b, sm(bf(bq[s] * e_t[s])))) for s in ids]
        l_k = [bf(_nt(eye_b, sm(bf(k2[s] * e_t[s])))) for s in ids]
        yield

        tinv = [jnp.where(eye, 1.0, a_pb[s]) for s in ids]
        apow = [bf(a_pb[s]) for s in ids]
        a2 = [_dot(apow[s], sm(apow[s])) for s in ids]
        av = [_dot(stack([a_pk[s], a_rk[s], l_k[s]]), sm(bf(v[s]))) for s in ids]
        yield
        for _ in range(NEUMANN_DOUBLINGS - 1):
            apow = [bf(a2[s]) for s in ids]
            both = [_dot(stack([bf(tinv[s]), apow[s]]), sm(apow[s])) for s in ids]
            yield
            tinv = [tinv[s] + top(both[s]) for s in ids]
            a2 = [mid(both[s]) for s in ids]
        tinv = [tinv[s] + _dot(bf(tinv[s]), sm(bf(a2[s]))) for s in ids]
        yield
        tb = [bf(t) for t in tinv]

        ah = [_dot(tb[s], sm(pt[s])) for s in ids]
        uh = [_dot(tb[s], sm(bf(top(av[s])))) for s in ids]
        yield
        lhs = [stack([a_rb[s], l_b[s]]) for s in ids]
        o_a = [_dot(lhs[s], sm(bf(ah[s]))) for s in ids]
        o_u = [_dot(lhs[s], sm(bf(uh[s]))) for s in ids]
        yield
        rh = [rt[s] + top(o_a[s]) for s in ids]
        mt = [jnp.where(eye, jnp.exp(tot[s]), 0.0) + mid(o_a[s]) for s in ids]
        oh = [mid(av[s]) + top(o_u[s]) for s in ids]
        gt = [mid(o_u[s]) + av[s][2 * c:3 * c] for s in ids]

        bonus = _dot(bf(stack([r[w] * k2[w] * rk_ref[streams[items[w][0]][0]] for w in ids])), bd1) * stack(v)
        for w, (s, h) in enumerate(items):
            y_s[slot, s, h * c:(h + 1) * c, :] = bonus[w * c:(w + 1) * c]
        state = [st_ref[s] for s in range(len(streams))]
        for step in range(unit // c):
            now = [w for w in ids if w % (unit // c) == step]
            out = {w: _dot(stack([bf(rh[w]), bf(mt[w])]), sm(bf(state[items[w][0]]))) for w in now}
            yield
            for w in now:
                s, h = items[w]
                state[s] = mid(out[w]) + gt[w]
                o_s[slot, s, h * c:(h + 1) * c, :] = top(out[w]) + oh[w]
        for s in range(len(streams)):
            st_ref[s] = state[s]

    def finish(j, slot):
        _, rows = unit_rows(j)
        o = stack([o_s[slot, s] for s in ids])
        mean = _dot_split_lhs(o, bdm)
        yield
        xc = o - mean
        var = _dot(bf(xc * xc), bdm)
        yield
        y = xc * lax.rsqrt(var + GN_EPS) * lng_ref[...] + lnb_ref[...] + stack([y_s[slot, s] for s in ids])
        for s, (d, b) in enumerate(streams):
            out_ref = outb_ref if d == 1 else outf_ref
            out_ref[b, rows[s], :] = rows_of(y, s)

    def interleave(*phases):
        live = list(phases)
        while live:
            for phase in list(live):
                if next(phase, "done") == "done":
                    live.remove(phase)

    def attend(j):
        phases = []
        for plan, ins, outs, bufs in zip(plans, attn_in, attn_out, attn_buf):
            assert plan.tq % (Q_SUB * n_units) == 0
            blocks = [(t, pl.multiple_of((j + n_units * u) * Q_SUB, Q_SUB))
                      for t in range(plan.res) for u in range(plan.tq // Q_SUB // n_units)]
            phases.append(_attn_blocks(plan, i, blocks, ins[0], bufs, ins[7], outs[0], outs[1]))
        return phases

    o_s[0] = jnp.zeros(o_s.shape[1:], F32)
    y_s[0] = jnp.zeros(y_s.shape[1:], F32)
    for plan, ins, bufs in zip(plans, attn_in, attn_buf):
        _attn_gather(plan, ins[:7], bufs)
    interleave(prepare(0, 0))

    def pipeline(j, carry):
        interleave(finish(jnp.maximum(j - 1, 0), 0),
                   advance(0),
                   prepare(jnp.minimum(j + 1, n_units - 1), 0),
                   *attend(j))
        return carry

    lax.fori_loop(0, n_units, pipeline, 0)
    interleave(finish(n_units - 1, 0))


def _mixers(qkv, rkv, lat, lp, batch, seq):
    tc = min(RWKV_ROWS, seq)
    nt = seq // tc
    hb = tc // 8
    last_halo = seq // 8 - 1
    fwd = lambda i: i
    bwd = lambda i: nt - 1 - i
    fwd_halo = lambda i: jnp.maximum(i * hb - 1, 0)
    bwd_halo = lambda i: jnp.minimum((bwd(i) + 1) * hb, last_halo)
    full3 = lambda shape: pl.BlockSpec(shape, lambda i: (0, 0, 0))
    full2 = lambda shape: pl.BlockSpec(shape, lambda i: (0, 0))
    n_streams = 2 * batch
    plans = [_AttnPlan(g, batch, seq, nt) for g in range(N_GROUPS)]
    in_specs = [
        pl.BlockSpec((batch, tc, RKV_W), lambda i: (0, fwd(i), 0)),
        pl.BlockSpec((batch, 8, RKV_W), lambda i: (0, fwd_halo(i), 0)),
        pl.BlockSpec((batch, tc, RKV_W), lambda i: (0, bwd(i), 0)),
        pl.BlockSpec((batch, 8, RKV_W), lambda i: (0, bwd_halo(i), 0)),
        pl.BlockSpec((batch, tc, LAT_W), lambda i: (0, fwd(i), 0)),
        pl.BlockSpec((batch, 8, LAT_W), lambda i: (0, fwd_halo(i), 0)),
        pl.BlockSpec((batch, tc, LAT_W), lambda i: (0, bwd(i), 1)),
        pl.BlockSpec((batch, 8, LAT_W), lambda i: (0, bwd_halo(i), 1)),
        full3((2, 1, RKV_W)), full3((2, 1, LAT_W)), full3((2, 1, BRANCH_W)),
        full3((2, LAT_W, BRANCH_W)), full3((2, 1, BRANCH_W)), full3((2, LAT_W, BRANCH_W)),
        full3((2, 1, BRANCH_W)), full3((2, 1, BRANCH_W)), full3((2, 1, BRANCH_W)),
        full2((1, BRANCH_W)), full2((1, BRANCH_W)),
        full2((BRANCH_W, BRANCH_W)), full2((BRANCH_W, BRANCH_W)),
    ]
    assert len(in_specs) == N_RWKV_IN
    operands = [rkv, rkv, rkv, rkv, lat, lat, lat, lat, lp["mu_rkv"], lp["mu_lat"], lp["w0"], lp["w_up_pad"],
                lp["a0"], lp["a_up_pad"], lp["k_k"], lp["k_a"], lp["r_k"], lp["ln_g"], lp["ln_b"],
                lp["bd_mean"], lp["bd_ones"]]
    rwkv_out = jax.ShapeDtypeStruct((batch, seq, BRANCH_W), F32)
    out_specs = [pl.BlockSpec((batch, tc, BRANCH_W), lambda i: (0, fwd(i), 0)),
                 pl.BlockSpec((batch, tc, BRANCH_W), lambda i: (0, bwd(i), 0))]
    out_shape = [rwkv_out, rwkv_out]
    scratch = ([pltpu.VMEM((n_streams, RWKV_CHUNK, BRANCH_W), F32)]
               + [pltpu.VMEM((1, n_streams, RWKV_UNIT, BRANCH_W), F32)] * 8)
    for plan in plans:
        ins, outs = plan.specs()
        assert len(ins) == N_ATTN_IN
        view = qkv.reshape(batch, seq // PERM_ROWS, plan.dil, plan.per, QKV_W)
        in_specs += ins
        operands += [view] * 7 + [_attn_bias(plan.group)]
        out_specs += outs
        dims = (batch, seq // PERM_ROWS, plan.dil, plan.per, BRANCH_W)
        out_shape += [jax.ShapeDtypeStruct(dims, BF16), jax.ShapeDtypeStruct(dims, F32)]
        scratch += plan.scratch()
    res = pl.pallas_call(
        functools.partial(_mixer_kernel, batch=batch, tc=tc, plans=plans),
        grid=(nt,),
        in_specs=in_specs,
        out_specs=out_specs,
        out_shape=out_shape,
        scratch_shapes=scratch,
        compiler_params=_params("arbitrary"),
        name="mixers",
    )(*operands)
    attn = [(res[2 + 2 * g], res[3 + 2 * g]) for g in range(N_GROUPS)]
    return attn, res[0], res[1]


def _silu(z):
    return z * jax.nn.sigmoid(z)


def _merge_kernel(x_ref, g_ref, o0_ref, l0_ref, o1_ref, l1_ref, o2_ref, l2_ref, z_ref,
                  ubp_ref, bc_ref, ubn_ref, yd0_ref, yd1_ref,
                  unperm_ref, poolw_ref, pools_ref, sgg_ref, sgw_ref, sgb_ref, wg_ref, wb_ref, wo_ref,
                  out_ref, *, tm, seq):
    i = pl.program_id(1)
    n_tiles = seq // tm
    x = x_ref[0]
    ms = jnp.mean(x * x, axis=-1, keepdims=True)
    h = (x * lax.rsqrt(ms + EPS) * g_ref[...]).astype(BF16)
    z = z_ref[0]
    lane_g = _lane_head(tm)
    parts = [slice(p * PERM_ROWS, (p + 1) * PERM_ROWS) for p in range(tm // PERM_ROWS)]

    def natural(ref, group, exact):
        tiles = []
        for t in range(tm // PERM_ROWS):
            if group == 0:
                tiles.append(ref[0, t, 0].astype(F32))
                continue
            v = jnp.concatenate([ref[0, t, r] for r in range(ATTN_DILATIONS[group])], axis=0)
            unperm = unperm_ref[group - 1]
            tiles.append(_dot_split_lhs_rhs(unperm, v) if exact else _dot(unperm, v))
        return jnp.concatenate(tiles, axis=0)

    l0, l1, l2 = natural(l0_ref, 0, True), natural(l1_ref, 1, True), natural(l2_ref, 2, True)
    o0, o1, o2 = natural(o0_ref, 0, False), natural(o1_ref, 1, False), natural(o2_ref, 2, False)
    gate_logits = [[_dot(h[p], wg_ref[:, n * D_MODEL:(n + 1) * D_MODEL]) for n in range(N_BRANCH)]
                   for p in parts]
    mx = jnp.maximum(jnp.maximum(l0, l1), l2)
    e0, e1, e2 = jnp.exp(l0 - mx), jnp.exp(l1 - mx), jnp.exp(l2 - mx)
    y_a = (e0 * o0 + e1 * o1 + e2 * o2) / (e0 + e1 + e2)

    bc = bc_ref[0]
    u = bc[:, 0:BRANCH_W]
    prev = jnp.where(i == 0, 0.0, ubp_ref[0][:, 0:BRANCH_W])
    nxt = jnp.where(i == n_tiles - 1, 0.0, ubn_ref[0][:, 0:BRANCH_W])
    ext = jnp.concatenate([prev, u, nxt], axis=0)
    n_ext = tm + 2 * POOL_HALO
    f1 = ext + pltpu.roll(ext, 1, 0)
    f2 = pltpu.roll(f1, 1, 0) + pltpu.roll(f1, n_ext - 1, 0)
    f4 = pltpu.roll(f2, 2, 0) + pltpu.roll(f2, n_ext - 2, 0)
    f8 = pltpu.roll(f4, 4, 0) + pltpu.roll(f4, n_ext - 4, 0)
    lane_e = _lane_head(n_ext)
    win = jnp.where(lane_e == 0, f1, jnp.where(lane_e == 1, f2, jnp.where(lane_e == 2, f4, f8)))
    win = win[POOL_HALO:POOL_HALO + tm, :]
    pos = i * tm + lax.broadcasted_iota(jnp.int32, (tm, BRANCH_W), 0)
    half = jnp.where(lane_g == 0, POOL_HALF[0],
                     jnp.where(lane_g == 1, POOL_HALF[1], jnp.where(lane_g == 2, POOL_HALF[2], POOL_HALF[3])))
    cnt = (jnp.minimum(pos + half, seq) - jnp.maximum(pos - half, 0)).astype(F32)
    y_b = _dot((win / cnt - u).astype(BF16), poolw_ref[...]) * pools_ref[...]

    u_c = bc[:, BRANCH_W:2 * BRANCH_W]
    v_c = bc[:, 2 * BRANCH_W:3 * BRANCH_W]
    msv = jnp.mean(v_c * v_c, axis=-1, keepdims=True)
    v_n = (v_c * lax.rsqrt(msv + EPS) * sgg_ref[...]).astype(BF16)
    lane_c = _lane_head(CHUNK)
    sv_chunks = []
    for ch in range(tm // CHUNK):
        vch = v_n[ch * CHUNK:(ch + 1) * CHUNK, :]
        v_groups = jnp.concatenate([jnp.where(lane_c == g, vch, jnp.zeros_like(vch)) for g in range(SG_GROUPS)],
                                   axis=0)
        sv_chunks.append(_dot(sgw_ref[...], v_groups) + sgb_ref[...])
    y_c = u_c * jnp.concatenate(sv_chunks, axis=0)

    y_d = yd0_ref[0] + yd1_ref[0]

    ys = [(y * _silu(z[:, n * BRANCH_W:(n + 1) * BRANCH_W])).astype(BF16)
          for n, y in enumerate((y_a, y_b, y_c, y_d))]
    branch = [[_dot(ys[n][p], wb_ref[n]) for n in range(N_BRANCH)] for p in parts]
    merged = []
    for gl, br in zip(gate_logits, branch):
        acc = jax.nn.sigmoid(gl[0]) * br[0]
        for n in range(1, N_BRANCH):
            acc = acc + jax.nn.sigmoid(gl[n]) * br[n]
        merged.append(acc.astype(BF16))
    for p, m in zip(parts, merged):
        out_ref[0, p, :] = x[p] + _dot(m, wo_ref[...])


def _merge(x, attn, z, bc, yd0, yd1, lp, batch, seq):
    tm = MERGE_ROWS
    hb = tm // POOL_HALO
    last_halo = seq // POOL_HALO - 1
    row = lambda w: pl.BlockSpec((1, tm, w), lambda b, i: (b, i, 0))
    const2 = lambda shape: pl.BlockSpec(shape, lambda b, i: (0, 0))
    const3 = lambda shape: pl.BlockSpec(shape, lambda b, i: (0, 0, 0))
    assert tm % PERM_ROWS == 0
    in_specs = [row(D_MODEL), const2((1, D_MODEL))]
    for dil in ATTN_DILATIONS:
        slab = pl.BlockSpec((1, tm // PERM_ROWS, dil, PERM_ROWS // dil, BRANCH_W), lambda b, i: (b, i, 0, 0, 0))
        in_specs += [slab, slab]
    in_specs += [
        row(Z_W),
        pl.BlockSpec((1, POOL_HALO, BRANCH_W), lambda b, i: (b, jnp.maximum(i * hb - 1, 0), 0)),
        row(BC_W),
        pl.BlockSpec((1, POOL_HALO, BRANCH_W), lambda b, i: (b, jnp.minimum((i + 1) * hb, last_halo), 0)),
        row(BRANCH_W), row(BRANCH_W),
        const3((N_GROUPS - 1, PERM_ROWS, PERM_ROWS)),
        const2((BRANCH_W, BRANCH_W)), const2((1, BRANCH_W)), const2((1, BRANCH_W)),
        const2((CHUNK, SG_GROUPS * CHUNK)), const2((CHUNK, BRANCH_W)),
        const2((D_MODEL, N_BRANCH * D_MODEL)), const3((N_BRANCH, BRANCH_W, D_MODEL)),
        const2((D_MODEL, D_MODEL)),
    ]
    flat_attn = [a for pair in attn for a in pair]
    return pl.pallas_call(
        functools.partial(_merge_kernel, tm=tm, seq=seq),
        grid=(batch, seq // tm),
        in_specs=in_specs,
        out_specs=row(D_MODEL),
        out_shape=jax.ShapeDtypeStruct((batch, seq, D_MODEL), F32),
        compiler_params=_params("parallel", "parallel"),
        name="merge",
    )(x, lp["norm_g"], *flat_attn, z, bc, bc, bc, yd0, yd1, _residue_perms(True),
      lp["pool_w_bd"], lp["pool_scale"], lp["sg_norm_g"], lp["sg_w"], lp["sg_bias"],
      lp["w_gate"], lp["w_branch"], lp["w_out"])


def _block_diag(blocks):
    n, w, _ = blocks.shape
    eye = np.eye(n, dtype=np.float32)
    return jnp.einsum("gij,gh->gihj", blocks, eye).reshape(n * w, n * w)


def _layer_params(l, norm_g, w_in, q_norm_g, k_norm_g, pool_w, pool_scale, sg_norm_g, sg_w, sg_b,
                  mu_rkv, mu_lat, w0, w_up, a0, a_up, k_k, k_a, r_k, ln_g, ln_b, w_branch, w_out):
    w = w_in[l]
    seg = lambda lo, width: w[:, lo:lo + width]
    w_small = jnp.concatenate([
        seg(0, QKV_W), seg(3840, RKV_W), seg(4608, 2 * LAT_W), seg(2560, BRANCH_W), seg(3072, 2 * BRANCH_W),
        seg(2304, BRANCH_W), seg(2816, BRANCH_W), seg(3584, BRANCH_W), seg(4864, BRANCH_W)], axis=1)
    head_blocks = np.kron(np.eye(RWKV_HEADS, dtype=np.float32), np.ones((HEAD_DIM, HEAD_DIM), np.float32))
    zeros_lat = jnp.zeros((2, LAT_W // 2, BRANCH_W), F32)
    vec3 = lambda p: p[l][:, None, :]
    return {
        "norm_g": norm_g[l][None, :],
        "w_small": w_small.astype(BF16),
        "w_gate": w[:, PROJ_SMALL_W:].astype(BF16),
        "q_gain": jnp.tile(q_norm_g[l], HEADS_PER_GROUP)[None, :],
        "k_gain": jnp.tile(k_norm_g[l], HEADS_PER_GROUP)[None, :],
        "bd_mean": jnp.asarray(head_blocks / HEAD_DIM, BF16),
        "bd_ones": jnp.asarray(head_blocks, BF16),
        "pool_w_bd": _block_diag(pool_w[l]).astype(BF16),
        "pool_scale": pool_scale[l][None, :],
        "sg_norm_g": sg_norm_g[l][None, :],
        "sg_w": jnp.concatenate([sg_w[l][g] for g in range(SG_GROUPS)], axis=1).astype(BF16),
        "sg_bias": jnp.repeat(sg_b[l].T, HEAD_DIM, axis=1),
        "mu_rkv": vec3(mu_rkv), "mu_lat": vec3(mu_lat), "w0": vec3(w0), "a0": vec3(a0),
        "k_k": vec3(k_k), "k_a": vec3(k_a), "r_k": vec3(r_k),
        "w_up_pad": jnp.concatenate([w_up[l], zeros_lat], axis=1),
        "a_up_pad": jnp.concatenate([zeros_lat, a_up[l]], axis=1),
        "ln_g": ln_g[l][None, :], "ln_b": ln_b[l][None, :],
        "w_branch": w_branch[l].astype(BF16),
        "w_out": w_out[l].astype(BF16),
    }


def _layer(x, lp):
    batch, seq, _ = x.shape
    qkv, rkv, lat, bc, z = _proj(x.reshape(batch * seq, D_MODEL), lp)
    rkv = rkv.reshape(batch, seq, RKV_W)
    lat = lat.reshape(batch, seq, 2 * LAT_W)
    attn, yd0, yd1 = _mixers(qkv, rkv, lat, lp, batch, seq)
    return _merge(x, attn, z.reshape(batch, seq, Z_W), bc.reshape(batch, seq, BC_W), yd0, yd1, lp, batch, seq)


def kernel(x_prompt, x_sample, norm_g, w_in, q_norm_g, k_norm_g, pool_w, pool_scale, sg_norm_g, sg_w, sg_b,
           mu_rkv, mu_lat, w0, w_up, a0, a_up, k_k, k_a, r_k, ln_g, ln_b, w_branch, w_out):
    weights = (norm_g, w_in, q_norm_g, k_norm_g, pool_w, pool_scale, sg_norm_g, sg_w, sg_b,
               mu_rkv, mu_lat, w0, w_up, a0, a_up, k_k, k_a, r_k, ln_g, ln_b, w_branch, w_out)
    layers = [_layer_params(l, *weights) for l in range(norm_g.shape[0])]
    outs = []
    for x in (x_prompt, x_sample):
        for lp in layers:
            x = _layer(x, lp)
        outs.append(x)
    return tuple(outs)
```
